```python
import math
import jax
import jax.numpy as jnp
from jax import lax
import numpy as np


D_MODEL = 1024
BATCH = 8
SEQ = 4096
DEPTH = 2

GRID_W = 64
CTX_LEN = 256
N_Q_HEADS = 8
N_KV_HEADS = 2
GQA_GROUP = N_Q_HEADS // N_KV_HEADS
HEAD_DIM = D_MODEL // N_Q_HEADS
N_FREQ = HEAD_DIM // 4
ROPE_THETA = 10000.0
Q_BLOCK = 128
CONV_DIM = D_MODEL
CONV_K = 31
RWKV_HEAD = 64
RWKV_HEADS = D_MODEL // RWKV_HEAD
RWKV_DIM = RWKV_HEADS * RWKV_HEAD
DECAY_LORA = 64
ICLR_LORA = 64
GATE_LORA = 128
SHIFT_K = 3
DECAY_SCALE = math.exp(-0.5)
D_FF = 256 * ((8 * D_MODEL // 3 + 255) // 256)
FFN_CONV_K = 3
N_BRANCH = 3
N_MOD = 6
EPS = 1e-6
LN_EPS = 1e-5
GN_EPS = RWKV_HEAD * 1e-5
IN_WIDTHS = (N_Q_HEADS * HEAD_DIM, N_KV_HEADS * HEAD_DIM, N_KV_HEADS * HEAD_DIM, 2 * CONV_DIM, 3 * RWKV_DIM,
             DECAY_LORA, DECAY_LORA, ICLR_LORA, ICLR_LORA, GATE_LORA, N_BRANCH * D_MODEL)
IN_SPLITS = tuple(int(s) for s in np.cumsum(IN_WIDTHS)[:-1])
N_IN = int(sum(IN_WIDTHS))

kernel_name = 'hybrid_gqa_conformer_rwkv7_dit_block'


def rms_norm(x, g, eps=EPS):
    xf = x.astype(jnp.float32)
    y = xf * lax.rsqrt(jnp.mean(xf * xf, axis=-1, keepdims=True) + eps)
    return (y * g.astype(jnp.float32)).astype(x.dtype)


def layer_norm(x, g, b, eps=LN_EPS):
    xf = x.astype(jnp.float32)
    mu = jnp.mean(xf, axis=-1, keepdims=True)
    var = jnp.mean(jnp.square(xf - mu), axis=-1, keepdims=True)
    y = (xf - mu) * lax.rsqrt(var + eps)
    return (y * g.astype(jnp.float32) + b.astype(jnp.float32)).astype(x.dtype)


def dwconv(x, w):
    return lax.conv_general_dilated(x, w[:, None, :].astype(x.dtype), window_strides=(1,), padding='SAME',
                                    dimension_numbers=('NWC', 'WIO', 'NWC'), feature_group_count=x.shape[-1])


def axial_rope_tables(rows):
    row = jnp.repeat(jnp.arange(rows, dtype=jnp.float32), GRID_W)
    col = jnp.tile(jnp.arange(GRID_W, dtype=jnp.float32), rows)
    inv_freq = ROPE_THETA ** (-jnp.arange(N_FREQ, dtype=jnp.float32) / N_FREQ)
    ang_r = row[:, None] * inv_freq
    ang_c = col[:, None] * inv_freq
    return (jnp.cos(ang_r), jnp.sin(ang_r), jnp.cos(ang_c), jnp.sin(ang_c))


def _rotate(x, cos, sin):
    x1, x2 = jnp.split(x, 2, axis=-1)
    cos = cos[None, :, None, :].astype(x.dtype)
    sin = sin[None, :, None, :].astype(x.dtype)
    return jnp.concatenate([x1 * cos - x2 * sin, x2 * cos + x1 * sin], axis=-1)


def apply_axial_rope(x, rope):
    cos_r, sin_r, cos_c, sin_c = rope
    xr, xc = jnp.split(x, 2, axis=-1)
    return jnp.concatenate([_rotate(xr, cos_r, sin_r), _rotate(xc, cos_c, sin_c)], axis=-1)


def gqa_attend(q, k, v):
    s = jnp.einsum('bqkgd,bskd->bkgqs', q, k).astype(jnp.float32) * (HEAD_DIM ** -0.5)
    p = jax.nn.softmax(s, axis=-1).astype(v.dtype)
    return jnp.einsum('bkgqs,bskd->bqkgd', p, v)


def attention_branch(qc, kc, vc, qx, kx, vx, lp, rope, need_ctx):
    B, S, _ = qx.shape
    L = kc.shape[1]
    heads = lambda t, n: t.reshape(t.shape[0], t.shape[1], n, HEAD_DIM)
    kc = rms_norm(heads(kc, N_KV_HEADS), lp['k_norm'])
    vc = heads(vc, N_KV_HEADS)
    qx = apply_axial_rope(rms_norm(heads(qx, N_Q_HEADS), lp['q_norm']), rope)
    kx = apply_axial_rope(rms_norm(heads(kx, N_KV_HEADS), lp['k_norm']), rope)
    k_all = jnp.concatenate([kc, kx], axis=1)
    v_all = jnp.concatenate([vc, heads(vx, N_KV_HEADS)], axis=1)
    nb = S // Q_BLOCK
    q_blocks = qx.reshape(B, nb, Q_BLOCK, N_KV_HEADS, GQA_GROUP, HEAD_DIM).swapaxes(0, 1)
    out_x = lax.map(lambda qb: gqa_attend(qb, k_all, v_all), q_blocks)
    out_x = out_x.swapaxes(0, 1).reshape(B, S, N_Q_HEADS * HEAD_DIM)
    if not need_ctx:
        return None, out_x
    qc = rms_norm(heads(qc, N_Q_HEADS), lp['q_norm']).reshape(B, L, N_KV_HEADS, GQA_GROUP, HEAD_DIM)
    out_c = gqa_attend(qc, kc, vc).reshape(B, L, N_Q_HEADS * HEAD_DIM)
    return out_c, out_x


def conformer_conv(u, lp):
    a, b = jnp.split(u, 2, axis=-1)
    y = dwconv(a * jax.nn.sigmoid(b), lp['conv_w']) + lp['conv_b']
    return jax.nn.silu(layer_norm(y, lp['conv_ln_g'], lp['conv_ln_b']))


def l2_normalize(x):
    xf = x.astype(jnp.float32)
    return (xf * lax.rsqrt(jnp.maximum(jnp.sum(xf * xf, axis=-1, keepdims=True), 1e-12))).astype(x.dtype)


def rwkv_prepare(rkv, lw, la, lp):
    B, T, _ = rkv.shape
    hd = lambda t: t.reshape(B, T, RWKV_HEADS, RWKV_HEAD)
    r, k, v = jnp.split(dwconv(rkv, lp['shift_w']), 3, axis=-1)
    kap = l2_normalize(hd(k * lp['k_k']))
    dirs = []
    for d in range(2):
        w = jnp.exp(-DECAY_SCALE * jax.nn.sigmoid(lp['decay_w0'][d] + jnp.tanh(lw[d]) @ lp['decay_up'][d]))
        a = jax.nn.sigmoid(lp['iclr_a0'][d] + la[d] @ lp['iclr_up'][d])
        k_d = k * (1.0 + (a - 1.0) * lp['k_a'])
        dirs.append((hd(w), hd(k_d), hd(a) * kap))
    return hd(r), hd(v), kap, dirs


def wkv_scan(r, w, k, v, kap, b, s0, reverse, emit):
    xs = tuple(jnp.moveaxis(t, 1, 0) for t in (r, w, k, v, kap, b))

    def step(S, inp):
        r_t, w_t, k_t, v_t, kap_t, b_t = inp
        sa = jnp.einsum('bhvk,bhk->bhv', S, kap_t)
        S = S * w_t[:, :, None, :] - sa[..., :, None] * b_t[..., None, :] + v_t[..., :, None] * k_t[..., None, :]
        y = jnp.einsum('bhvk,bhk->bhv', S, r_t) if emit else None
        return S, y

    S, ys = lax.scan(step, s0, xs, reverse=reverse)
    if not emit:
        return None, S
    return jnp.moveaxis(ys, 0, 1).astype(r.dtype), S


def rwkv_output(y, r, v, dirs, lg, lp):
    B, T = y.shape[0], y.shape[1]
    yf = y.astype(jnp.float32)
    mu = jnp.mean(yf, axis=-1, keepdims=True)
    var = jnp.mean(jnp.square(yf - mu), axis=-1, keepdims=True)
    yn = ((yf - mu) * lax.rsqrt(var + GN_EPS)).reshape(B, T, RWKV_DIM)
    yn = (yn * lp['wkv_gn_g'] + lp['wkv_gn_b']).astype(y.dtype)
    bonus = (jnp.sum(r * dirs[0][1] * lp['r_k'], axis=-1, keepdims=True)
             + jnp.sum(r * dirs[1][1] * lp['r_k'], axis=-1, keepdims=True)) * v
    g = jax.nn.sigmoid(lg) @ lp['gate_up']
    return (yn + bonus.reshape(B, T, RWKV_DIM)) * g


def rwkv_branch(prep_c, prep_x, lg_c, lg_x, lp, need_ctx):
    r_c, v_c, kap_c, dirs_c = prep_c
    r_x, v_x, kap_x, dirs_x = prep_x
    B = r_x.shape[0]
    s0 = jnp.zeros((B, RWKV_HEADS, RWKV_HEAD, RWKV_HEAD), jnp.float32)
    ys_c, ys_x = [], []
    for d, rev in enumerate((False, True)):
        w_c, k_c, b_c = dirs_c[d]
        y_c, s_ctx = wkv_scan(r_c, w_c, k_c, v_c, kap_c, b_c, s0, rev, need_ctx)
        w_x, k_x, b_x = dirs_x[d]
        y_x, _ = wkv_scan(r_x, w_x, k_x, v_x, kap_x, b_x, s_ctx, rev, True)
        ys_c.append(y_c)
        ys_x.append(y_x)
    out_x = rwkv_output(ys_x[0] + ys_x[1], r_x, v_x, dirs_x, lg_x, lp)
    if not need_ctx:
        return None, out_x
    out_c = rwkv_output(ys_c[0] + ys_c[1], r_c, v_c, dirs_c, lg_c, lp)
    return out_c, out_x


def gated_merge(gates, att, conv, rwkv, lp):
    ga, gc, gr = jnp.split(jax.nn.sigmoid(gates), N_BRANCH, axis=-1)
    m = ga * (att @ lp['w_attn_o']) + gc * (conv @ lp['w_conv_o']) + gr * (rwkv @ lp['w_rwkv_o'])
    return m @ lp['w_out']


def conv_ffn(h, lp):
    z = dwconv(h @ lp['w_ffn_up'], lp['ffn_conv_w'])
    gate, val = jnp.split(z, 2, axis=-1)
    return (jax.nn.silu(gate) * val) @ lp['w_ffn_down']


def modulated(t, g, shift, scale):
    return rms_norm(t, g) * (1.0 + scale) + shift


def trunk_layer(x, xc, c, c_ctx, rope, lp, last):
    mod_x = jnp.split((jax.nn.silu(c) @ lp['w_mod'] + lp['b_mod'])[:, None, :], N_MOD, axis=-1)
    mod_c = jnp.split((jax.nn.silu(c_ctx) @ lp['w_mod'] + lp['b_mod'])[None, None, :], N_MOD, axis=-1)
    need_ctx = not last
    hx = modulated(x, lp['g_pre_mix'], mod_x[0], mod_x[1])
    hc = modulated(xc, lp['g_pre_mix'], mod_c[0], mod_c[1])
    qx, kx, vx, glux, rkvx, wfx, wbx, afx, abx, lgx, gatex = jnp.split(hx @ lp['w_in'], IN_SPLITS, axis=-1)
    qc, kc, vc, gluc, rkvc, wfc, wbc, afc, abc, lgc, gatec = jnp.split(hc @ lp['w_in'], IN_SPLITS, axis=-1)
    att_c, att_x = attention_branch(qc, kc, vc, qx, kx, vx, lp, rope, need_ctx)
    conv_x = conformer_conv(glux, lp)
    prep_x = rwkv_prepare(rkvx, (wfx, wbx), (afx, abx), lp)
    prep_c = rwkv_prepare(rkvc, (wfc, wbc), (afc, abc), lp)
    rw_c, rw_x = rwkv_branch(prep_c, prep_x, lgc, lgx, lp, need_ctx)
    x = x + mod_x[2] * rms_norm(gated_merge(gatex, att_x, conv_x, rw_x, lp), lp['g_post_mix'])
    hx = modulated(x, lp['g_pre_ffn'], mod_x[3], mod_x[4])
    x = x + mod_x[5] * rms_norm(conv_ffn(hx, lp), lp['g_post_ffn'])
    if last:
        return x, None
    conv_c = conformer_conv(gluc, lp)
    xc = xc + mod_c[2] * rms_norm(gated_merge(gatec, att_c, conv_c, rw_c, lp), lp['g_post_mix'])
    hc = modulated(xc, lp['g_pre_ffn'], mod_c[3], mod_c[4])
    xc = xc + mod_c[5] * rms_norm(conv_ffn(hc, lp), lp['g_post_ffn'])
    return x, xc


def setup_inputs(seed: int = 0) -> dict:
    key = jax.random.key(seed)
    ks = iter(jax.random.split(key, 48))

    def nrm(shape, scale):
        return scale * jax.random.normal(next(ks), shape, jnp.float32)

    def gain(shape):
        return 1.0 + nrm(shape, 0.02)

    D = D_MODEL
    Ld = DEPTH
    sd = D ** -0.5
    return {
        'x': nrm((BATCH, SEQ, D), 1.0),
        'c': nrm((BATCH, D), 1.0),
        'ctx': nrm((BATCH, CTX_LEN, D), 1.0),
        'c_ctx': nrm((D,), 1.0),
        'w_mod': nrm((Ld, D, N_MOD * D), sd),
        'b_mod': nrm((Ld, N_MOD * D), 0.01),
        'g_pre_mix': gain((Ld, D)),
        'g_post_mix': gain((Ld, D)),
        'g_pre_ffn': gain((Ld, D)),
        'g_post_ffn': gain((Ld, D)),
        'w_in': nrm((Ld, D, N_IN), sd),
        'q_norm': gain((Ld, HEAD_DIM)),
        'k_norm': gain((Ld, HEAD_DIM)),
        'w_attn_o': nrm((Ld, N_Q_HEADS * HEAD_DIM, D), (N_Q_HEADS * HEAD_DIM) ** -0.5),
        'conv_w': nrm((Ld, CONV_K, CONV_DIM), CONV_K ** -0.5),
        'conv_b': nrm((Ld, CONV_DIM), 0.01),
        'conv_ln_g': gain((Ld, CONV_DIM)),
        'conv_ln_b': nrm((Ld, CONV_DIM), 0.01),
        'w_conv_o': nrm((Ld, CONV_DIM, D), CONV_DIM ** -0.5),
        'shift_w': nrm((Ld, SHIFT_K, 3 * RWKV_DIM), 0.1) + jnp.array([0.0, 1.0, 0.0], jnp.float32)[None, :, None],
        'decay_w0': nrm((Ld, 2, RWKV_DIM), 0.5),
        'decay_up': nrm((Ld, 2, DECAY_LORA, RWKV_DIM), 0.1),
        'iclr_a0': nrm((Ld, 2, RWKV_DIM), 0.5),
        'iclr_up': nrm((Ld, 2, ICLR_LORA, RWKV_DIM), 0.1),
        'gate_up': nrm((Ld, GATE_LORA, RWKV_DIM), GATE_LORA ** -0.5),
        'k_k': 1.0 + nrm((Ld, RWKV_DIM), 0.1),
        'k_a': 1.0 + nrm((Ld, RWKV_DIM), 0.1),
        'r_k': nrm((Ld, RWKV_HEADS, RWKV_HEAD), 0.1),
        'wkv_gn_g': gain((Ld, RWKV_DIM)),
        'wkv_gn_b': nrm((Ld, RWKV_DIM), 0.01),
        'w_rwkv_o': nrm((Ld, RWKV_DIM, D), RWKV_DIM ** -0.5),
        'w_out': nrm((Ld, D, D), sd),
        'w_ffn_up': nrm((Ld, D, 2 * D_FF), sd),
        'ffn_conv_w': nrm((Ld, FFN_CONV_K, 2 * D_FF), FFN_CONV_K ** -0.5),
        'w_ffn_down': nrm((Ld, D_FF, D), D_FF ** -0.5),
    }


def reference(x, c, ctx, c_ctx, w_mod, b_mod, g_pre_mix, g_post_mix, g_pre_ffn, g_post_ffn, w_in, q_norm, k_norm,
              w_attn_o, conv_w, conv_b, conv_ln_g, conv_ln_b, w_conv_o, shift_w, decay_w0, decay_up, iclr_a0,
              iclr_up, gate_up, k_k, k_a, r_k, wkv_gn_g, wkv_gn_b, w_rwkv_o, w_out, w_ffn_up, ffn_conv_w, w_ffn_down):
    rows = x.shape[1] // GRID_W
    rope = axial_rope_tables(rows)
    xc = ctx
    for l in range(DEPTH):
        lp = {
            'w_mod': w_mod[l], 'b_mod': b_mod[l],
            'g_pre_mix': g_pre_mix[l], 'g_post_mix': g_post_mix[l],
            'g_pre_ffn': g_pre_ffn[l], 'g_post_ffn': g_post_ffn[l],
            'w_in': w_in[l], 'q_norm': q_norm[l], 'k_norm': k_norm[l], 'w_attn_o': w_attn_o[l],
            'conv_w': conv_w[l], 'conv_b': conv_b[l], 'conv_ln_g': conv_ln_g[l], 'conv_ln_b': conv_ln_b[l],
            'w_conv_o': w_conv_o[l], 'shift_w': shift_w[l], 'decay_w0': decay_w0[l], 'decay_up': decay_up[l],
            'iclr_a0': iclr_a0[l], 'iclr_up': iclr_up[l], 'gate_up': gate_up[l], 'k_k': k_k[l], 'k_a': k_a[l],
            'r_k': r_k[l], 'wkv_gn_g': wkv_gn_g[l], 'wkv_gn_b': wkv_gn_b[l], 'w_rwkv_o': w_rwkv_o[l],
            'w_out': w_out[l], 'w_ffn_up': w_ffn_up[l], 'ffn_conv_w': ffn_conv_w[l], 'w_ffn_down': w_ffn_down[l],
        }
        x, xc = trunk_layer(x, xc, c, c_ctx, rope, lp, l == DEPTH - 1)
    return x
```

```python
import functools
import math

import jax
import jax.numpy as jnp
from jax import lax
from jax.experimental import pallas as pl
from jax.experimental.pallas import tpu as pltpu

F32 = jnp.float32
BF16 = jnp.bfloat16

D_MODEL = 1024
GRID_W = 64
N_Q_HEADS = 8
N_KV_HEADS = 2
GQA_GROUP = N_Q_HEADS // N_KV_HEADS
HEAD_DIM = 128
N_FREQ = HEAD_DIM // 4
ROPE_THETA = 10000.0
CONV_K = 31
CONV_HALO = 16
RWKV_HEAD = 64
RWKV_HEADS = D_MODEL // RWKV_HEAD
DECAY_SCALE = math.exp(-0.5)
D_FF = 2816
N_MOD = 6
EPS = 1e-6
LN_EPS = 1e-5
GN_EPS = RWKV_HEAD * 1e-5
LANES = 128
SUBLANES = 8
CHUNK = 64
N_PAIR = D_MODEL // LANES
ROW_TILE = 256
VMEM_LIMIT = 48 * 1024 * 1024

COL_Q = 0
COL_KV = 1024
COL_LORA = 1536
COL_GLU = 2048
COL_RKV = 4096
COL_GATE = 7168
N_IN_PAD = 10240


def _cparams(sem):
    return pltpu.CompilerParams(dimension_semantics=sem, vmem_limit_bytes=VMEM_LIMIT)


def _bdot(a, b):
    return jnp.dot(a.astype(BF16), b.astype(BF16), preferred_element_type=F32)


def _bdot_nt(a, b):
    return lax.dot_general(a.astype(BF16), b.astype(BF16), (((1,), (1,)), ((), ())),
                           preferred_element_type=F32)


def _bdot_tn(a, b):
    return lax.dot_general(a.astype(BF16), b.astype(BF16), (((0,), (0,)), ((), ())),
                           preferred_element_type=F32)


def _split2(x):
    hi = x.astype(BF16)
    lo = (x - hi.astype(F32)).astype(BF16)
    return hi, lo


def _dot3(a, b):
    ah, al = _split2(a)
    bh, bl = _split2(b)
    dot = functools.partial(jnp.dot, preferred_element_type=F32)
    return dot(ah, bh) + dot(al, bh) + dot(ah, bl)


def _sigmoid(x):
    return jax.nn.sigmoid(x)


def _head_sum(x, e_ref, et_ref):
    xh, xl = _split2(x)
    dot = functools.partial(jnp.dot, preferred_element_type=F32)
    s = dot(xh, e_ref[...]) + dot(xl, e_ref[...])
    sh, sl = _split2(s)
    return dot(sh, et_ref[...]) + dot(sl, et_ref[...])


def _mod_kernel(c_ref, w_ref, b_ref, o_ref):
    c = c_ref[...]
    o_ref[...] = _dot3(c * _sigmoid(c), w_ref[...]) + b_ref[...]


def _mod_call(cc, w_mod, b_mod):
    rows = cc.shape[0]
    n = w_mod.shape[1]
    tn = 1536
    return pl.pallas_call(
        _mod_kernel,
        grid=(n // tn,),
        in_specs=[pl.BlockSpec((rows, D_MODEL), lambda j: (0, 0)),
                  pl.BlockSpec((D_MODEL, tn), lambda j: (0, j)),
                  pl.BlockSpec((1, tn), lambda j: (0, j))],
        out_specs=pl.BlockSpec((rows, tn), lambda j: (0, j)),
        out_shape=jax.ShapeDtypeStruct((rows, n), F32),
        compiler_params=_cparams(("parallel",)),
        name="mod",
    )(cc, w_mod, b_mod.reshape(1, n))


def _mod_row(n_ctx_tiles, ctx_row):
    return lambda b, i, *_: (jnp.where(i < n_ctx_tiles, ctx_row, b), 0, 0)


def _nm_matmul_kernel(a_ref, mod_ref, g_ref, w_ref, o_ref, h_ref, *, shift_idx):
    @pl.when(pl.program_id(2) == 0)
    def _():
        x = a_ref[...]
        y = x * lax.rsqrt(jnp.mean(x * x, axis=-1, keepdims=True) + EPS) * g_ref[...]
        h = y * (1.0 + mod_ref[shift_idx + 1:shift_idx + 2, :]) + mod_ref[shift_idx:shift_idx + 1, :]
        h_ref[...] = h.astype(BF16)

    o_ref[...] = jnp.dot(h_ref[...], w_ref[...], preferred_element_type=F32).astype(o_ref.dtype)


def _nm_matmul(a, mod3, g, w, *, shift_idx, tn, n_ctx_tiles, ctx_row, tm):
    bsz, tt, d = a.shape
    n = w.shape[1]
    return pl.pallas_call(
        functools.partial(_nm_matmul_kernel, shift_idx=shift_idx),
        grid=(bsz, tt // tm, n // tn),
        in_specs=[pl.BlockSpec((None, tm, d), lambda b, i, j: (b, i, 0)),
                  pl.BlockSpec((None, N_MOD, d), _mod_row(n_ctx_tiles, ctx_row)),
                  pl.BlockSpec((1, d), lambda b, i, j: (0, 0)),
                  pl.BlockSpec((d, tn), lambda b, i, j: (0, j))],
        out_specs=pl.BlockSpec((None, tm, tn), lambda b, i, j: (b, i, j)),
        out_shape=jax.ShapeDtypeStruct((bsz, tt, n), F32),
        scratch_shapes=[pltpu.VMEM((tm, d), BF16)],
        compiler_params=_cparams(("parallel", "parallel", "arbitrary")),
        name="norm_mod_matmul",
    )(a, mod3, g.reshape(1, d), w)


def _qk_prep_kernel(q_ref, kv_ref, cos_ref, sin_ref, qn_ref, kn_ref, qo_ref, ko_ref, vo_ref):
    cos = cos_ref[...]
    sin = sin_ref[...]
    lane = lax.broadcasted_iota(jnp.int32, cos.shape, 1)
    first = (lane & (N_FREQ)) == 0

    def norm_rope(x, g):
        y = x * lax.rsqrt(jnp.mean(x * x, axis=-1, keepdims=True) + EPS) * g
        partner = jnp.where(first, pltpu.roll(y, LANES - N_FREQ, 1), pltpu.roll(y, N_FREQ, 1))
        return y * cos + partner * sin

    for h in range(N_Q_HEADS):
        sl = slice(h * HEAD_DIM, (h + 1) * HEAD_DIM)
        qo_ref[:, sl] = norm_rope(q_ref[:, sl], qn_ref[...]).astype(BF16)
    for h in range(N_KV_HEADS):
        sl = slice(h * HEAD_DIM, (h + 1) * HEAD_DIM)
        ko_ref[:, sl] = norm_rope(kv_ref[:, sl], kn_ref[...]).astype(BF16)
    vo_ref[...] = kv_ref[:, N_KV_HEADS * HEAD_DIM:].astype(BF16)


def _qk_prep(proj, cos, sin, q_norm, k_norm, *, tm):
    bsz, tt, _ = proj.shape
    dq = N_Q_HEADS * HEAD_DIM
    dkv = N_KV_HEADS * HEAD_DIM
    return pl.pallas_call(
        _qk_prep_kernel,
        grid=(bsz, tt // tm),
        in_specs=[pl.BlockSpec((None, tm, dq), lambda b, i: (b, i, COL_Q // dq)),
                  pl.BlockSpec((None, tm, 2 * dkv), lambda b, i: (b, i, COL_KV // (2 * dkv))),
                  pl.BlockSpec((tm, HEAD_DIM), lambda b, i: (i, 0)),
                  pl.BlockSpec((tm, HEAD_DIM), lambda b, i: (i, 0)),
                  pl.BlockSpec((1, HEAD_DIM), lambda b, i: (0, 0)),
                  pl.BlockSpec((1, HEAD_DIM), lambda b, i: (0, 0))],
        out_specs=[pl.BlockSpec((None, tm, dq), lambda b, i: (b, i, 0)),
                   pl.BlockSpec((None, tm, dkv), lambda b, i: (b, i, 0)),
                   pl.BlockSpec((None, tm, dkv), lambda b, i: (b, i, 0))],
        out_shape=[jax.ShapeDtypeStruct((bsz, tt, dq), BF16),
                   jax.ShapeDtypeStruct((bsz, tt, dkv), BF16),
                   jax.ShapeDtypeStruct((bsz, tt, dkv), BF16)],
        compiler_params=_cparams(("parallel", "parallel")),
        name="qk_prep",
    )(proj, proj, cos, sin, q_norm.reshape(1, HEAD_DIM), k_norm.reshape(1, HEAD_DIM))


def _attn_kernel(q_ref, k_ref, v_ref, o_ref, *, n_ctx_tiles, ctx_len):
    c = (HEAD_DIM ** -0.5) * math.log2(math.e)

    def attend(k, v):
        for g in range(GQA_GROUP):
            sl = slice(g * HEAD_DIM, (g + 1) * HEAD_DIM)
            s = lax.dot_general(q_ref[:, sl], k, (((1,), (1,)), ((), ())), preferred_element_type=F32)
            m = jnp.max(s, axis=-1, keepdims=True)
            p = jnp.exp2((s - m) * c)
            l = jnp.sum(p, axis=-1, keepdims=True)
            o = jnp.dot(p.astype(BF16), v, preferred_element_type=F32) / l
            o_ref[:, sl] = o.astype(o_ref.dtype)

    i = pl.program_id(2)

    @pl.when(i < n_ctx_tiles)
    def _():
        attend(k_ref[0:ctx_len, :], v_ref[0:ctx_len, :])

    @pl.when(i >= n_ctx_tiles)
    def _():
        attend(k_ref[...], v_ref[...])


def _attention(q, k, v, *, ctx_len, tq):
    bsz, tt, dq = q.shape
    gw = GQA_GROUP * HEAD_DIM
    return pl.pallas_call(
        functools.partial(_attn_kernel, n_ctx_tiles=ctx_len // tq, ctx_len=ctx_len),
        grid=(bsz, N_KV_HEADS, tt // tq),
        in_specs=[pl.BlockSpec((None, tq, gw), lambda b, h, i: (b, i, h)),
                  pl.BlockSpec((None, tt, HEAD_DIM), lambda b, h, i: (b, 0, h)),
                  pl.BlockSpec((None, tt, HEAD_DIM), lambda b, h, i: (b, 0, h))],
        out_specs=pl.BlockSpec((None, tq, gw), lambda b, h, i: (b, i, h)),
        out_shape=jax.ShapeDtypeStruct((bsz, tt, dq), BF16),
        compiler_params=_cparams(("parallel", "parallel", "arbitrary")),
        name="attention",
    )(q, k, v)


def _halo_specs(tm, halo, tt, col_block, width):
    per = tm // halo
    last = tt // halo - 1
    prev = pl.BlockSpec((None, halo, width), lambda b, i: (b, jnp.maximum(i * per - 1, 0), col_block))
    nxt = pl.BlockSpec((None, halo, width), lambda b, i: (b, jnp.minimum((i + 1) * per, last), col_block))
    return prev, nxt


def _edge_flags(i, n_ctx_tiles, n_tiles):
    first = jnp.logical_or(i == 0, i == n_ctx_tiles)
    last = jnp.logical_or(i == n_ctx_tiles - 1, i == n_tiles - 1)
    return first, last


def _conformer_kernel(ac_ref, bc_ref, ap_ref, bp_ref, an_ref, bn_ref, w_ref, cb_ref, lg_ref, lb_ref,
                      o_ref, buf_ref, *, n_ctx_tiles, n_tiles, tm, rc):
    first, last = _edge_flags(pl.program_id(1), n_ctx_tiles, n_tiles)
    h = CONV_HALO
    buf_ref[0:h, :] = jnp.where(first, 0.0, ap_ref[...] * _sigmoid(bp_ref[...]))
    buf_ref[h:h + tm, :] = ac_ref[...] * _sigmoid(bc_ref[...])
    buf_ref[h + tm:h + tm + h, :] = jnp.where(last, 0.0, an_ref[...] * _sigmoid(bn_ref[...]))
    off = h - CONV_K // 2
    for c in range(tm // rc):
        acc = jnp.zeros((rc, D_MODEL), F32)
        for j in range(CONV_K):
            r0 = c * rc + j + off
            acc = acc + buf_ref[r0:r0 + rc, :] * w_ref[j:j + 1, :]
        y = acc + cb_ref[...]
        mu = jnp.mean(y, axis=-1, keepdims=True)
        dlt = y - mu
        var = jnp.mean(dlt * dlt, axis=-1, keepdims=True)
        z = dlt * lax.rsqrt(var + LN_EPS) * lg_ref[...] + lb_ref[...]
        o_ref[c * rc:(c + 1) * rc, :] = (z * _sigmoid(z)).astype(o_ref.dtype)


def _conformer(proj, conv_w, conv_b, ln_g, ln_b, *, ctx_len, tm):
    bsz, tt, _ = proj.shape
    d = D_MODEL
    ca = COL_GLU // d
    prev_a, next_a = _halo_specs(tm, CONV_HALO, tt, ca, d)
    prev_b, next_b = _halo_specs(tm, CONV_HALO, tt, ca + 1, d)
    vec = lambda: pl.BlockSpec((1, d), lambda b, i: (0, 0))
    return pl.pallas_call(
        functools.partial(_conformer_kernel, n_ctx_tiles=ctx_len // tm, n_tiles=tt // tm, tm=tm, rc=32),
        grid=(bsz, tt // tm),
        in_specs=[pl.BlockSpec((None, tm, d), lambda b, i: (b, i, ca)),
                  pl.BlockSpec((None, tm, d), lambda b, i: (b, i, ca + 1)),
                  prev_a, prev_b, next_a, next_b,
                  pl.BlockSpec((CONV_K, d), lambda b, i: (0, 0)),
                  vec(), vec(), vec()],
        out_specs=pl.BlockSpec((None, tm, d), lambda b, i: (b, i, 0)),
        out_shape=jax.ShapeDtypeStruct((bsz, tt, d), BF16),
        scratch_shapes=[pltpu.VMEM((tm + 2 * CONV_HALO, d), F32)],
        compiler_params=_cparams(("parallel", "parallel")),
        name="conformer",
    )(proj, proj, proj, proj, proj, proj, conv_w, conv_b.reshape(1, d), ln_g.reshape(1, d), ln_b.reshape(1, d))


def _rwkv_prep_kernel(rc_ref, kc_ref, vc_ref, rp_ref, kp_ref, vp_ref, rn_ref, kn_ref, vn_ref, lora_ref,
                      sw_ref, kk_ref, ka_ref, w0_ref, dup_ref, a0_ref, iup_ref, e_ref, et_ref,
                      r_ref, v_ref, kap_ref, lw_ref, kd_ref, bd_ref, buf_ref,
                      *, n_ctx_tiles, n_tiles, tm):
    first, last = _edge_flags(pl.program_id(1), n_ctx_tiles, n_tiles)
    h = SUBLANES
    d = D_MODEL

    def shift(cur_ref, prev_ref, next_ref, col):
        buf_ref[0:h, :] = jnp.where(first, 0.0, prev_ref[...])
        buf_ref[h:h + tm, :] = cur_ref[...]
        buf_ref[h + tm:h + tm + h, :] = jnp.where(last, 0.0, next_ref[...])
        sl = slice(col * d, (col + 1) * d)
        return (buf_ref[h - 1:h - 1 + tm, :] * sw_ref[0:1, sl] + buf_ref[h:h + tm, :] * sw_ref[1:2, sl]
                + buf_ref[h + 1:h + 1 + tm, :] * sw_ref[2:3, sl])

    r = shift(rc_ref, rp_ref, rn_ref, 0)
    k = shift(kc_ref, kp_ref, kn_ref, 1)
    v = shift(vc_ref, vp_ref, vn_ref, 2)
    r_ref[...] = r
    v_ref[...] = v
    kk = k * kk_ref[...]
    ss = _head_sum(kk * kk, e_ref, et_ref)
    kap = kk * lax.rsqrt(jnp.maximum(ss, 1e-12))
    kap_ref[...] = kap
    tw = jnp.tanh(lora_ref[:, 0:LANES])
    la = lora_ref[:, LANES:2 * LANES]
    for dr in range(2):
        z = w0_ref[dr:dr + 1, :] + _dot3(tw, dup_ref[dr])
        lw_ref[dr] = -DECAY_SCALE * _sigmoid(z)
        a = _sigmoid(a0_ref[dr:dr + 1, :] + _dot3(la, iup_ref[dr]))
        kd_ref[dr] = k * (1.0 + (a - 1.0) * ka_ref[...])
        bd_ref[dr] = a * kap


def _rwkv_prep(proj, shift_w, k_k, k_a, decay_w0, decay_up_pad, iclr_a0, iclr_up_pad, e, et, *, ctx_len, tm):
    bsz, tt, _ = proj.shape
    d = D_MODEL
    c0 = COL_RKV // d
    cur = lambda c: pl.BlockSpec((None, tm, d), lambda b, i: (b, i, c))
    halos = [_halo_specs(tm, SUBLANES, tt, c0 + c, d) for c in range(3)]
    full = lambda shape: pl.BlockSpec(shape, lambda b, i: (0,) * len(shape))
    out1 = pl.BlockSpec((None, tm, d), lambda b, i: (b, i, 0))
    out2 = pl.BlockSpec((2, None, tm, d), lambda b, i: (0, b, i, 0))
    s1 = jax.ShapeDtypeStruct((bsz, tt, d), F32)
    s2 = jax.ShapeDtypeStruct((2, bsz, tt, d), F32)
    return pl.pallas_call(
        functools.partial(_rwkv_prep_kernel, n_ctx_tiles=ctx_len // tm, n_tiles=tt // tm, tm=tm),
        grid=(bsz, tt // tm),
        in_specs=[cur(c0), cur(c0 + 1), cur(c0 + 2),
                  halos[0][0], halos[1][0], halos[2][0], halos[0][1], halos[1][1], halos[2][1],
                  pl.BlockSpec((None, tm, 512), lambda b, i: (b, i, COL_LORA // 512)),
                  full((3, 3 * d)), full((1, d)), full((1, d)), full((2, d)), full((2, LANES, d)),
                  full((2, d)), full((2, LANES, d)), full((d, LANES)), full((LANES, d))],
        out_specs=[out1, out1, out1, out2, out2, out2],
        out_shape=[s1, s1, s1, s2, s2, s2],
        scratch_shapes=[pltpu.VMEM((tm + 2 * SUBLANES, d), F32)],
        compiler_params=_cparams(("parallel", "parallel")),
        name="rwkv_prep",
    )(proj, proj, proj, proj, proj, proj, proj, proj, proj, proj,
      shift_w, k_k.reshape(1, d), k_a.reshape(1, d), decay_w0, decay_up_pad, iclr_a0, iclr_up_pad, e, et)


def _scan_kernel(r_ref, v_ref, kap_ref, lw_ref, k_ref, b_ref, y_ref, h_ref):
    c = CHUNK
    c2 = 2 * c
    sgn = 1 - 2 * pl.program_id(1)

    @pl.when(pl.program_id(2) == 0)
    def _():
        h_ref[...] = jnp.zeros_like(h_ref)

    row = lax.broadcasted_iota(jnp.int32, (c, c), 0)
    col = lax.broadcasted_iota(jnp.int32, (c, c), 1)
    incl = jnp.where((col - row) * sgn <= 0, 1.0, 0.0).astype(BF16)
    lw = lw_ref[...]
    hi = lw.astype(BF16)
    rem = lw - hi.astype(F32)
    mid = rem.astype(BF16)
    lo = (rem - mid.astype(F32)).astype(BF16)
    dot = functools.partial(jnp.dot, preferred_element_type=F32)
    cum = dot(incl, hi) + dot(incl, mid) + dot(incl, lo)
    tot = jnp.sum(lw, axis=0, keepdims=True)

    rr = lax.broadcasted_iota(jnp.int32, (c2, c2), 0)
    cc = lax.broadcasted_iota(jnp.int32, (c2, c2), 1)
    same = (rr ^ cc) < c
    order = (cc - rr) * sgn
    strict = jnp.logical_and(same, order < 0)
    incl2 = jnp.logical_and(same, order <= 0)
    eye = jnp.where(cc == rr, 1.0, 0.0)
    head0 = lax.broadcasted_iota(jnp.int32, (c, c2), 1) < RWKV_HEAD

    def stack(x):
        return jnp.concatenate([jnp.where(head0, x, 0.0), jnp.where(head0, 0.0, x)], axis=0)

    def fold(x):
        return x[0:c] + x[c:c2]

    for p in range(N_PAIR):
        sl = slice(p * LANES, (p + 1) * LANES)
        cum_p = cum[:, sl]
        lw_p = lw[:, sl]
        tot_p = tot[:, sl]
        p_in = jnp.exp(cum_p)
        p_ex = jnp.exp(cum_p - lw_p)
        p_inv = jnp.exp(-cum_p)
        p_end = jnp.exp(tot_p - cum_p)
        kap = kap_ref[:, sl]
        k = k_ref[:, sl]
        b = b_ref[:, sl]
        v = v_ref[:, sl]
        kt = kap * p_ex
        rt = r_ref[:, sl] * p_in
        bh = b * p_inv
        kh = k * p_inv
        kend = k * p_end
        bend = b * p_end

        skt = stack(kt)
        srt = stack(rt)
        sv = stack(v)
        a = _bdot_nt(jnp.concatenate([skt, srt], axis=0), jnp.concatenate([bh, bh, kh, kh], axis=0))
        a_ab = jnp.where(strict, a[0:c2, 0:c2], 0.0)
        a_ak = jnp.where(strict, a[0:c2, c2:2 * c2], 0.0)
        a_rb = jnp.where(incl2, a[c2:2 * c2, 0:c2], 0.0)
        a_rk = jnp.where(incl2, a[c2:2 * c2, c2:2 * c2], 0.0)

        t = eye - a_ab
        x = a_ab
        n = 2
        while n < c:
            x = _bdot(x, x)
            t = t + _bdot(t, x)
            n *= 2

        av = _bdot(a_ak, sv)
        wu = _bdot(t, jnp.concatenate([skt, av], axis=1))
        w_s = wu[:, 0:c2]
        u0_s = wu[:, c2:2 * c2]
        y0 = fold(_bdot(jnp.concatenate([a_rk, -a_rb], axis=1), jnp.concatenate([sv, u0_s], axis=0)))
        y1 = fold(srt - _bdot(a_rb, w_s))
        w_p = fold(w_s)
        u0 = fold(u0_s)

        h0 = h_ref[p]
        y_ref[:, sl] = _bdot(y1, h0) + y0
        m = jnp.where(eye > 0.0, jnp.exp(tot_p), 0.0) - jnp.where(same, _bdot_tn(bend, w_p), 0.0)
        nn = jnp.where(same, _bdot_tn(jnp.concatenate([kend, -bend], axis=0),
                                      jnp.concatenate([v, u0], axis=0)), 0.0)
        h_ref[p] = _dot3(m, h0) + nn


def _rwkv_scan(r, v, kap, lw, kd, bd, *, ctx_len):
    bsz, tt, d = r.shape
    nc = tt // CHUNK
    ncc = ctx_len // CHUNK

    def chunk(dr, s):
        return jnp.where(dr == 0, s, jnp.where(s < ncc, ncc - 1 - s, nc + ncc - 1 - s))

    shared = pl.BlockSpec((None, CHUNK, d), lambda b, dr, s: (b, chunk(dr, s), 0))
    per_dir = pl.BlockSpec((None, None, CHUNK, d), lambda b, dr, s: (dr, b, chunk(dr, s), 0))
    return pl.pallas_call(
        _scan_kernel,
        grid=(bsz, 2, nc),
        in_specs=[shared, shared, shared, per_dir, per_dir, per_dir],
        out_specs=per_dir,
        out_shape=jax.ShapeDtypeStruct((2, bsz, tt, d), F32),
        scratch_shapes=[pltpu.VMEM((N_PAIR, LANES, LANES), F32)],
        compiler_params=_cparams(("parallel", "parallel", "arbitrary")),
        name="rwkv_scan",
    )(r, v, kap, lw, kd, bd)


def _rwkv_out_kernel(y_ref, r_ref, v_ref, kd_ref, lora_ref, gg_ref, gb_ref, rk_ref, gup_ref, e_ref, et_ref, o_ref):
    inv = 1.0 / RWKV_HEAD
    y = y_ref[0] + y_ref[1]
    mu = _head_sum(y, e_ref, et_ref) * inv
    dlt = y - mu
    var = _head_sum(dlt * dlt, e_ref, et_ref) * inv
    yn = dlt * lax.rsqrt(var + GN_EPS) * gg_ref[...] + gb_ref[...]
    bonus = _head_sum(r_ref[...] * (kd_ref[0] + kd_ref[1]) * rk_ref[...], e_ref, et_ref) * v_ref[...]
    g = _bdot(_sigmoid(lora_ref[:, 2 * LANES:3 * LANES]), gup_ref[...])
    o_ref[...] = ((yn + bonus) * g).astype(o_ref.dtype)


def _rwkv_out(y, r, v, kd, proj, gn_g, gn_b, r_k, gate_up, e, et, *, tm):
    bsz, tt, d = r.shape
    one = pl.BlockSpec((None, tm, d), lambda b, i: (b, i, 0))
    two = pl.BlockSpec((2, None, tm, d), lambda b, i: (0, b, i, 0))
    full = lambda shape: pl.BlockSpec(shape, lambda b, i: (0,) * len(shape))
    return pl.pallas_call(
        _rwkv_out_kernel,
        grid=(bsz, tt // tm),
        in_specs=[two, one, one, two,
                  pl.BlockSpec((None, tm, 512), lambda b, i: (b, i, COL_LORA // 512)),
                  full((1, d)), full((1, d)), full((1, d)), full((LANES, d)), full((d, LANES)), full((LANES, d))],
        out_specs=one,
        out_shape=jax.ShapeDtypeStruct((bsz, tt, d), BF16),
        compiler_params=_cparams(("parallel", "parallel")),
        name="rwkv_out",
    )(y, r, v, kd, proj, gn_g.reshape(1, d), gn_b.reshape(1, d), r_k.reshape(1, d), gate_up, e, et)


def _merge_kernel(a_ref, att_ref, conv_ref, rw_ref, ga_ref, gc_ref, gr_ref, mod_ref, g_ref,
                  wa_ref, wc_ref, wr_ref, wo_ref, o_ref):
    dot = functools.partial(jnp.dot, preferred_element_type=F32)
    m = (_sigmoid(ga_ref[...]) * dot(att_ref[...], wa_ref[...])
         + _sigmoid(gc_ref[...]) * dot(conv_ref[...], wc_ref[...])
         + _sigmoid(gr_ref[...]) * dot(rw_ref[...], wr_ref[...]))
    z = dot(m.astype(BF16), wo_ref[...])
    zn = z * lax.rsqrt(jnp.mean(z * z, axis=-1, keepdims=True) + EPS) * g_ref[...]
    o_ref[...] = a_ref[...] + mod_ref[2:3, :] * zn


def _merge(a, att, conv, rw, proj, mod3, g, wa, wc, wr, wo, *, n_ctx_tiles, ctx_row, tm):
    bsz, tt, d = a.shape
    cg = COL_GATE // d
    one = pl.BlockSpec((None, tm, d), lambda b, i: (b, i, 0))
    gate = lambda c: pl.BlockSpec((None, tm, d), lambda b, i: (b, i, cg + c))
    wspec = pl.BlockSpec((d, d), lambda b, i: (0, 0))
    return pl.pallas_call(
        _merge_kernel,
        grid=(bsz, tt // tm),
        in_specs=[one, one, one, one, gate(0), gate(1), gate(2),
                  pl.BlockSpec((None, N_MOD, d), _mod_row(n_ctx_tiles, ctx_row)),
                  pl.BlockSpec((1, d), lambda b, i: (0, 0)),
                  wspec, wspec, wspec, wspec],
        out_specs=one,
        out_shape=jax.ShapeDtypeStruct((bsz, tt, d), F32),
        compiler_params=_cparams(("parallel", "parallel")),
        name="merge",
    )(a, att, conv, rw, proj, proj, proj, mod3, g.reshape(1, d), wa, wc, wr, wo)


def _ffn_tail_kernel(a_ref, zc_ref, zp_ref, zn_ref, cw_ref, mod_ref, g_ref, wd_ref, o_ref, buf_ref,
                     *, n_ctx_tiles, n_tiles, tm):
    first, last = _edge_flags(pl.program_id(1), n_ctx_tiles, n_tiles)
    h = SUBLANES
    buf_ref[0:h, :] = jnp.where(first, 0.0, zp_ref[...])
    buf_ref[h:h + tm, :] = zc_ref[...]
    buf_ref[h + tm:h + tm + h, :] = jnp.where(last, 0.0, zn_ref[...])

    def conv(sl):
        return (buf_ref[h - 1:h - 1 + tm, sl] * cw_ref[0:1, sl] + buf_ref[h:h + tm, sl] * cw_ref[1:2, sl]
                + buf_ref[h + 1:h + 1 + tm, sl] * cw_ref[2:3, sl])

    gate = conv(slice(0, D_FF))
    val = conv(slice(D_FF, 2 * D_FF))
    u = (gate * _sigmoid(gate) * val).astype(BF16)
    z = jnp.dot(u, wd_ref[...], preferred_element_type=F32)
    zn = z * lax.rsqrt(jnp.mean(z * z, axis=-1, keepdims=True) + EPS) * g_ref[...]
    o_ref[...] = a_ref[...] + mod_ref[5:6, :] * zn


def _ffn_tail(a, z, conv_w, mod3, g, wd, *, n_ctx_tiles, ctx_row, tm):
    bsz, tt, d = a.shape
    f2 = 2 * D_FF
    prev, nxt = _halo_specs(tm, SUBLANES, tt, 0, f2)
    one = pl.BlockSpec((None, tm, d), lambda b, i: (b, i, 0))
    return pl.pallas_call(
        functools.partial(_ffn_tail_kernel, n_ctx_tiles=n_ctx_tiles, n_tiles=tt // tm, tm=tm),
        grid=(bsz, tt // tm),
        in_specs=[one, pl.BlockSpec((None, tm, f2), lambda b, i: (b, i, 0)), prev, nxt,
                  pl.BlockSpec((3, f2), lambda b, i: (0, 0)),
                  pl.BlockSpec((None, N_MOD, d), _mod_row(n_ctx_tiles, ctx_row)),
                  pl.BlockSpec((1, d), lambda b, i: (0, 0)),
                  pl.BlockSpec((D_FF, d), lambda b, i: (0, 0))],
        out_specs=one,
        out_shape=jax.ShapeDtypeStruct((bsz, tt, d), F32),
        scratch_shapes=[pltpu.VMEM((tm + 2 * SUBLANES, f2), F32)],
        compiler_params=_cparams(("parallel", "parallel")),
        name="ffn_tail",
    )(a, z, z, z, conv_w, mod3, g.reshape(1, d), wd)


def _rope_tables(ctx_len, seq):
    rows = seq // GRID_W
    row = jnp.repeat(jnp.arange(rows, dtype=F32), GRID_W)
    col = jnp.tile(jnp.arange(GRID_W, dtype=F32), rows)
    inv_freq = ROPE_THETA ** (-jnp.arange(N_FREQ, dtype=F32) / N_FREQ)
    ang_r = row[:, None] * inv_freq
    ang_c = col[:, None] * inv_freq
    cos = jnp.concatenate([jnp.cos(ang_r), jnp.cos(ang_r), jnp.cos(ang_c), jnp.cos(ang_c)], axis=-1)
    sin = jnp.concatenate([-jnp.sin(ang_r), jnp.sin(ang_r), -jnp.sin(ang_c), jnp.sin(ang_c)], axis=-1)
    cos = jnp.concatenate([jnp.ones((ctx_len, HEAD_DIM), F32), cos], axis=0)
    sin = jnp.concatenate([jnp.zeros((ctx_len, HEAD_DIM), F32), sin], axis=0)
    return cos, sin


def _reorder_w_in(w):
    qkv = w[:, 0:1536]
    glu = w[:, 1536:3584]
    rkv = w[:, 3584:6656]
    lora = w[:, 6656:7040]
    gates = w[:, 7040:10112]
    pad = jnp.zeros((w.shape[0], LANES), w.dtype)
    return jnp.concatenate([qkv, lora, pad, glu, rkv, gates], axis=1).astype(BF16)


def _pad_lora_up(up):
    z = jnp.zeros_like(up[0])
    return jnp.stack([jnp.concatenate([up[0], z], axis=0), jnp.concatenate([z, up[1]], axis=0)])


def kernel(x, c, ctx, c_ctx, w_mod, b_mod, g_pre_mix, g_post_mix, g_pre_ffn, g_post_ffn, w_in, q_norm, k_norm,
           w_attn_o, conv_w, conv_b, conv_ln_g, conv_ln_b, w_conv_o, shift_w, decay_w0, decay_up, iclr_a0,
           iclr_up, gate_up, k_k, k_a, r_k, wkv_gn_g, wkv_gn_b, w_rwkv_o, w_out, w_ffn_up, ffn_conv_w, w_ffn_down):
    bsz, seq, d = x.shape
    ctx_len = ctx.shape[1]
    depth = w_mod.shape[0]
    tm = min(ROW_TILE, ctx_len)
    assert d == D_MODEL and ctx_len % tm == 0 and seq % tm == 0 and ctx_len % CHUNK == 0 and seq % CHUNK == 0
    n_ctx_tiles = ctx_len // tm

    a = jnp.concatenate([ctx, x], axis=1)
    mod_rows = -(-(bsz + 1) // SUBLANES) * SUBLANES
    cc = jnp.zeros((mod_rows, d), F32).at[:bsz].set(c).at[bsz].set(c_ctx)
    cos, sin = _rope_tables(ctx_len, seq)
    head = jnp.arange(d, dtype=jnp.int32) // RWKV_HEAD
    e = (head[:, None] == jnp.arange(LANES, dtype=jnp.int32)[None, :]).astype(BF16)
    et = e.T

    for l in range(depth):
        mod3 = _mod_call(cc, w_mod[l], b_mod[l]).reshape(mod_rows, N_MOD, d)
        kw = dict(n_ctx_tiles=n_ctx_tiles, ctx_row=bsz, tm=tm)
        proj = _nm_matmul(a, mod3, g_pre_mix[l], _reorder_w_in(w_in[l]), shift_idx=0, tn=1024, **kw)
        q, k, v = _qk_prep(proj, cos, sin, q_norm[l], k_norm[l], tm=tm)
        att = _attention(q, k, v, ctx_len=ctx_len, tq=tm)
        conv = _conformer(proj, conv_w[l], conv_b[l], conv_ln_g[l], conv_ln_b[l], ctx_len=ctx_len, tm=tm)
        r, vv, kap, lw, kd, bd = _rwkv_prep(proj, shift_w[l], k_k[l], k_a[l], decay_w0[l], _pad_lora_up(decay_up[l]),
                                            iclr_a0[l], _pad_lora_up(iclr_up[l]), e, et, ctx_len=ctx_len, tm=tm)
        y = _rwkv_scan(r, vv, kap, lw, kd, bd, ctx_len=ctx_len)
        rw = _rwkv_out(y, r, vv, kd, proj, wkv_gn_g[l], wkv_gn_b[l], r_k[l], gate_up[l].astype(BF16), e, et, tm=tm)
        a = _merge(a, att, conv, rw, proj, mod3, g_post_mix[l], w_attn_o[l].astype(BF16), w_conv_o[l].astype(BF16),
                   w_rwkv_o[l].astype(BF16), w_out[l].astype(BF16), **kw)
        z = _nm_matmul(a, mod3, g_pre_ffn[l], w_ffn_up[l].astype(BF16), shift_idx=3, tn=512, **kw)
        a = _ffn_tail(a, z, ffn_conv_w[l], mod3, g_post_ffn[l], w_ffn_down[l].astype(BF16), **kw)
    return a[:, ctx_len:, :]
```

```python
import functools
import math

import jax
import jax.numpy as jnp
from jax import lax
from jax.experimental import pallas as pl
from jax.experimental.pallas import tpu as pltpu

F32 = jnp.float32
BF16 = jnp.bfloat16

D_MODEL = 1024
GRID_W = 64
N_Q_HEADS = 8
N_KV_HEADS = 2
GQA_GROUP = N_Q_HEADS // N_KV_HEADS
HEAD_DIM = 128
N_FREQ = HEAD_DIM // 4
ROPE_THETA = 10000.0
CONV_K = 31
CONV_HALO = 16
RWKV_HEAD = 64
RWKV_HEADS = D_MODEL // RWKV_HEAD
DECAY_SCALE = math.exp(-0.5)
D_FF = 2816
N_MOD = 6
EPS = 1e-6
LN_EPS = 1e-5
GN_EPS = RWKV_HEAD * 1e-5
LANES = 128
SUBLANES = 8
CHUNK = 64
N_PAIR = D_MODEL // LANES
ROW_TILE = 256
VMEM_LIMIT = 48 * 1024 * 1024

COL_Q = 0
COL_KV = 1024
COL_LORA = 1536
COL_GLU = 2048
COL_RKV = 4096
COL_GATE = 7168
N_IN_PAD = 10240


def _cparams(sem):
    return pltpu.CompilerParams(dimension_semantics=sem, vmem_limit_bytes=VMEM_LIMIT)


def _bdot(a, b):
    return jnp.dot(a.astype(BF16), b.astype(BF16), preferred_element_type=F32)


def _bdot_nt(a, b):
    return lax.dot_general(a.astype(BF16), b.astype(BF16), (((1,), (1,)), ((), ())),
                           preferred_element_type=F32)


def _bdot_tn(a, b):
    return lax.dot_general(a.astype(BF16), b.astype(BF16), (((0,), (0,)), ((), ())),
                           preferred_element_type=F32)


def _split2(x):
    hi = x.astype(BF16)
    lo = (x - hi.astype(F32)).astype(BF16)
    return hi, lo


def _dot3(a, b):
    ah, al = _split2(a)
    bh, bl = _split2(b)
    dot = functools.partial(jnp.dot, preferred_element_type=F32)
    return dot(ah, bh) + dot(al, bh) + dot(ah, bl)


def _sigmoid(x):
    return jax.nn.sigmoid(x)


def _head_sum(x, e_ref, et_ref):
    xh, xl = _split2(x)
    dot = functools.partial(jnp.dot, preferred_element_type=F32)
    s = dot(xh, e_ref[...]) + dot(xl, e_ref[...])
    sh, sl = _split2(s)
    return dot(sh, et_ref[...]) + dot(sl, et_ref[...])


def _mod_kernel(c_ref, w_ref, b_ref, o_ref):
    c = c_ref[...]
    o_ref[...] = _dot3(c * _sigmoid(c), w_ref[...]) + b_ref[...]


def _mod_call(cc, w_mod, b_mod):
    rows = cc.shape[0]
    n = w_mod.shape[1]
    tn = 1536
    return pl.pallas_call(
        _mod_kernel,
        grid=(n // tn,),
        in_specs=[pl.BlockSpec((rows, D_MODEL), lambda j: (0, 0)),
                  pl.BlockSpec((D_MODEL, tn), lambda j: (0, j)),
                  pl.BlockSpec((1, tn), lambda j: (0, j))],
        out_specs=pl.BlockSpec((rows, tn), lambda j: (0, j)),
        out_shape=jax.ShapeDtypeStruct((rows, n), F32),
        compiler_params=_cparams(("parallel",)),
        name="mod",
    )(cc, w_mod, b_mod.reshape(1, n))


def _mod_row(n_ctx_tiles, ctx_row):
    return lambda b, i, *_: (jnp.where(i < n_ctx_tiles, ctx_row, b), 0, 0)


def _nm_matmul_kernel(a_ref, mod_ref, g_ref, w_ref, o_ref, h_ref, *, shift_idx, seg, nseg):
    @pl.when(pl.program_id(1) == 0)
    def _():
        for s in range(nseg):
            rows = slice(s * seg, (s + 1) * seg)
            x = a_ref[rows, :]
            y = x * lax.rsqrt(jnp.mean(x * x, axis=-1, keepdims=True) + EPS) * g_ref[...]
            h = y * (1.0 + mod_ref[s, shift_idx + 1:shift_idx + 2, :]) + mod_ref[s, shift_idx:shift_idx + 1, :]
            h_ref[rows, :] = h.astype(BF16)

    o_ref[...] = jnp.dot(h_ref[...], w_ref[...], preferred_element_type=F32).astype(o_ref.dtype)


def _nm_matmul(a, modseg, g, w, *, shift_idx, tn, seg):
    bsz, tt, d = a.shape
    n = w.shape[1]
    rows = bsz * tt
    nseg = max(s for s in (4, 2, 1) if (rows // seg) % s == 0)
    tm = seg * nseg
    out = pl.pallas_call(
        functools.partial(_nm_matmul_kernel, shift_idx=shift_idx, seg=seg, nseg=nseg),
        grid=(rows // tm, n // tn),
        in_specs=[pl.BlockSpec((tm, d), lambda i, j: (i, 0)),
                  pl.BlockSpec((nseg, N_MOD, d), lambda i, j: (i, 0, 0)),
                  pl.BlockSpec((1, d), lambda i, j: (0, 0)),
                  pl.BlockSpec((d, tn), lambda i, j: (0, j))],
        out_specs=pl.BlockSpec((tm, tn), lambda i, j: (i, j)),
        out_shape=jax.ShapeDtypeStruct((rows, n), F32),
        scratch_shapes=[pltpu.VMEM((tm, d), BF16)],
        compiler_params=_cparams(("parallel", "arbitrary")),
        name="norm_mod_matmul",
    )(a.reshape(rows, d), modseg, g.reshape(1, d), w)
    return out.reshape(bsz, tt, n)


def _qk_prep_kernel(q_ref, kv_ref, cos_ref, sin_ref, qn_ref, kn_ref, qo_ref, ko_ref, vo_ref):
    cos = cos_ref[...]
    sin = sin_ref[...]
    lane = lax.broadcasted_iota(jnp.int32, cos.shape, 1)
    first = (lane & (N_FREQ)) == 0

    def norm_rope(x, g):
        y = x * lax.rsqrt(jnp.mean(x * x, axis=-1, keepdims=True) + EPS) * g
        partner = jnp.where(first, pltpu.roll(y, LANES - N_FREQ, 1), pltpu.roll(y, N_FREQ, 1))
        return y * cos + partner * sin

    for h in range(N_Q_HEADS):
        sl = slice(h * HEAD_DIM, (h + 1) * HEAD_DIM)
        qo_ref[:, sl] = norm_rope(q_ref[:, sl], qn_ref[...]).astype(BF16)
    for h in range(N_KV_HEADS):
        sl = slice(h * HEAD_DIM, (h + 1) * HEAD_DIM)
        ko_ref[:, sl] = norm_rope(kv_ref[:, sl], kn_ref[...]).astype(BF16)
    vo_ref[...] = kv_ref[:, N_KV_HEADS * HEAD_DIM:].astype(BF16)


def _qk_prep(proj, cos, sin, q_norm, k_norm, *, tm):
    bsz, tt, _ = proj.shape
    dq = N_Q_HEADS * HEAD_DIM
    dkv = N_KV_HEADS * HEAD_DIM
    return pl.pallas_call(
        _qk_prep_kernel,
        grid=(bsz, tt // tm),
        in_specs=[pl.BlockSpec((None, tm, dq), lambda b, i: (b, i, COL_Q // dq)),
                  pl.BlockSpec((None, tm, 2 * dkv), lambda b, i: (b, i, COL_KV // (2 * dkv))),
                  pl.BlockSpec((tm, HEAD_DIM), lambda b, i: (i, 0)),
                  pl.BlockSpec((tm, HEAD_DIM), lambda b, i: (i, 0)),
                  pl.BlockSpec((1, HEAD_DIM), lambda b, i: (0, 0)),
                  pl.BlockSpec((1, HEAD_DIM), lambda b, i: (0, 0))],
        out_specs=[pl.BlockSpec((None, tm, dq), lambda b, i: (b, i, 0)),
                   pl.BlockSpec((None, tm, dkv), lambda b, i: (b, i, 0)),
                   pl.BlockSpec((None, tm, dkv), lambda b, i: (b, i, 0))],
        out_shape=[jax.ShapeDtypeStruct((bsz, tt, dq), BF16),
                   jax.ShapeDtypeStruct((bsz, tt, dkv), BF16),
                   jax.ShapeDtypeStruct((bsz, tt, dkv), BF16)],
        compiler_params=_cparams(("parallel", "parallel")),
        name="qk_prep",
    )(proj, proj, cos, sin, q_norm.reshape(1, HEAD_DIM), k_norm.reshape(1, HEAD_DIM))


def _attn_kernel(q_ref, k_ref, v_ref, o_ref, *, n_ctx_tiles, ctx_len):
    c = (HEAD_DIM ** -0.5) * math.log2(math.e)

    def attend(k, v):
        for g in range(GQA_GROUP):
            sl = slice(g * HEAD_DIM, (g + 1) * HEAD_DIM)
            s = lax.dot_general(q_ref[:, sl], k, (((1,), (1,)), ((), ())), preferred_element_type=F32)
            m = jnp.max(s, axis=-1, keepdims=True)
            p = jnp.exp2((s - m) * c)
            l = jnp.sum(p, axis=-1, keepdims=True)
            o = jnp.dot(p.astype(BF16), v, preferred_element_type=F32) / l
            o_ref[:, sl] = o.astype(o_ref.dtype)

    i = pl.program_id(2)

    @pl.when(i < n_ctx_tiles)
    def _():
        attend(k_ref[0:ctx_len, :], v_ref[0:ctx_len, :])

    @pl.when(i >= n_ctx_tiles)
    def _():
        attend(k_ref[...], v_ref[...])


def _attention(q, k, v, *, ctx_len, tq):
    bsz, tt, dq = q.shape
    gw = GQA_GROUP * HEAD_DIM
    return pl.pallas_call(
        functools.partial(_attn_kernel, n_ctx_tiles=ctx_len // tq, ctx_len=ctx_len),
        grid=(bsz, N_KV_HEADS, tt // tq),
        in_specs=[pl.BlockSpec((None, tq, gw), lambda b, h, i: (b, i, h)),
                  pl.BlockSpec((None, tt, HEAD_DIM), lambda b, h, i: (b, 0, h)),
                  pl.BlockSpec((None, tt, HEAD_DIM), lambda b, h, i: (b, 0, h))],
        out_specs=pl.BlockSpec((None, tq, gw), lambda b, h, i: (b, i, h)),
        out_shape=jax.ShapeDtypeStruct((bsz, tt, dq), BF16),
        compiler_params=_cparams(("parallel", "parallel", "arbitrary")),
        name="attention",
    )(q, k, v)


def _halo_specs(tm, halo, tt, col_block, width):
    per = tm // halo
    last = tt // halo - 1
    prev = pl.BlockSpec((None, halo, width), lambda b, i: (b, jnp.maximum(i * per - 1, 0), col_block))
    nxt = pl.BlockSpec((None, halo, width), lambda b, i: (b, jnp.minimum((i + 1) * per, last), col_block))
    return prev, nxt


def _edge_flags(i, n_ctx_tiles, n_tiles):
    first = jnp.logical_or(i == 0, i == n_ctx_tiles)
    last = jnp.logical_or(i == n_ctx_tiles - 1, i == n_tiles - 1)
    return first, last


def _conformer_kernel(ac_ref, bc_ref, ap_ref, bp_ref, an_ref, bn_ref, w_ref, cb_ref, lg_ref, lb_ref,
                      o_ref, buf_ref, sh_ref, *, n_ctx_tiles, n_tiles, tm, rc):
    first, last = _edge_flags(pl.program_id(1), n_ctx_tiles, n_tiles)
    h = CONV_HALO
    buf_ref[0:h, :] = jnp.where(first, 0.0, ap_ref[...] * _sigmoid(bp_ref[...]))
    buf_ref[h:h + tm, :] = ac_ref[...] * _sigmoid(bc_ref[...])
    buf_ref[h + tm:h + tm + h, :] = jnp.where(last, 0.0, an_ref[...] * _sigmoid(bn_ref[...]))
    n_sh = sh_ref.shape[1]
    for s in range(1, SUBLANES):
        sh_ref[s - 1] = buf_ref[s:s + n_sh, :]
    off = h - CONV_K // 2
    for c in range(tm // rc):
        acc = jnp.zeros((rc, D_MODEL), F32)
        for j in range(CONV_K):
            s = (j + off) % SUBLANES
            r0 = c * rc + (j + off) - s
            tap = buf_ref[r0:r0 + rc, :] if s == 0 else sh_ref[s - 1, r0:r0 + rc, :]
            acc = acc + tap * w_ref[j:j + 1, :]
        y = acc + cb_ref[...]
        mu = jnp.mean(y, axis=-1, keepdims=True)
        dlt = y - mu
        var = jnp.mean(dlt * dlt, axis=-1, keepdims=True)
        z = dlt * lax.rsqrt(var + LN_EPS) * lg_ref[...] + lb_ref[...]
        o_ref[c * rc:(c + 1) * rc, :] = (z * _sigmoid(z)).astype(o_ref.dtype)


def _conformer(proj, conv_w, conv_b, ln_g, ln_b, *, ctx_len, tm):
    bsz, tt, _ = proj.shape
    d = D_MODEL
    ca = COL_GLU // d
    prev_a, next_a = _halo_specs(tm, CONV_HALO, tt, ca, d)
    prev_b, next_b = _halo_specs(tm, CONV_HALO, tt, ca + 1, d)
    vec = lambda: pl.BlockSpec((1, d), lambda b, i: (0, 0))
    return pl.pallas_call(
        functools.partial(_conformer_kernel, n_ctx_tiles=ctx_len // tm, n_tiles=tt // tm, tm=tm, rc=32),
        grid=(bsz, tt // tm),
        in_specs=[pl.BlockSpec((None, tm, d), lambda b, i: (b, i, ca)),
                  pl.BlockSpec((None, tm, d), lambda b, i: (b, i, ca + 1)),
                  prev_a, prev_b, next_a, next_b,
                  pl.BlockSpec((CONV_K, d), lambda b, i: (0, 0)),
                  vec(), vec(), vec()],
        out_specs=pl.BlockSpec((None, tm, d), lambda b, i: (b, i, 0)),
        out_shape=jax.ShapeDtypeStruct((bsz, tt, d), BF16),
        scratch_shapes=[pltpu.VMEM((tm + 2 * CONV_HALO, d), F32),
                        pltpu.VMEM((SUBLANES - 1, tm + 2 * CONV_HALO - SUBLANES, d), F32)],
        compiler_params=_cparams(("parallel", "parallel")),
        name="conformer",
    )(proj, proj, proj, proj, proj, proj, conv_w, conv_b.reshape(1, d), ln_g.reshape(1, d), ln_b.reshape(1, d))


def _rwkv_prep_kernel(rc_ref, kc_ref, vc_ref, rp_ref, kp_ref, vp_ref, rn_ref, kn_ref, vn_ref, lora_ref,
                      sw_ref, kk_ref, ka_ref, w0_ref, dup_ref, a0_ref, iup_ref, e_ref, et_ref,
                      r_ref, v_ref, kap_ref, lw_ref, kd_ref, bd_ref, buf_ref,
                      *, n_ctx_tiles, n_tiles, tm):
    first, last = _edge_flags(pl.program_id(1), n_ctx_tiles, n_tiles)
    h = SUBLANES
    d = D_MODEL

    def shift(cur_ref, prev_ref, next_ref, col):
        buf_ref[0:h, :] = jnp.where(first, 0.0, prev_ref[...])
        buf_ref[h:h + tm, :] = cur_ref[...]
        buf_ref[h + tm:h + tm + h, :] = jnp.where(last, 0.0, next_ref[...])
        sl = slice(col * d, (col + 1) * d)
        return (buf_ref[h - 1:h - 1 + tm, :] * sw_ref[0:1, sl] + buf_ref[h:h + tm, :] * sw_ref[1:2, sl]
                + buf_ref[h + 1:h + 1 + tm, :] * sw_ref[2:3, sl])

    r = shift(rc_ref, rp_ref, rn_ref, 0)
    k = shift(kc_ref, kp_ref, kn_ref, 1)
    v = shift(vc_ref, vp_ref, vn_ref, 2)
    r_ref[...] = r
    v_ref[...] = v
    kk = k * kk_ref[...]
    ss = _head_sum(kk * kk, e_ref, et_ref)
    kap = kk * lax.rsqrt(jnp.maximum(ss, 1e-12))
    kap_ref[...] = kap
    tw = jnp.tanh(lora_ref[:, 0:LANES])
    la = lora_ref[:, LANES:2 * LANES]
    for dr in range(2):
        z = w0_ref[dr:dr + 1, :] + _dot3(tw, dup_ref[dr])
        lw_ref[dr] = -DECAY_SCALE * _sigmoid(z)
        a = _sigmoid(a0_ref[dr:dr + 1, :] + _dot3(la, iup_ref[dr]))
        kd_ref[dr] = k * (1.0 + (a - 1.0) * ka_ref[...])
        bd_ref[dr] = a * kap


def _rwkv_prep(proj, shift_w, k_k, k_a, decay_w0, decay_up_pad, iclr_a0, iclr_up_pad, e, et, *, ctx_len, tm):
    bsz, tt, _ = proj.shape
    d = D_MODEL
    c0 = COL_RKV // d
    cur = lambda c: pl.BlockSpec((None, tm, d), lambda b, i: (b, i, c))
    halos = [_halo_specs(tm, SUBLANES, tt, c0 + c, d) for c in range(3)]
    full = lambda shape: pl.BlockSpec(shape, lambda b, i: (0,) * len(shape))
    out1 = pl.BlockSpec((None, tm, d), lambda b, i: (b, i, 0))
    out2 = pl.BlockSpec((2, None, tm, d), lambda b, i: (0, b, i, 0))
    s1 = jax.ShapeDtypeStruct((bsz, tt, d), F32)
    s2 = jax.ShapeDtypeStruct((2, bsz, tt, d), F32)
    return pl.pallas_call(
        functools.partial(_rwkv_prep_kernel, n_ctx_tiles=ctx_len // tm, n_tiles=tt // tm, tm=tm),
        grid=(bsz, tt // tm),
        in_specs=[cur(c0), cur(c0 + 1), cur(c0 + 2),
                  halos[0][0], halos[1][0], halos[2][0], halos[0][1], halos[1][1], halos[2][1],
                  pl.BlockSpec((None, tm, 512), lambda b, i: (b, i, COL_LORA // 512)),
                  full((3, 3 * d)), full((1, d)), full((1, d)), full((2, d)), full((2, LANES, d)),
                  full((2, d)), full((2, LANES, d)), full((d, LANES)), full((LANES, d))],
        out_specs=[out1, out1, out1, out2, out2, out2],
        out_shape=[s1, s1, s1, s2, s2, s2],
        scratch_shapes=[pltpu.VMEM((tm + 2 * SUBLANES, d), F32)],
        compiler_params=_cparams(("parallel", "parallel")),
        name="rwkv_prep",
    )(proj, proj, proj, proj, proj, proj, proj, proj, proj, proj,
      shift_w, k_k.reshape(1, d), k_a.reshape(1, d), decay_w0, decay_up_pad, iclr_a0, iclr_up_pad, e, et)


def _scan_kernel(r_ref, v_ref, kap_ref, lw_ref, k_ref, b_ref, y_ref, h_ref):
    c = CHUNK
    c2 = 2 * c
    sgn = 1 - 2 * pl.program_id(1)

    @pl.when(pl.program_id(2) == 0)
    def _():
        h_ref[...] = jnp.zeros_like(h_ref)

    row = lax.broadcasted_iota(jnp.int32, (c, c), 0)
    col = lax.broadcasted_iota(jnp.int32, (c, c), 1)
    incl = jnp.where((col - row) * sgn <= 0, 1.0, 0.0).astype(BF16)
    lw = lw_ref[...]
    hi = lw.astype(BF16)
    rem = lw - hi.astype(F32)
    mid = rem.astype(BF16)
    lo = (rem - mid.astype(F32)).astype(BF16)
    dot = functools.partial(jnp.dot, preferred_element_type=F32)
    cum = dot(incl, hi) + dot(incl, mid) + dot(incl, lo)
    tot = jnp.sum(lw, axis=0, keepdims=True)

    rr = lax.broadcasted_iota(jnp.int32, (c2, c2), 0)
    cc = lax.broadcasted_iota(jnp.int32, (c2, c2), 1)
    same = (rr ^ cc) < c
    order = (cc - rr) * sgn
    strict = jnp.logical_and(same, order < 0)
    incl2 = jnp.logical_and(same, order <= 0)
    eye = jnp.where(cc == rr, 1.0, 0.0)
    head0 = lax.broadcasted_iota(jnp.int32, (c, c2), 1) < RWKV_HEAD

    def stack(x):
        return jnp.concatenate([jnp.where(head0, x, 0.0), jnp.where(head0, 0.0, x)], axis=0)

    def fold(x):
        return x[0:c] + x[c:c2]

    pairs = range(N_PAIR)
    sls = [slice(p * LANES, (p + 1) * LANES) for p in pairs]
    skt, srt, sv, vs, kend, bend, ptot, a = [], [], [], [], [], [], [], []
    for sl in sls:
        cum_p = cum[:, sl]
        p_inv = jnp.exp(-cum_p)
        p_end = jnp.exp(tot[:, sl] - cum_p)
        k = k_ref[:, sl]
        b = b_ref[:, sl]
        v = v_ref[:, sl]
        bh = b * p_inv
        kh = k * p_inv
        skt.append(stack(kap_ref[:, sl] * jnp.exp(cum_p - lw[:, sl])))
        srt.append(stack(r_ref[:, sl] * jnp.exp(cum_p)))
        sv.append(stack(v))
        vs.append(v)
        kend.append(k * p_end)
        bend.append(b * p_end)
        ptot.append(jnp.exp(tot[:, sl]))
        a.append(_bdot_nt(jnp.concatenate([skt[-1], srt[-1]], axis=0), jnp.concatenate([bh, bh, kh, kh], axis=0)))
    a_ab = [jnp.where(strict, a[p][0:c2, 0:c2], 0.0) for p in pairs]
    a_ak = [jnp.where(strict, a[p][0:c2, c2:2 * c2], 0.0) for p in pairs]
    a_rb = [jnp.where(incl2, a[p][c2:2 * c2, 0:c2], 0.0) for p in pairs]
    a_rk = [jnp.where(incl2, a[p][c2:2 * c2, c2:2 * c2], 0.0) for p in pairs]

    t = [eye - a_ab[p] for p in pairs]
    x = a_ab
    n = 2
    while n < c:
        x = [_bdot(x[p], x[p]) for p in pairs]
        t = [t[p] + _bdot(t[p], x[p]) for p in pairs]
        n *= 2

    av = [_bdot(a_ak[p], sv[p]) for p in pairs]
    wu = [_bdot(t[p], jnp.concatenate([skt[p], av[p]], axis=1)) for p in pairs]
    w_s = [wu[p][:, 0:c2] for p in pairs]
    u0_s = [wu[p][:, c2:2 * c2] for p in pairs]
    y0 = [fold(_bdot(jnp.concatenate([a_rk[p], -a_rb[p]], axis=1), jnp.concatenate([sv[p], u0_s[p]], axis=0)))
          for p in pairs]
    y1 = [fold(srt[p] - _bdot(a_rb[p], w_s[p])) for p in pairs]
    h0 = [h_ref[p] for p in pairs]
    for p in pairs:
        y_ref[:, sls[p]] = _bdot(y1[p], h0[p]) + y0[p]
    m = [jnp.where(eye > 0.0, ptot[p], 0.0) - jnp.where(same, _bdot_tn(bend[p], fold(w_s[p])), 0.0) for p in pairs]
    nn = [jnp.where(same, _bdot_tn(jnp.concatenate([kend[p], -bend[p]], axis=0),
                                   jnp.concatenate([vs[p], fold(u0_s[p])], axis=0)), 0.0) for p in pairs]
    for p in pairs:
        h_ref[p] = _dot3(m[p], h0[p]) + nn[p]


def _rwkv_scan(r, v, kap, lw, kd, bd, *, ctx_len):
    bsz, tt, d = r.shape
    nc = tt // CHUNK
    ncc = ctx_len // CHUNK

    def chunk(dr, s):
        return jnp.where(dr == 0, s, jnp.where(s < ncc, ncc - 1 - s, nc + ncc - 1 - s))

    shared = pl.BlockSpec((None, CHUNK, d), lambda b, dr, s: (b, chunk(dr, s), 0))
    per_dir = pl.BlockSpec((None, None, CHUNK, d), lambda b, dr, s: (dr, b, chunk(dr, s), 0))
    return pl.pallas_call(
        _scan_kernel,
        grid=(bsz, 2, nc),
        in_specs=[shared, shared, shared, per_dir, per_dir, per_dir],
        out_specs=per_dir,
        out_shape=jax.ShapeDtypeStruct((2, bsz, tt, d), F32),
        scratch_shapes=[pltpu.VMEM((N_PAIR, LANES, LANES), F32)],
        compiler_params=_cparams(("parallel", "parallel", "arbitrary")),
        name="rwkv_scan",
    )(r, v, kap, lw, kd, bd)


def _rwkv_out_kernel(y_ref, r_ref, v_ref, kd_ref, lora_ref, gg_ref, gb_ref, rk_ref, gup_ref, e_ref, et_ref, o_ref):
    inv = 1.0 / RWKV_HEAD
    y = y_ref[0] + y_ref[1]
    mu = _head_sum(y, e_ref, et_ref) * inv
    dlt = y - mu
    var = _head_sum(dlt * dlt, e_ref, et_ref) * inv
    yn = dlt * lax.rsqrt(var + GN_EPS) * gg_ref[...] + gb_ref[...]
    bonus = _head_sum(r_ref[...] * (kd_ref[0] + kd_ref[1]) * rk_ref[...], e_ref, et_ref) * v_ref[...]
    g = _bdot(_sigmoid(lora_ref[:, 2 * LANES:3 * LANES]), gup_ref[...])
    o_ref[...] = ((yn + bonus) * g).astype(o_ref.dtype)


def _rwkv_out(y, r, v, kd, proj, gn_g, gn_b, r_k, gate_up, e, et, *, tm):
    bsz, tt, d = r.shape
    one = pl.BlockSpec((None, tm, d), lambda b, i: (b, i, 0))
    two = pl.BlockSpec((2, None, tm, d), lambda b, i: (0, b, i, 0))
    full = lambda shape: pl.BlockSpec(shape, lambda b, i: (0,) * len(shape))
    return pl.pallas_call(
        _rwkv_out_kernel,
        grid=(bsz, tt // tm),
        in_specs=[two, one, one, two,
                  pl.BlockSpec((None, tm, 512), lambda b, i: (b, i, COL_LORA // 512)),
                  full((1, d)), full((1, d)), full((1, d)), full((LANES, d)), full((d, LANES)), full((LANES, d))],
        out_specs=one,
        out_shape=jax.ShapeDtypeStruct((bsz, tt, d), BF16),
        compiler_params=_cparams(("parallel", "parallel")),
        name="rwkv_out",
    )(y, r, v, kd, proj, gn_g.reshape(1, d), gn_b.reshape(1, d), r_k.reshape(1, d), gate_up, e, et)


def _merge_kernel(a_ref, att_ref, conv_ref, rw_ref, ga_ref, gc_ref, gr_ref, mod_ref, g_ref,
                  wa_ref, wc_ref, wr_ref, wo_ref, o_ref):
    dot = functools.partial(jnp.dot, preferred_element_type=F32)
    m = (_sigmoid(ga_ref[...]) * dot(att_ref[...], wa_ref[...])
         + _sigmoid(gc_ref[...]) * dot(conv_ref[...], wc_ref[...])
         + _sigmoid(gr_ref[...]) * dot(rw_ref[...], wr_ref[...]))
    z = dot(m.astype(BF16), wo_ref[...])
    zn = z * lax.rsqrt(jnp.mean(z * z, axis=-1, keepdims=True) + EPS) * g_ref[...]
    o_ref[...] = a_ref[...] + mod_ref[2:3, :] * zn


def _merge(a, att, conv, rw, proj, mod3, g, wa, wc, wr, wo, *, n_ctx_tiles, ctx_row, tm):
    bsz, tt, d = a.shape
    cg = COL_GATE // d
    one = pl.BlockSpec((None, tm, d), lambda b, i: (b, i, 0))
    gate = lambda c: pl.BlockSpec((None, tm, d), lambda b, i: (b, i, cg + c))
    wspec = pl.BlockSpec((d, d), lambda b, i: (0, 0))
    return pl.pallas_call(
        _merge_kernel,
        grid=(bsz, tt // tm),
        in_specs=[one, one, one, one, gate(0), gate(1), gate(2),
                  pl.BlockSpec((None, N_MOD, d), _mod_row(n_ctx_tiles, ctx_row)),
                  pl.BlockSpec((1, d), lambda b, i: (0, 0)),
                  wspec, wspec, wspec, wspec],
        out_specs=one,
        out_shape=jax.ShapeDtypeStruct((bsz, tt, d), F32),
        compiler_params=_cparams(("parallel", "parallel")),
        name="merge",
    )(a, att, conv, rw, proj, proj, proj, mod3, g.reshape(1, d), wa, wc, wr, wo)


def _ffn_tail_kernel(a_ref, zc_ref, zp_ref, zn_ref, cw_ref, mod_ref, g_ref, wd_ref, o_ref, buf_ref,
                     *, n_ctx_tiles, n_tiles, tm):
    first, last = _edge_flags(pl.program_id(1), n_ctx_tiles, n_tiles)
    h = SUBLANES
    buf_ref[0:h, :] = jnp.where(first, 0.0, zp_ref[...])
    buf_ref[h:h + tm, :] = zc_ref[...]
    buf_ref[h + tm:h + tm + h, :] = jnp.where(last, 0.0, zn_ref[...])

    def conv(sl):
        return (buf_ref[h - 1:h - 1 + tm, sl] * cw_ref[0:1, sl] + buf_ref[h:h + tm, sl] * cw_ref[1:2, sl]
                + buf_ref[h + 1:h + 1 + tm, sl] * cw_ref[2:3, sl])

    gate = conv(slice(0, D_FF))
    val = conv(slice(D_FF, 2 * D_FF))
    u = (gate * _sigmoid(gate) * val).astype(BF16)
    z = jnp.dot(u, wd_ref[...], preferred_element_type=F32)
    zn = z * lax.rsqrt(jnp.mean(z * z, axis=-1, keepdims=True) + EPS) * g_ref[...]
    o_ref[...] = a_ref[...] + mod_ref[5:6, :] * zn


def _ffn_tail(a, z, conv_w, mod3, g, wd, *, n_ctx_tiles, ctx_row, tm):
    bsz, tt, d = a.shape
    f2 = 2 * D_FF
    prev, nxt = _halo_specs(tm, SUBLANES, tt, 0, f2)
    one = pl.BlockSpec((None, tm, d), lambda b, i: (b, i, 0))
    return pl.pallas_call(
        functools.partial(_ffn_tail_kernel, n_ctx_tiles=n_ctx_tiles, n_tiles=tt // tm, tm=tm),
        grid=(bsz, tt // tm),
        in_specs=[one, pl.BlockSpec((None, tm, f2), lambda b, i: (b, i, 0)), prev, nxt,
                  pl.BlockSpec((3, f2), lambda b, i: (0, 0)),
                  pl.BlockSpec((None, N_MOD, d), _mod_row(n_ctx_tiles, ctx_row)),
                  pl.BlockSpec((1, d), lambda b, i: (0, 0)),
                  pl.BlockSpec((D_FF, d), lambda b, i: (0, 0))],
        out_specs=one,
        out_shape=jax.ShapeDtypeStruct((bsz, tt, d), F32),
        scratch_shapes=[pltpu.VMEM((tm + 2 * SUBLANES, f2), F32)],
        compiler_params=_cparams(("parallel", "parallel")),
        name="ffn_tail",
    )(a, z, z, z, conv_w, mod3, g.reshape(1, d), wd)


def _rope_tables(ctx_len, seq):
    rows = seq // GRID_W
    row = jnp.repeat(jnp.arange(rows, dtype=F32), GRID_W)
    col = jnp.tile(jnp.arange(GRID_W, dtype=F32), rows)
    inv_freq = ROPE_THETA ** (-jnp.arange(N_FREQ, dtype=F32) / N_FREQ)
    ang_r = row[:, None] * inv_freq
    ang_c = col[:, None] * inv_freq
    cos = jnp.concatenate([jnp.cos(ang_r), jnp.cos(ang_r), jnp.cos(ang_c), jnp.cos(ang_c)], axis=-1)
    sin = jnp.concatenate([-jnp.sin(ang_r), jnp.sin(ang_r), -jnp.sin(ang_c), jnp.sin(ang_c)], axis=-1)
    cos = jnp.concatenate([jnp.ones((ctx_len, HEAD_DIM), F32), cos], axis=0)
    sin = jnp.concatenate([jnp.zeros((ctx_len, HEAD_DIM), F32), sin], axis=0)
    return cos, sin


def _reorder_w_in(w):
    qkv = w[:, 0:1536]
    glu = w[:, 1536:3584]
    rkv = w[:, 3584:6656]
    lora = w[:, 6656:7040]
    gates = w[:, 7040:10112]
    pad = jnp.zeros((w.shape[0], LANES), w.dtype)
    return jnp.concatenate([qkv, lora, pad, glu, rkv, gates], axis=1).astype(BF16)


def _pad_lora_up(up):
    z = jnp.zeros_like(up[0])
    return jnp.stack([jnp.concatenate([up[0], z], axis=0), jnp.concatenate([z, up[1]], axis=0)])


def kernel(x, c, ctx, c_ctx, w_mod, b_mod, g_pre_mix, g_post_mix, g_pre_ffn, g_post_ffn, w_in, q_norm, k_norm,
           w_attn_o, conv_w, conv_b, conv_ln_g, conv_ln_b, w_conv_o, shift_w, decay_w0, decay_up, iclr_a0,
           iclr_up, gate_up, k_k, k_a, r_k, wkv_gn_g, wkv_gn_b, w_rwkv_o, w_out, w_ffn_up, ffn_conv_w, w_ffn_down):
    bsz, seq, d = x.shape
    ctx_len = ctx.shape[1]
    depth = w_mod.shape[0]
    tm = min(ROW_TILE, ctx_len)
    assert d == D_MODEL and ctx_len % tm == 0 and seq % tm == 0 and ctx_len % CHUNK == 0 and seq % CHUNK == 0
    n_ctx_tiles = ctx_len // tm

    a = jnp.concatenate([ctx, x], axis=1)
    mod_rows = -(-(bsz + 1) // SUBLANES) * SUBLANES
    cc = jnp.zeros((mod_rows, d), F32).at[:bsz].set(c).at[bsz].set(c_ctx)
    cos, sin = _rope_tables(ctx_len, seq)
    head = jnp.arange(d, dtype=jnp.int32) // RWKV_HEAD
    e = (head[:, None] == jnp.arange(LANES, dtype=jnp.int32)[None, :]).astype(BF16)
    et = e.T
    segs = (ctx_len + seq) // tm
    seg_row = jnp.array([bsz if s % segs < n_ctx_tiles else s // segs for s in range(bsz * segs)], jnp.int32)

    for l in range(depth):
        mod3 = _mod_call(cc, w_mod[l], b_mod[l]).reshape(mod_rows, N_MOD, d)
        modseg = jnp.take(mod3, seg_row, axis=0)
        kw = dict(n_ctx_tiles=n_ctx_tiles, ctx_row=bsz, tm=tm)
        proj = _nm_matmul(a, modseg, g_pre_mix[l], _reorder_w_in(w_in[l]), shift_idx=0, tn=2048, seg=tm)
        q, k, v = _qk_prep(proj, cos, sin, q_norm[l], k_norm[l], tm=tm)
        att = _attention(q, k, v, ctx_len=ctx_len, tq=tm)
        conv = _conformer(proj, conv_w[l], conv_b[l], conv_ln_g[l], conv_ln_b[l], ctx_len=ctx_len, tm=tm)
        r, vv, kap, lw, kd, bd = _rwkv_prep(proj, shift_w[l], k_k[l], k_a[l], decay_w0[l], _pad_lora_up(decay_up[l]),
                                            iclr_a0[l], _pad_lora_up(iclr_up[l]), e, et, ctx_len=ctx_len, tm=tm)
        y = _rwkv_scan(r, vv, kap, lw, kd, bd, ctx_len=ctx_len)
        rw = _rwkv_out(y, r, vv, kd, proj, wkv_gn_g[l], wkv_gn_b[l], r_k[l], gate_up[l].astype(BF16), e, et, tm=tm)
        a = _merge(a, att, conv, rw, proj, mod3, g_post_mix[l], w_attn_o[l].astype(BF16), w_conv_o[l].astype(BF16),
                   w_rwkv_o[l].astype(BF16), w_out[l].astype(BF16), **kw)
        z = _nm_matmul(a, modseg, g_pre_ffn[l], w_ffn_up[l].astype(BF16), shift_idx=3, tn=D_FF // 2, seg=tm)
        a = _ffn_tail(a, z, ffn_conv_w[l], mod3, g_post_ffn[l], w_ffn_down[l].astype(BF16), **kw)
    return a[:, ctx_len:, :]
```

```python
import functools
import math

import jax
import jax.numpy as jnp
from jax import lax
from jax.experimental import pallas as pl
from jax.experimental.pallas import tpu as pltpu

F32 = jnp.float32
BF16 = jnp.bfloat16

D_MODEL = 1024
GRID_W = 64
N_Q_HEADS = 8
N_KV_HEADS = 2
GQA_GROUP = N_Q_HEADS // N_KV_HEADS
HEAD_DIM = 128
N_FREQ = HEAD_DIM // 4
ROPE_THETA = 10000.0
CONV_K = 31
CONV_HALO = 16
RWKV_HEAD = 64
RWKV_HEADS = D_MODEL // RWKV_HEAD
DECAY_SCALE = math.exp(-0.5)
D_FF = 2816
N_MOD = 6
EPS = 1e-6
LN_EPS = 1e-5
GN_EPS = RWKV_HEAD * 1e-5
LANES = 128
SUBLANES = 8
CHUNK = 64
N_PAIR = D_MODEL // LANES
SCAN_BATCH = 2
ROW_TILE = 256
VMEM_LIMIT = 48 * 1024 * 1024

COL_Q = 0
COL_KV = 1024
COL_LORA = 1536
COL_GLU = 2048
COL_RKV = 4096
COL_GATE = 7168
N_IN_PAD = 10240


def _cparams(sem):
    return pltpu.CompilerParams(dimension_semantics=sem, vmem_limit_bytes=VMEM_LIMIT)


def _bdot(a, b):
    return jnp.dot(a.astype(BF16), b.astype(BF16), preferred_element_type=F32)


def _bdot_nt(a, b):
    return lax.dot_general(a.astype(BF16), b.astype(BF16), (((1,), (1,)), ((), ())),
                           preferred_element_type=F32)


def _bdot_tn(a, b):
    return lax.dot_general(a.astype(BF16), b.astype(BF16), (((0,), (0,)), ((), ())),
                           preferred_element_type=F32)


def _split2(x):
    hi = x.astype(BF16)
    lo = (x - hi.astype(F32)).astype(BF16)
    return hi, lo


def _dot3(a, b):
    ah, al = _split2(a)
    bh, bl = _split2(b)
    dot = functools.partial(jnp.dot, preferred_element_type=F32)
    return dot(ah, bh) + dot(al, bh) + dot(ah, bl)


def _sigmoid(x):
    return jax.nn.sigmoid(x)


def _head_sum(x, e_ref, et_ref):
    xh, xl = _split2(x)
    dot = functools.partial(jnp.dot, preferred_element_type=F32)
    s = dot(xh, e_ref[...]) + dot(xl, e_ref[...])
    sh, sl = _split2(s)
    return dot(sh, et_ref[...]) + dot(sl, et_ref[...])


def _mod_kernel(c_ref, w_ref, b_ref, o_ref):
    c = c_ref[...]
    o_ref[...] = _dot3(c * _sigmoid(c), w_ref[...]) + b_ref[...]


def _mod_call(cc, w_mod, b_mod):
    rows = cc.shape[0]
    n = w_mod.shape[1]
    tn = 1536
    return pl.pallas_call(
        _mod_kernel,
        grid=(n // tn,),
        in_specs=[pl.BlockSpec((rows, D_MODEL), lambda j: (0, 0)),
                  pl.BlockSpec((D_MODEL, tn), lambda j: (0, j)),
                  pl.BlockSpec((1, tn), lambda j: (0, j))],
        out_specs=pl.BlockSpec((rows, tn), lambda j: (0, j)),
        out_shape=jax.ShapeDtypeStruct((rows, n), F32),
        compiler_params=_cparams(("parallel",)),
        name="mod",
    )(cc, w_mod, b_mod.reshape(1, n))


def _mod_row(n_ctx_tiles, ctx_row):
    return lambda b, i, *_: (jnp.where(i < n_ctx_tiles, ctx_row, b), 0, 0)


def _nm_matmul_kernel(a_ref, mod_ref, g_ref, w_ref, o_ref, h_ref, *, shift_idx, seg, nseg):
    @pl.when(pl.program_id(1) == 0)
    def _():
        for s in range(nseg):
            rows = slice(s * seg, (s + 1) * seg)
            x = a_ref[rows, :]
            y = x * lax.rsqrt(jnp.mean(x * x, axis=-1, keepdims=True) + EPS) * g_ref[...]
            h = y * (1.0 + mod_ref[s, shift_idx + 1:shift_idx + 2, :]) + mod_ref[s, shift_idx:shift_idx + 1, :]
            h_ref[rows, :] = h.astype(BF16)

    o_ref[...] = jnp.dot(h_ref[...], w_ref[...], preferred_element_type=F32).astype(o_ref.dtype)


def _nm_matmul(a, modseg, g, w, *, shift_idx, tn, seg):
    bsz, tt, d = a.shape
    n = w.shape[1]
    rows = bsz * tt
    nseg = max(s for s in (4, 2, 1) if (rows // seg) % s == 0)
    tm = seg * nseg
    out = pl.pallas_call(
        functools.partial(_nm_matmul_kernel, shift_idx=shift_idx, seg=seg, nseg=nseg),
        grid=(rows // tm, n // tn),
        in_specs=[pl.BlockSpec((tm, d), lambda i, j: (i, 0)),
                  pl.BlockSpec((nseg, N_MOD, d), lambda i, j: (i, 0, 0)),
                  pl.BlockSpec((1, d), lambda i, j: (0, 0)),
                  pl.BlockSpec((d, tn), lambda i, j: (0, j))],
        out_specs=pl.BlockSpec((tm, tn), lambda i, j: (i, j)),
        out_shape=jax.ShapeDtypeStruct((rows, n), F32),
        scratch_shapes=[pltpu.VMEM((tm, d), BF16)],
        compiler_params=_cparams(("parallel", "arbitrary")),
        name="norm_mod_matmul",
    )(a.reshape(rows, d), modseg, g.reshape(1, d), w)
    return out.reshape(bsz, tt, n)


def _qk_prep_kernel(q_ref, kv_ref, cos_ref, sin_ref, qn_ref, kn_ref, qo_ref, ko_ref, vo_ref):
    cos = cos_ref[...]
    sin = sin_ref[...]
    lane = lax.broadcasted_iota(jnp.int32, cos.shape, 1)
    first = (lane & (N_FREQ)) == 0

    def norm_rope(x, g):
        y = x * lax.rsqrt(jnp.mean(x * x, axis=-1, keepdims=True) + EPS) * g
        partner = jnp.where(first, pltpu.roll(y, LANES - N_FREQ, 1), pltpu.roll(y, N_FREQ, 1))
        return y * cos + partner * sin

    for h in range(N_Q_HEADS):
        sl = slice(h * HEAD_DIM, (h + 1) * HEAD_DIM)
        qo_ref[:, sl] = norm_rope(q_ref[:, sl], qn_ref[...]).astype(BF16)
    for h in range(N_KV_HEADS):
        sl = slice(h * HEAD_DIM, (h + 1) * HEAD_DIM)
        ko_ref[:, sl] = norm_rope(kv_ref[:, sl], kn_ref[...]).astype(BF16)
    vo_ref[...] = kv_ref[:, N_KV_HEADS * HEAD_DIM:].astype(BF16)


def _qk_prep(proj, cos, sin, q_norm, k_norm, *, tm):
    bsz, tt, _ = proj.shape
    dq = N_Q_HEADS * HEAD_DIM
    dkv = N_KV_HEADS * HEAD_DIM
    return pl.pallas_call(
        _qk_prep_kernel,
        grid=(bsz, tt // tm),
        in_specs=[pl.BlockSpec((None, tm, dq), lambda b, i: (b, i, COL_Q // dq)),
                  pl.BlockSpec((None, tm, 2 * dkv), lambda b, i: (b, i, COL_KV // (2 * dkv))),
                  pl.BlockSpec((tm, HEAD_DIM), lambda b, i: (i, 0)),
                  pl.BlockSpec((tm, HEAD_DIM), lambda b, i: (i, 0)),
                  pl.BlockSpec((1, HEAD_DIM), lambda b, i: (0, 0)),
                  pl.BlockSpec((1, HEAD_DIM), lambda b, i: (0, 0))],
        out_specs=[pl.BlockSpec((None, tm, dq), lambda b, i: (b, i, 0)),
                   pl.BlockSpec((None, tm, dkv), lambda b, i: (b, i, 0)),
                   pl.BlockSpec((None, tm, dkv), lambda b, i: (b, i, 0))],
        out_shape=[jax.ShapeDtypeStruct((bsz, tt, dq), BF16),
                   jax.ShapeDtypeStruct((bsz, tt, dkv), BF16),
                   jax.ShapeDtypeStruct((bsz, tt, dkv), BF16)],
        compiler_params=_cparams(("parallel", "parallel")),
        name="qk_prep",
    )(proj, proj, cos, sin, q_norm.reshape(1, HEAD_DIM), k_norm.reshape(1, HEAD_DIM))


def _attn_kernel(q_ref, k_ref, v_ref, o_ref, *, n_ctx_tiles, ctx_len):
    c = (HEAD_DIM ** -0.5) * math.log2(math.e)

    def attend(k, v):
        for g in range(GQA_GROUP):
            sl = slice(g * HEAD_DIM, (g + 1) * HEAD_DIM)
            s = lax.dot_general(q_ref[:, sl], k, (((1,), (1,)), ((), ())), preferred_element_type=F32)
            m = jnp.max(s, axis=-1, keepdims=True)
            p = jnp.exp2((s - m) * c)
            l = jnp.sum(p, axis=-1, keepdims=True)
            o = jnp.dot(p.astype(BF16), v, preferred_element_type=F32) / l
            o_ref[:, sl] = o.astype(o_ref.dtype)

    i = pl.program_id(2)

    @pl.when(i < n_ctx_tiles)
    def _():
        attend(k_ref[0:ctx_len, :], v_ref[0:ctx_len, :])

    @pl.when(i >= n_ctx_tiles)
    def _():
        attend(k_ref[...], v_ref[...])


def _attention(q, k, v, *, ctx_len, tq):
    bsz, tt, dq = q.shape
    gw = GQA_GROUP * HEAD_DIM
    return pl.pallas_call(
        functools.partial(_attn_kernel, n_ctx_tiles=ctx_len // tq, ctx_len=ctx_len),
        grid=(bsz, N_KV_HEADS, tt // tq),
        in_specs=[pl.BlockSpec((None, tq, gw), lambda b, h, i: (b, i, h)),
                  pl.BlockSpec((None, tt, HEAD_DIM), lambda b, h, i: (b, 0, h)),
                  pl.BlockSpec((None, tt, HEAD_DIM), lambda b, h, i: (b, 0, h))],
        out_specs=pl.BlockSpec((None, tq, gw), lambda b, h, i: (b, i, h)),
        out_shape=jax.ShapeDtypeStruct((bsz, tt, dq), BF16),
        compiler_params=_cparams(("parallel", "parallel", "arbitrary")),
        name="attention",
    )(q, k, v)


def _halo_specs(tm, halo, tt, col_block, width):
    per = tm // halo
    last = tt // halo - 1
    prev = pl.BlockSpec((None, halo, width), lambda b, i: (b, jnp.maximum(i * per - 1, 0), col_block))
    nxt = pl.BlockSpec((None, halo, width), lambda b, i: (b, jnp.minimum((i + 1) * per, last), col_block))
    return prev, nxt


def _edge_flags(i, n_ctx_tiles, n_tiles):
    first = jnp.logical_or(i == 0, i == n_ctx_tiles)
    last = jnp.logical_or(i == n_ctx_tiles - 1, i == n_tiles - 1)
    return first, last


def _conformer_kernel(ac_ref, bc_ref, ap_ref, bp_ref, an_ref, bn_ref, w_ref, cb_ref, lg_ref, lb_ref,
                      o_ref, buf_ref, sh_ref, *, n_ctx_tiles, n_tiles, tm, rc):
    first, last = _edge_flags(pl.program_id(1), n_ctx_tiles, n_tiles)
    h = CONV_HALO
    buf_ref[0:h, :] = jnp.where(first, 0.0, ap_ref[...] * _sigmoid(bp_ref[...]))
    buf_ref[h:h + tm, :] = ac_ref[...] * _sigmoid(bc_ref[...])
    buf_ref[h + tm:h + tm + h, :] = jnp.where(last, 0.0, an_ref[...] * _sigmoid(bn_ref[...]))
    n_sh = sh_ref.shape[1]
    for s in range(1, SUBLANES):
        sh_ref[s - 1] = buf_ref[s:s + n_sh, :]
    off = h - CONV_K // 2
    for c in range(tm // rc):
        acc = jnp.zeros((rc, D_MODEL), F32)
        for j in range(CONV_K):
            s = (j + off) % SUBLANES
            r0 = c * rc + (j + off) - s
            tap = buf_ref[r0:r0 + rc, :] if s == 0 else sh_ref[s - 1, r0:r0 + rc, :]
            acc = acc + tap * w_ref[j:j + 1, :]
        y = acc + cb_ref[...]
        mu = jnp.mean(y, axis=-1, keepdims=True)
        dlt = y - mu
        var = jnp.mean(dlt * dlt, axis=-1, keepdims=True)
        z = dlt * lax.rsqrt(var + LN_EPS) * lg_ref[...] + lb_ref[...]
        o_ref[c * rc:(c + 1) * rc, :] = (z * _sigmoid(z)).astype(o_ref.dtype)


def _conformer(proj, conv_w, conv_b, ln_g, ln_b, *, ctx_len, tm):
    bsz, tt, _ = proj.shape
    d = D_MODEL
    ca = COL_GLU // d
    prev_a, next_a = _halo_specs(tm, CONV_HALO, tt, ca, d)
    prev_b, next_b = _halo_specs(tm, CONV_HALO, tt, ca + 1, d)
    vec = lambda: pl.BlockSpec((1, d), lambda b, i: (0, 0))
    return pl.pallas_call(
        functools.partial(_conformer_kernel, n_ctx_tiles=ctx_len // tm, n_tiles=tt // tm, tm=tm, rc=32),
        grid=(bsz, tt // tm),
        in_specs=[pl.BlockSpec((None, tm, d), lambda b, i: (b, i, ca)),
                  pl.BlockSpec((None, tm, d), lambda b, i: (b, i, ca + 1)),
                  prev_a, prev_b, next_a, next_b,
                  pl.BlockSpec((CONV_K, d), lambda b, i: (0, 0)),
                  vec(), vec(), vec()],
        out_specs=pl.BlockSpec((None, tm, d), lambda b, i: (b, i, 0)),
        out_shape=jax.ShapeDtypeStruct((bsz, tt, d), BF16),
        scratch_shapes=[pltpu.VMEM((tm + 2 * CONV_HALO, d), F32),
                        pltpu.VMEM((SUBLANES - 1, tm + 2 * CONV_HALO - SUBLANES, d), F32)],
        compiler_params=_cparams(("parallel", "parallel")),
        name="conformer",
    )(proj, proj, proj, proj, proj, proj, conv_w, conv_b.reshape(1, d), ln_g.reshape(1, d), ln_b.reshape(1, d))


def _rwkv_prep_kernel(rc_ref, kc_ref, vc_ref, rp_ref, kp_ref, vp_ref, rn_ref, kn_ref, vn_ref, lora_ref,
                      sw_ref, kk_ref, ka_ref, w0_ref, dup_ref, a0_ref, iup_ref, e_ref, et_ref,
                      r_ref, v_ref, kap_ref, lw_ref, kd_ref, bd_ref, buf_ref,
                      *, n_ctx_tiles, n_tiles, tm):
    first, last = _edge_flags(pl.program_id(1), n_ctx_tiles, n_tiles)
    h = SUBLANES
    d = D_MODEL

    def shift(cur_ref, prev_ref, next_ref, col):
        buf_ref[0:h, :] = jnp.where(first, 0.0, prev_ref[...])
        buf_ref[h:h + tm, :] = cur_ref[...]
        buf_ref[h + tm:h + tm + h, :] = jnp.where(last, 0.0, next_ref[...])
        sl = slice(col * d, (col + 1) * d)
        return (buf_ref[h - 1:h - 1 + tm, :] * sw_ref[0:1, sl] + buf_ref[h:h + tm, :] * sw_ref[1:2, sl]
                + buf_ref[h + 1:h + 1 + tm, :] * sw_ref[2:3, sl])

    r = shift(rc_ref, rp_ref, rn_ref, 0)
    k = shift(kc_ref, kp_ref, kn_ref, 1)
    v = shift(vc_ref, vp_ref, vn_ref, 2)
    r_ref[...] = r
    v_ref[...] = v
    kk = k * kk_ref[...]
    ss = _head_sum(kk * kk, e_ref, et_ref)
    kap = kk * lax.rsqrt(jnp.maximum(ss, 1e-12))
    kap_ref[...] = kap
    tw = jnp.tanh(lora_ref[:, 0:LANES])
    la = lora_ref[:, LANES:2 * LANES]
    for dr in range(2):
        z = w0_ref[dr:dr + 1, :] + _dot3(tw, dup_ref[dr])
        lw_ref[dr] = -DECAY_SCALE * _sigmoid(z)
        a = _sigmoid(a0_ref[dr:dr + 1, :] + _dot3(la, iup_ref[dr]))
        kd_ref[dr] = k * (1.0 + (a - 1.0) * ka_ref[...])
        bd_ref[dr] = a * kap


def _rwkv_prep(proj, shift_w, k_k, k_a, decay_w0, decay_up_pad, iclr_a0, iclr_up_pad, e, et, *, ctx_len, tm):
    bsz, tt, _ = proj.shape
    d = D_MODEL
    c0 = COL_RKV // d
    cur = lambda c: pl.BlockSpec((None, tm, d), lambda b, i: (b, i, c))
    halos = [_halo_specs(tm, SUBLANES, tt, c0 + c, d) for c in range(3)]
    full = lambda shape: pl.BlockSpec(shape, lambda b, i: (0,) * len(shape))
    out1 = pl.BlockSpec((None, tm, d), lambda b, i: (b, i, 0))
    out2 = pl.BlockSpec((2, None, tm, d), lambda b, i: (0, b, i, 0))
    s1 = jax.ShapeDtypeStruct((bsz, tt, d), F32)
    s2 = jax.ShapeDtypeStruct((2, bsz, tt, d), F32)
    return pl.pallas_call(
        functools.partial(_rwkv_prep_kernel, n_ctx_tiles=ctx_len // tm, n_tiles=tt // tm, tm=tm),
        grid=(bsz, tt // tm),
        in_specs=[cur(c0), cur(c0 + 1), cur(c0 + 2),
                  halos[0][0], halos[1][0], halos[2][0], halos[0][1], halos[1][1], halos[2][1],
                  pl.BlockSpec((None, tm, 512), lambda b, i: (b, i, COL_LORA // 512)),
                  full((3, 3 * d)), full((1, d)), full((1, d)), full((2, d)), full((2, LANES, d)),
                  full((2, d)), full((2, LANES, d)), full((d, LANES)), full((LANES, d))],
        out_specs=[out1, out1, out1, out2, out2, out2],
        out_shape=[s1, s1, s1, s2, s2, s2],
        scratch_shapes=[pltpu.VMEM((tm + 2 * SUBLANES, d), F32)],
        compiler_params=_cparams(("parallel", "parallel")),
        name="rwkv_prep",
    )(proj, proj, proj, proj, proj, proj, proj, proj, proj, proj,
      shift_w, k_k.reshape(1, d), k_a.reshape(1, d), decay_w0, decay_up_pad, iclr_a0, iclr_up_pad, e, et)


def _scan_kernel(r_ref, v_ref, kap_ref, lw_ref, k_ref, b_ref, y_ref, h_ref):
    c = CHUNK
    c2 = 2 * c
    sgn = 1 - 2 * pl.program_id(1)

    @pl.when(pl.program_id(2) == 0)
    def _():
        h_ref[...] = jnp.zeros_like(h_ref)

    row = lax.broadcasted_iota(jnp.int32, (c, c), 0)
    col = lax.broadcasted_iota(jnp.int32, (c, c), 1)
    incl = jnp.where((col - row) * sgn <= 0, 1.0, 0.0).astype(BF16)
    dot = functools.partial(jnp.dot, preferred_element_type=F32)
    nb = lw_ref.shape[0]
    cum = []
    for bi in range(nb):
        lw = lw_ref[bi]
        hi = lw.astype(BF16)
        rem = lw - hi.astype(F32)
        mid = rem.astype(BF16)
        lo = (rem - mid.astype(F32)).astype(BF16)
        cum.append(dot(incl, hi) + dot(incl, mid) + dot(incl, lo))

    prow = lax.broadcasted_iota(jnp.int32, (c, c2), 0)
    pcol = lax.broadcasted_iota(jnp.int32, (c, c2), 1)
    order = ((pcol & (c - 1)) - prow) * sgn
    strict = order < 0
    upto = order <= 0
    eye = order == 0
    head0 = pcol < RWKV_HEAD

    def stack(x):
        return jnp.concatenate([jnp.where(head0, x, 0.0), jnp.where(head0, 0.0, x)], axis=0)

    def pack(x):
        return jnp.where(head0, x[0:c], x[c:c2])

    units = [(bi, slice(p * LANES, (p + 1) * LANES)) for bi in range(nb) for p in range(N_PAIR)]
    pairs = range(len(units))
    kt, rt, vs, kend, bend, ptot, a = [], [], [], [], [], [], []
    for bi, sl in units:
        cum_p = cum[bi][:, sl]
        lw_p = lw_ref[bi, :, sl]
        tot_p = jnp.sum(lw_p, axis=0, keepdims=True)
        p_inv = jnp.exp(-cum_p)
        p_end = jnp.exp(tot_p - cum_p)
        k = k_ref[bi, :, sl]
        b = b_ref[bi, :, sl]
        kt.append(kap_ref[bi, :, sl] * jnp.exp(cum_p - lw_p))
        rt.append(r_ref[bi, :, sl] * jnp.exp(cum_p))
        vs.append(v_ref[bi, :, sl])
        kend.append(k * p_end)
        bend.append(b * p_end)
        ptot.append(jnp.exp(tot_p))
        a.append(_bdot_nt(jnp.concatenate([kt[-1], rt[-1]], axis=0),
                          jnp.concatenate([stack(b * p_inv), stack(k * p_inv)], axis=0)))
    a_ab = [jnp.where(strict, a[p][0:c, 0:c2], 0.0) for p in pairs]
    a_ak = [jnp.where(strict, a[p][0:c, c2:2 * c2], 0.0) for p in pairs]
    a_rb = [jnp.where(upto, a[p][c:c2, 0:c2], 0.0) for p in pairs]
    a_rk = [jnp.where(upto, a[p][c:c2, c2:2 * c2], 0.0) for p in pairs]

    t = [jnp.where(eye, 1.0, 0.0) - a_ab[p] for p in pairs]
    x = [_bdot(a_ab[p], stack(a_ab[p])) for p in pairs]
    n = 2
    while 2 * n < c:
        xt = [_bdot(jnp.concatenate([x[p], t[p]], axis=0), stack(x[p])) for p in pairs]
        t = [t[p] + xt[p][c:c2] for p in pairs]
        x = [xt[p][0:c] for p in pairs]
        n *= 2
    t = [t[p] + _bdot(t[p], stack(x[p])) for p in pairs]

    asv = [_bdot(jnp.concatenate([a_ak[p], a_rk[p]], axis=0), stack(vs[p])) for p in pairs]
    wu = [_bdot(t[p], jnp.concatenate([stack(kt[p]), stack(asv[p][0:c])], axis=1)) for p in pairs]
    w = [wu[p][:, 0:c2] for p in pairs]
    u0 = [wu[p][:, c2:2 * c2] for p in pairs]
    arb = [_bdot(a_rb[p], jnp.concatenate([stack(w[p]), stack(u0[p])], axis=1)) for p in pairs]
    y0 = [asv[p][c:c2] - arb[p][:, c2:2 * c2] for p in pairs]
    y1 = [rt[p] - arb[p][:, 0:c2] for p in pairs]
    m = [jnp.where(eye, ptot[p], 0.0) - pack(_bdot_tn(bend[p], w[p])) for p in pairs]
    nn = [pack(_bdot_tn(jnp.concatenate([kend[p], -bend[p]], axis=0), jnp.concatenate([vs[p], u0[p]], axis=0)))
          for p in pairs]
    for p in pairs:
        h0 = h_ref[p]
        hh = h0.astype(BF16)
        hl = h0 - hh.astype(F32)
        mh, ml = _split2(m[p])
        top = _bdot(jnp.concatenate([mh, ml, y1[p].astype(BF16)], axis=0), stack(hh.astype(F32)))
        y_ref[units[p][0], :, units[p][1]] = top[c2:c2 + c] + y0[p]
        h_ref[p] = top[0:c] + top[c:c2] + _bdot(mh, stack(hl)) + nn[p]


def _rwkv_scan(r, v, kap, lw, kd, bd, *, ctx_len):
    bsz, tt, d = r.shape
    nc = tt // CHUNK
    ncc = ctx_len // CHUNK

    def chunk(dr, s):
        return jnp.where(dr == 0, s, jnp.where(s < ncc, ncc - 1 - s, nc + ncc - 1 - s))

    ub = SCAN_BATCH if bsz % SCAN_BATCH == 0 else 1
    shared = pl.BlockSpec((ub, CHUNK, d), lambda b, dr, s: (b, chunk(dr, s), 0))
    per_dir = pl.BlockSpec((None, ub, CHUNK, d), lambda b, dr, s: (dr, b, chunk(dr, s), 0))
    return pl.pallas_call(
        _scan_kernel,
        grid=(bsz // ub, 2, nc),
        in_specs=[shared, shared, shared, per_dir, per_dir, per_dir],
        out_specs=per_dir,
        out_shape=jax.ShapeDtypeStruct((2, bsz, tt, d), F32),
        scratch_shapes=[pltpu.VMEM((ub * N_PAIR, CHUNK, LANES), F32)],
        compiler_params=_cparams(("parallel", "parallel", "arbitrary")),
        name="rwkv_scan",
    )(r, v, kap, lw, kd, bd)


def _rwkv_out_kernel(y_ref, r_ref, v_ref, kd_ref, lora_ref, gg_ref, gb_ref, rk_ref, gup_ref, e_ref, et_ref, o_ref):
    inv = 1.0 / RWKV_HEAD
    y = y_ref[0] + y_ref[1]
    mu = _head_sum(y, e_ref, et_ref) * inv
    dlt = y - mu
    var = _head_sum(dlt * dlt, e_ref, et_ref) * inv
    yn = dlt * lax.rsqrt(var + GN_EPS) * gg_ref[...] + gb_ref[...]
    bonus = _head_sum(r_ref[...] * (kd_ref[0] + kd_ref[1]) * rk_ref[...], e_ref, et_ref) * v_ref[...]
    g = _bdot(_sigmoid(lora_ref[:, 2 * LANES:3 * LANES]), gup_ref[...])
    o_ref[...] = ((yn + bonus) * g).astype(o_ref.dtype)


def _rwkv_out(y, r, v, kd, proj, gn_g, gn_b, r_k, gate_up, e, et, *, tm):
    bsz, tt, d = r.shape
    one = pl.BlockSpec((None, tm, d), lambda b, i: (b, i, 0))
    two = pl.BlockSpec((2, None, tm, d), lambda b, i: (0, b, i, 0))
    full = lambda shape: pl.BlockSpec(shape, lambda b, i: (0,) * len(shape))
    return pl.pallas_call(
        _rwkv_out_kernel,
        grid=(bsz, tt // tm),
        in_specs=[two, one, one, two,
                  pl.BlockSpec((None, tm, 512), lambda b, i: (b, i, COL_LORA // 512)),
                  full((1, d)), full((1, d)), full((1, d)), full((LANES, d)), full((d, LANES)), full((LANES, d))],
        out_specs=one,
        out_shape=jax.ShapeDtypeStruct((bsz, tt, d), BF16),
        compiler_params=_cparams(("parallel", "parallel")),
        name="rwkv_out",
    )(y, r, v, kd, proj, gn_g.reshape(1, d), gn_b.reshape(1, d), r_k.reshape(1, d), gate_up, e, et)


def _merge_kernel(a_ref, att_ref, conv_ref, rw_ref, ga_ref, gc_ref, gr_ref, mod_ref, g_ref,
                  wa_ref, wc_ref, wr_ref, wo_ref, o_ref):
    dot = functools.partial(jnp.dot, preferred_element_type=F32)
    m = (_sigmoid(ga_ref[...]) * dot(att_ref[...], wa_ref[...])
         + _sigmoid(gc_ref[...]) * dot(conv_ref[...], wc_ref[...])
         + _sigmoid(gr_ref[...]) * dot(rw_ref[...], wr_ref[...]))
    z = dot(m.astype(BF16), wo_ref[...])
    zn = z * lax.rsqrt(jnp.mean(z * z, axis=-1, keepdims=True) + EPS) * g_ref[...]
    o_ref[...] = a_ref[...] + mod_ref[2:3, :] * zn


def _merge(a, att, conv, rw, proj, mod3, g, wa, wc, wr, wo, *, n_ctx_tiles, ctx_row, tm):
    bsz, tt, d = a.shape
    cg = COL_GATE // d
    one = pl.BlockSpec((None, tm, d), lambda b, i: (b, i, 0))
    gate = lambda c: pl.BlockSpec((None, tm, d), lambda b, i: (b, i, cg + c))
    wspec = pl.BlockSpec((d, d), lambda b, i: (0, 0))
    return pl.pallas_call(
        _merge_kernel,
        grid=(bsz, tt // tm),
        in_specs=[one, one, one, one, gate(0), gate(1), gate(2),
                  pl.BlockSpec((None, N_MOD, d), _mod_row(n_ctx_tiles, ctx_row)),
                  pl.BlockSpec((1, d), lambda b, i: (0, 0)),
                  wspec, wspec, wspec, wspec],
        out_specs=one,
        out_shape=jax.ShapeDtypeStruct((bsz, tt, d), F32),
        compiler_params=_cparams(("parallel", "parallel")),
        name="merge",
    )(a, att, conv, rw, proj, proj, proj, mod3, g.reshape(1, d), wa, wc, wr, wo)


def _ffn_tail_kernel(a_ref, zc_ref, zp_ref, zn_ref, cw_ref, mod_ref, g_ref, wd_ref, o_ref, buf_ref,
                     *, n_ctx_tiles, n_tiles, tm):
    first, last = _edge_flags(pl.program_id(1), n_ctx_tiles, n_tiles)
    h = SUBLANES
    buf_ref[0:h, :] = jnp.where(first, 0.0, zp_ref[...])
    buf_ref[h:h + tm, :] = zc_ref[...]
    buf_ref[h + tm:h + tm + h, :] = jnp.where(last, 0.0, zn_ref[...])

    def conv(sl):
        return (buf_ref[h - 1:h - 1 + tm, sl] * cw_ref[0:1, sl] + buf_ref[h:h + tm, sl] * cw_ref[1:2, sl]
                + buf_ref[h + 1:h + 1 + tm, sl] * cw_ref[2:3, sl])

    gate = conv(slice(0, D_FF))
    val = conv(slice(D_FF, 2 * D_FF))
    u = (gate * _sigmoid(gate) * val).astype(BF16)
    z = jnp.dot(u, wd_ref[...], preferred_element_type=F32)
    zn = z * lax.rsqrt(jnp.mean(z * z, axis=-1, keepdims=True) + EPS) * g_ref[...]
    o_ref[...] = a_ref[...] + mod_ref[5:6, :] * zn


def _ffn_tail(a, z, conv_w, mod3, g, wd, *, n_ctx_tiles, ctx_row, tm):
    bsz, tt, d = a.shape
    f2 = 2 * D_FF
    prev, nxt = _halo_specs(tm, SUBLANES, tt, 0, f2)
    one = pl.BlockSpec((None, tm, d), lambda b, i: (b, i, 0))
    return pl.pallas_call(
        functools.partial(_ffn_tail_kernel, n_ctx_tiles=n_ctx_tiles, n_tiles=tt // tm, tm=tm),
        grid=(bsz, tt // tm),
        in_specs=[one, pl.BlockSpec((None, tm, f2), lambda b, i: (b, i, 0)), prev, nxt,
                  pl.BlockSpec((3, f2), lambda b, i: (0, 0)),
                  pl.BlockSpec((None, N_MOD, d), _mod_row(n_ctx_tiles, ctx_row)),
                  pl.BlockSpec((1, d), lambda b, i: (0, 0)),
                  pl.BlockSpec((D_FF, d), lambda b, i: (0, 0))],
        out_specs=one,
        out_shape=jax.ShapeDtypeStruct((bsz, tt, d), F32),
        scratch_shapes=[pltpu.VMEM((tm + 2 * SUBLANES, f2), F32)],
        compiler_params=_cparams(("parallel", "parallel")),
        name="ffn_tail",
    )(a, z, z, z, conv_w, mod3, g.reshape(1, d), wd)


def _rope_tables(ctx_len, seq):
    rows = seq // GRID_W
    row = jnp.repeat(jnp.arange(rows, dtype=F32), GRID_W)
    col = jnp.tile(jnp.arange(GRID_W, dtype=F32), rows)
    inv_freq = ROPE_THETA ** (-jnp.arange(N_FREQ, dtype=F32) / N_FREQ)
    ang_r = row[:, None] * inv_freq
    ang_c = col[:, None] * inv_freq
    cos = jnp.concatenate([jnp.cos(ang_r), jnp.cos(ang_r), jnp.cos(ang_c), jnp.cos(ang_c)], axis=-1)
    sin = jnp.concatenate([-jnp.sin(ang_r), jnp.sin(ang_r), -jnp.sin(ang_c), jnp.sin(ang_c)], axis=-1)
    cos = jnp.concatenate([jnp.ones((ctx_len, HEAD_DIM), F32), cos], axis=0)
    sin = jnp.concatenate([jnp.zeros((ctx_len, HEAD_DIM), F32), sin], axis=0)
    return cos, sin


def _reorder_w_in(w):
    qkv = w[:, 0:1536]
    glu = w[:, 1536:3584]
    rkv = w[:, 3584:6656]
    lora = w[:, 6656:7040]
    gates = w[:, 7040:10112]
    pad = jnp.zeros((w.shape[0], LANES), w.dtype)
    return jnp.concatenate([qkv, lora, pad, glu, rkv, gates], axis=1).astype(BF16)


def _pad_lora_up(up):
    z = jnp.zeros_like(up[0])
    return jnp.stack([jnp.concatenate([up[0], z], axis=0), jnp.concatenate([z, up[1]], axis=0)])


def kernel(x, c, ctx, c_ctx, w_mod, b_mod, g_pre_mix, g_post_mix, g_pre_ffn, g_post_ffn, w_in, q_norm, k_norm,
           w_attn_o, conv_w, conv_b, conv_ln_g, conv_ln_b, w_conv_o, shift_w, decay_w0, decay_up, iclr_a0,
           iclr_up, gate_up, k_k, k_a, r_k, wkv_gn_g, wkv_gn_b, w_rwkv_o, w_out, w_ffn_up, ffn_conv_w, w_ffn_down):
    bsz, seq, d = x.shape
    ctx_len = ctx.shape[1]
    depth = w_mod.shape[0]
    tm = min(ROW_TILE, ctx_len)
    assert d == D_MODEL and ctx_len % tm == 0 and seq % tm == 0 and ctx_len % CHUNK == 0 and seq % CHUNK == 0
    n_ctx_tiles = ctx_len // tm

    a = jnp.concatenate([ctx, x], axis=1)
    mod_rows = -(-(bsz + 1) // SUBLANES) * SUBLANES
    cc = jnp.zeros((mod_rows, d), F32).at[:bsz].set(c).at[bsz].set(c_ctx)
    cos, sin = _rope_tables(ctx_len, seq)
    head = jnp.arange(d, dtype=jnp.int32) // RWKV_HEAD
    e = (head[:, None] == jnp.arange(LANES, dtype=jnp.int32)[None, :]).astype(BF16)
    et = e.T
    segs = (ctx_len + seq) // tm
    seg_row = jnp.array([bsz if s % segs < n_ctx_tiles else s // segs for s in range(bsz * segs)], jnp.int32)

    for l in range(depth):
        mod3 = _mod_call(cc, w_mod[l], b_mod[l]).reshape(mod_rows, N_MOD, d)
        modseg = jnp.take(mod3, seg_row, axis=0)
        kw = dict(n_ctx_tiles=n_ctx_tiles, ctx_row=bsz, tm=tm)
        proj = _nm_matmul(a, modseg, g_pre_mix[l], _reorder_w_in(w_in[l]), shift_idx=0, tn=2048, seg=tm)
        q, k, v = _qk_prep(proj, cos, sin, q_norm[l], k_norm[l], tm=tm)
        att = _attention(q, k, v, ctx_len=ctx_len, tq=tm)
        conv = _conformer(proj, conv_w[l], conv_b[l], conv_ln_g[l], conv_ln_b[l], ctx_len=ctx_len, tm=tm)
        r, vv, kap, lw, kd, bd = _rwkv_prep(proj, shift_w[l], k_k[l], k_a[l], decay_w0[l], _pad_lora_up(decay_up[l]),
                                            iclr_a0[l], _pad_lora_up(iclr_up[l]), e, et, ctx_len=ctx_len, tm=tm)
        y = _rwkv_scan(r, vv, kap, lw, kd, bd, ctx_len=ctx_len)
        rw = _rwkv_out(y, r, vv, kd, proj, wkv_gn_g[l], wkv_gn_b[l], r_k[l], gate_up[l].astype(BF16), e, et, tm=tm)
        a = _merge(a, att, conv, rw, proj, mod3, g_post_mix[l], w_attn_o[l].astype(BF16), w_conv_o[l].astype(BF16),
                   w_rwkv_o[l].astype(BF16), w_out[l].astype(BF16), **kw)
        z = _nm_matmul(a, modseg, g_pre_ffn[l], w_ffn_up[l].astype(BF16), shift_idx=3, tn=D_FF // 2, seg=tm)
        a = _ffn_tail(a, z, ffn_conv_w[l], mod3, g_post_ffn[l], w_ffn_down[l].astype(BF16), **kw)
    return a[:, ctx_len:, :]
```

```python
import functools
import math

import jax
import jax.numpy as jnp
from jax import lax
from jax.experimental import pallas as pl
from jax.experimental.pallas import tpu as pltpu

F32 = jnp.float32
BF16 = jnp.bfloat16

D_MODEL = 1024
GRID_W = 64
N_Q_HEADS = 8
N_KV_HEADS = 2
GQA_GROUP = N_Q_HEADS // N_KV_HEADS
HEAD_DIM = 128
N_FREQ = HEAD_DIM // 4
ROPE_THETA = 10000.0
CONV_K = 31
CONV_HALO = 16
RWKV_HEAD = 64
RWKV_HEADS = D_MODEL // RWKV_HEAD
DECAY_SCALE = math.exp(-0.5)
D_FF = 2816
N_MOD = 6
EPS = 1e-6
LN_EPS = 1e-5
GN_EPS = RWKV_HEAD * 1e-5
SOFTMAX_C = (HEAD_DIM ** -0.5) * math.log2(math.e)
LANES = 128
SUBLANES = 8
CHUNK = 64
N_PAIR = D_MODEL // LANES
SCAN_BATCH = 2
ROW_TILE = 256
VMEM_LIMIT = 48 * 1024 * 1024

COL_Q = 0
COL_KV = 1024
COL_LORA = 1536
COL_GLU = 2048
COL_RKV = 4096
COL_GATE = 7168
N_IN_PAD = 10240


def _cparams(sem):
    return pltpu.CompilerParams(dimension_semantics=sem, vmem_limit_bytes=VMEM_LIMIT)


def _bdot(a, b):
    return jnp.dot(a.astype(BF16), b.astype(BF16), preferred_element_type=F32)


def _bdot_nt(a, b):
    return lax.dot_general(a.astype(BF16), b.astype(BF16), (((1,), (1,)), ((), ())),
                           preferred_element_type=F32)


def _bdot_tn(a, b):
    return lax.dot_general(a.astype(BF16), b.astype(BF16), (((0,), (0,)), ((), ())),
                           preferred_element_type=F32)


def _split2(x):
    hi = x.astype(BF16)
    lo = (x - hi.astype(F32)).astype(BF16)
    return hi, lo


def _dot3(a, b):
    ah, al = _split2(a)
    bh, bl = _split2(b)
    dot = functools.partial(jnp.dot, preferred_element_type=F32)
    return dot(ah, bh) + dot(al, bh) + dot(ah, bl)


def _sigmoid(x):
    return jax.nn.sigmoid(x)


def _head_sum(x, e_ref, et_ref):
    xh, xl = _split2(x)
    dot = functools.partial(jnp.dot, preferred_element_type=F32)
    s = dot(xh, e_ref[...]) + dot(xl, e_ref[...])
    sh, sl = _split2(s)
    return dot(sh, et_ref[...]) + dot(sl, et_ref[...])


def _mod_kernel(c_ref, w_ref, b_ref, o_ref):
    c = c_ref[...]
    o_ref[...] = _dot3(c * _sigmoid(c), w_ref[...]) + b_ref[...]


def _mod_call(cc, w_mod, b_mod):
    rows = cc.shape[0]
    n = w_mod.shape[1]
    tn = 1536
    return pl.pallas_call(
        _mod_kernel,
        grid=(n // tn,),
        in_specs=[pl.BlockSpec((rows, D_MODEL), lambda j: (0, 0)),
                  pl.BlockSpec((D_MODEL, tn), lambda j: (0, j)),
                  pl.BlockSpec((1, tn), lambda j: (0, j))],
        out_specs=pl.BlockSpec((rows, tn), lambda j: (0, j)),
        out_shape=jax.ShapeDtypeStruct((rows, n), F32),
        compiler_params=_cparams(("parallel",)),
        name="mod",
    )(cc, w_mod, b_mod.reshape(1, n))


def _mod_row(n_ctx_tiles, ctx_row):
    return lambda b, i, *_: (jnp.where(i < n_ctx_tiles, ctx_row, b), 0, 0)


def _nm_matmul_kernel(a_ref, mod_ref, g_ref, w_ref, o_ref, h_ref, *, shift_idx, seg, nseg):
    @pl.when(pl.program_id(1) == 0)
    def _():
        for s in range(nseg):
            rows = slice(s * seg, (s + 1) * seg)
            x = a_ref[rows, :]
            y = x * lax.rsqrt(jnp.mean(x * x, axis=-1, keepdims=True) + EPS) * g_ref[...]
            h = y * (1.0 + mod_ref[s, shift_idx + 1:shift_idx + 2, :]) + mod_ref[s, shift_idx:shift_idx + 1, :]
            h_ref[rows, :] = h.astype(BF16)

    o_ref[...] = jnp.dot(h_ref[...], w_ref[...], preferred_element_type=F32).astype(o_ref.dtype)


def _nm_matmul(a, modseg, g, w, *, shift_idx, tn, seg):
    bsz, tt, d = a.shape
    n = w.shape[1]
    rows = bsz * tt
    nseg = max(s for s in (4, 2, 1) if (rows // seg) % s == 0)
    tm = seg * nseg
    out = pl.pallas_call(
        functools.partial(_nm_matmul_kernel, shift_idx=shift_idx, seg=seg, nseg=nseg),
        grid=(rows // tm, n // tn),
        in_specs=[pl.BlockSpec((tm, d), lambda i, j: (i, 0)),
                  pl.BlockSpec((nseg, N_MOD, d), lambda i, j: (i, 0, 0)),
                  pl.BlockSpec((1, d), lambda i, j: (0, 0)),
                  pl.BlockSpec((d, tn), lambda i, j: (0, j))],
        out_specs=pl.BlockSpec((tm, tn), lambda i, j: (i, j)),
        out_shape=jax.ShapeDtypeStruct((rows, n), F32),
        scratch_shapes=[pltpu.VMEM((tm, d), BF16)],
        compiler_params=_cparams(("parallel", "arbitrary")),
        name="norm_mod_matmul",
    )(a.reshape(rows, d), modseg, g.reshape(1, d), w)
    return out.reshape(bsz, tt, n)


def _qk_prep_kernel(q_ref, kv_ref, cos_ref, sin_ref, qn_ref, kn_ref, qo_ref, ko_ref, vo_ref):
    cos = cos_ref[...]
    sin = sin_ref[...]
    lane = lax.broadcasted_iota(jnp.int32, cos.shape, 1)
    first = (lane & (N_FREQ)) == 0

    def norm_rope(x, g):
        y = x * lax.rsqrt(jnp.mean(x * x, axis=-1, keepdims=True) + EPS) * g
        partner = jnp.where(first, pltpu.roll(y, LANES - N_FREQ, 1), pltpu.roll(y, N_FREQ, 1))
        return y * cos + partner * sin

    for h in range(N_Q_HEADS):
        sl = slice(h * HEAD_DIM, (h + 1) * HEAD_DIM)
        qo_ref[:, sl] = (norm_rope(q_ref[:, sl], qn_ref[...]) * SOFTMAX_C).astype(BF16)
    for h in range(N_KV_HEADS):
        sl = slice(h * HEAD_DIM, (h + 1) * HEAD_DIM)
        ko_ref[:, sl] = norm_rope(kv_ref[:, sl], kn_ref[...]).astype(BF16)
    vo_ref[...] = kv_ref[:, N_KV_HEADS * HEAD_DIM:].astype(BF16)


def _qk_prep(proj, cos, sin, q_norm, k_norm, *, tm):
    bsz, tt, _ = proj.shape
    dq = N_Q_HEADS * HEAD_DIM
    dkv = N_KV_HEADS * HEAD_DIM
    return pl.pallas_call(
        _qk_prep_kernel,
        grid=(bsz, tt // tm),
        in_specs=[pl.BlockSpec((None, tm, dq), lambda b, i: (b, i, COL_Q // dq)),
                  pl.BlockSpec((None, tm, 2 * dkv), lambda b, i: (b, i, COL_KV // (2 * dkv))),
                  pl.BlockSpec((tm, HEAD_DIM), lambda b, i: (i, 0)),
                  pl.BlockSpec((tm, HEAD_DIM), lambda b, i: (i, 0)),
                  pl.BlockSpec((1, HEAD_DIM), lambda b, i: (0, 0)),
                  pl.BlockSpec((1, HEAD_DIM), lambda b, i: (0, 0))],
        out_specs=[pl.BlockSpec((None, tm, dq), lambda b, i: (b, i, 0)),
                   pl.BlockSpec((None, tm, dkv), lambda b, i: (b, i, 0)),
                   pl.BlockSpec((None, tm, dkv), lambda b, i: (b, i, 0))],
        out_shape=[jax.ShapeDtypeStruct((bsz, tt, dq), BF16),
                   jax.ShapeDtypeStruct((bsz, tt, dkv), BF16),
                   jax.ShapeDtypeStruct((bsz, tt, dkv), BF16)],
        compiler_params=_cparams(("parallel", "parallel")),
        name="qk_prep",
    )(proj, proj, cos, sin, q_norm.reshape(1, HEAD_DIM), k_norm.reshape(1, HEAD_DIM))


def _attn_kernel(q_ref, k_ref, v_ref, o_ref, *, n_ctx_tiles, ctx_len):
    tq = q_ref.shape[0]
    tt = k_ref.shape[0]
    half = (tt - ctx_len) // 2

    def attend(spans):
        q = jnp.concatenate([q_ref[:, g * HEAD_DIM:(g + 1) * HEAD_DIM] for g in range(GQA_GROUP)], axis=0)
        m = l = acc = None
        for start, size in spans:
            k = k_ref[start:start + size, :]
            v = v_ref[start:start + size, :]
            s = lax.dot_general(q, k, (((1,), (1,)), ((), ())), preferred_element_type=F32)
            mt = jnp.max(s, axis=-1, keepdims=True)
            if m is None:
                m = mt
                p = jnp.exp2(s - m)
                l = jnp.sum(p, axis=-1, keepdims=True)
                acc = jnp.dot(p.astype(BF16), v, preferred_element_type=F32)
            else:
                m_new = jnp.maximum(m, mt)
                alpha = jnp.exp2(m - m_new)
                p = jnp.exp2(s - m_new)
                l = alpha * l + jnp.sum(p, axis=-1, keepdims=True)
                acc = alpha * acc + jnp.dot(p.astype(BF16), v, preferred_element_type=F32)
                m = m_new
        o = acc / l
        for g in range(GQA_GROUP):
            o_ref[:, g * HEAD_DIM:(g + 1) * HEAD_DIM] = o[g * tq:(g + 1) * tq].astype(o_ref.dtype)

    i = pl.program_id(2)

    @pl.when(i < n_ctx_tiles)
    def _():
        attend([(0, ctx_len)])

    @pl.when(i >= n_ctx_tiles)
    def _():
        attend([(0, ctx_len), (ctx_len, half), (ctx_len + half, half)])


def _attention(q, k, v, *, ctx_len, tq):
    bsz, tt, dq = q.shape
    gw = GQA_GROUP * HEAD_DIM
    return pl.pallas_call(
        functools.partial(_attn_kernel, n_ctx_tiles=ctx_len // tq, ctx_len=ctx_len),
        grid=(bsz, N_KV_HEADS, tt // tq),
        in_specs=[pl.BlockSpec((None, tq, gw), lambda b, h, i: (b, i, h)),
                  pl.BlockSpec((None, tt, HEAD_DIM), lambda b, h, i: (b, 0, h)),
                  pl.BlockSpec((None, tt, HEAD_DIM), lambda b, h, i: (b, 0, h))],
        out_specs=pl.BlockSpec((None, tq, gw), lambda b, h, i: (b, i, h)),
        out_shape=jax.ShapeDtypeStruct((bsz, tt, dq), BF16),
        compiler_params=_cparams(("parallel", "parallel", "arbitrary")),
        name="attention",
    )(q, k, v)


def _halo_specs(tm, halo, tt, col_block, width):
    per = tm // halo
    last = tt // halo - 1
    prev = pl.BlockSpec((None, halo, width), lambda b, i: (b, jnp.maximum(i * per - 1, 0), col_block))
    nxt = pl.BlockSpec((None, halo, width), lambda b, i: (b, jnp.minimum((i + 1) * per, last), col_block))
    return prev, nxt


def _edge_flags(i, n_ctx_tiles, n_tiles):
    first = jnp.logical_or(i == 0, i == n_ctx_tiles)
    last = jnp.logical_or(i == n_ctx_tiles - 1, i == n_tiles - 1)
    return first, last


def _conformer_kernel(ac_ref, bc_ref, ap_ref, bp_ref, an_ref, bn_ref, w_ref, cb_ref, lg_ref, lb_ref,
                      o_ref, buf_ref, sh_ref, *, n_ctx_tiles, n_tiles, tm, rc):
    first, last = _edge_flags(pl.program_id(1), n_ctx_tiles, n_tiles)
    h = CONV_HALO
    buf_ref[0:h, :] = jnp.where(first, 0.0, ap_ref[...] * _sigmoid(bp_ref[...]))
    buf_ref[h:h + tm, :] = ac_ref[...] * _sigmoid(bc_ref[...])
    buf_ref[h + tm:h + tm + h, :] = jnp.where(last, 0.0, an_ref[...] * _sigmoid(bn_ref[...]))
    n_sh = sh_ref.shape[1]
    for s in range(1, SUBLANES):
        sh_ref[s - 1] = buf_ref[s:s + n_sh, :]
    off = h - CONV_K // 2
    for c in range(tm // rc):
        acc = jnp.zeros((rc, D_MODEL), F32)
        for j in range(CONV_K):
            s = (j + off) % SUBLANES
            r0 = c * rc + (j + off) - s
            tap = buf_ref[r0:r0 + rc, :] if s == 0 else sh_ref[s - 1, r0:r0 + rc, :]
            acc = acc + tap * jnp.concatenate([w_ref[j]] * (rc // SUBLANES), axis=0)
        y = acc + cb_ref[...]
        mu = jnp.mean(y, axis=-1, keepdims=True)
        dlt = y - mu
        var = jnp.mean(dlt * dlt, axis=-1, keepdims=True)
        z = dlt * lax.rsqrt(var + LN_EPS) * lg_ref[...] + lb_ref[...]
        o_ref[c * rc:(c + 1) * rc, :] = (z * _sigmoid(z)).astype(o_ref.dtype)


def _conformer(proj, conv_w, conv_b, ln_g, ln_b, *, ctx_len, tm):
    bsz, tt, _ = proj.shape
    d = D_MODEL
    ca = COL_GLU // d
    prev_a, next_a = _halo_specs(tm, CONV_HALO, tt, ca, d)
    prev_b, next_b = _halo_specs(tm, CONV_HALO, tt, ca + 1, d)
    vec = lambda: pl.BlockSpec((1, d), lambda b, i: (0, 0))
    return pl.pallas_call(
        functools.partial(_conformer_kernel, n_ctx_tiles=ctx_len // tm, n_tiles=tt // tm, tm=tm, rc=32),
        grid=(bsz, tt // tm),
        in_specs=[pl.BlockSpec((None, tm, d), lambda b, i: (b, i, ca)),
                  pl.BlockSpec((None, tm, d), lambda b, i: (b, i, ca + 1)),
                  prev_a, prev_b, next_a, next_b,
                  pl.BlockSpec((CONV_K, SUBLANES, d), lambda b, i: (0, 0, 0)),
                  vec(), vec(), vec()],
        out_specs=pl.BlockSpec((None, tm, d), lambda b, i: (b, i, 0)),
        out_shape=jax.ShapeDtypeStruct((bsz, tt, d), BF16),
        scratch_shapes=[pltpu.VMEM((tm + 2 * CONV_HALO, d), F32),
                        pltpu.VMEM((SUBLANES - 1, tm + 2 * CONV_HALO - SUBLANES, d), F32)],
        compiler_params=_cparams(("parallel", "parallel")),
        name="conformer",
    )(proj, proj, proj, proj, proj, proj, jnp.broadcast_to(conv_w[:, None, :], (CONV_K, SUBLANES, d)),
      conv_b.reshape(1, d), ln_g.reshape(1, d), ln_b.reshape(1, d))


def _rwkv_prep_kernel(rc_ref, kc_ref, vc_ref, rp_ref, kp_ref, vp_ref, rn_ref, kn_ref, vn_ref, lora_ref,
                      sw_ref, kk_ref, ka_ref, w0_ref, dup_ref, a0_ref, iup_ref, e_ref, et_ref,
                      r_ref, v_ref, kap_ref, lw_ref, kd_ref, bd_ref, buf_ref,
                      *, n_ctx_tiles, n_tiles, tm):
    first, last = _edge_flags(pl.program_id(1), n_ctx_tiles, n_tiles)
    h = SUBLANES
    d = D_MODEL

    def shift(cur_ref, prev_ref, next_ref, col):
        buf_ref[0:h, :] = jnp.where(first, 0.0, prev_ref[...])
        buf_ref[h:h + tm, :] = cur_ref[...]
        buf_ref[h + tm:h + tm + h, :] = jnp.where(last, 0.0, next_ref[...])
        sl = slice(col * d, (col + 1) * d)
        return (buf_ref[h - 1:h - 1 + tm, :] * sw_ref[0:1, sl] + buf_ref[h:h + tm, :] * sw_ref[1:2, sl]
                + buf_ref[h + 1:h + 1 + tm, :] * sw_ref[2:3, sl])

    r = shift(rc_ref, rp_ref, rn_ref, 0)
    k = shift(kc_ref, kp_ref, kn_ref, 1)
    v = shift(vc_ref, vp_ref, vn_ref, 2)
    r_ref[...] = r
    v_ref[...] = v
    kk = k * kk_ref[...]
    ss = _head_sum(kk * kk, e_ref, et_ref)
    kap = kk * lax.rsqrt(jnp.maximum(ss, 1e-12))
    kap_ref[...] = kap
    tw = jnp.tanh(lora_ref[:, 0:LANES])
    la = lora_ref[:, LANES:2 * LANES]
    for dr in range(2):
        z = w0_ref[dr:dr + 1, :] + _dot3(tw, dup_ref[dr])
        lw_ref[dr] = -DECAY_SCALE * _sigmoid(z)
        a = _sigmoid(a0_ref[dr:dr + 1, :] + _dot3(la, iup_ref[dr]))
        kd_ref[dr] = k * (1.0 + (a - 1.0) * ka_ref[...])
        bd_ref[dr] = a * kap


def _rwkv_prep(proj, shift_w, k_k, k_a, decay_w0, decay_up_pad, iclr_a0, iclr_up_pad, e, et, *, ctx_len, tm):
    bsz, tt, _ = proj.shape
    d = D_MODEL
    c0 = COL_RKV // d
    cur = lambda c: pl.BlockSpec((None, tm, d), lambda b, i: (b, i, c))
    halos = [_halo_specs(tm, SUBLANES, tt, c0 + c, d) for c in range(3)]
    full = lambda shape: pl.BlockSpec(shape, lambda b, i: (0,) * len(shape))
    out1 = pl.BlockSpec((None, tm, d), lambda b, i: (b, i, 0))
    out2 = pl.BlockSpec((2, None, tm, d), lambda b, i: (0, b, i, 0))
    s1 = jax.ShapeDtypeStruct((bsz, tt, d), F32)
    s2 = jax.ShapeDtypeStruct((2, bsz, tt, d), F32)
    return pl.pallas_call(
        functools.partial(_rwkv_prep_kernel, n_ctx_tiles=ctx_len // tm, n_tiles=tt // tm, tm=tm),
        grid=(bsz, tt // tm),
        in_specs=[cur(c0), cur(c0 + 1), cur(c0 + 2),
                  halos[0][0], halos[1][0], halos[2][0], halos[0][1], halos[1][1], halos[2][1],
                  pl.BlockSpec((None, tm, 512), lambda b, i: (b, i, COL_LORA // 512)),
                  full((3, 3 * d)), full((1, d)), full((1, d)), full((2, d)), full((2, LANES, d)),
                  full((2, d)), full((2, LANES, d)), full((d, LANES)), full((LANES, d))],
        out_specs=[out1, out1, out1, out2, out2, out2],
        out_shape=[s1, s1, s1, s2, s2, s2],
        scratch_shapes=[pltpu.VMEM((tm + 2 * SUBLANES, d), F32)],
        compiler_params=_cparams(("parallel", "parallel")),
        name="rwkv_prep",
    )(proj, proj, proj, proj, proj, proj, proj, proj, proj, proj,
      shift_w, k_k.reshape(1, d), k_a.reshape(1, d), decay_w0, decay_up_pad, iclr_a0, iclr_up_pad, e, et)


def _scan_kernel(r_ref, v_ref, kap_ref, lw_ref, k_ref, b_ref, y_ref, h_ref):
    c = CHUNK
    c2 = 2 * c
    sgn = 1 - 2 * pl.program_id(1)

    @pl.when(pl.program_id(2) == 0)
    def _():
        h_ref[...] = jnp.zeros_like(h_ref)

    row = lax.broadcasted_iota(jnp.int32, (c, c), 0)
    col = lax.broadcasted_iota(jnp.int32, (c, c), 1)
    incl = jnp.where((col - row) * sgn <= 0, 1.0, 0.0).astype(BF16)
    dot = functools.partial(jnp.dot, preferred_element_type=F32)
    nb = lw_ref.shape[0]
    cum = []
    for bi in range(nb):
        lw = lw_ref[bi]
        hi = lw.astype(BF16)
        rem = lw - hi.astype(F32)
        mid = rem.astype(BF16)
        lo = (rem - mid.astype(F32)).astype(BF16)
        cum.append(dot(incl, hi) + dot(incl, mid) + dot(incl, lo))

    prow = lax.broadcasted_iota(jnp.int32, (c, c2), 0)
    pcol = lax.broadcasted_iota(jnp.int32, (c, c2), 1)
    order = ((pcol & (c - 1)) - prow) * sgn
    strict = order < 0
    upto = order <= 0
    eye = order == 0
    head0 = pcol < RWKV_HEAD

    def stack(x):
        return jnp.concatenate([jnp.where(head0, x, 0.0), jnp.where(head0, 0.0, x)], axis=0)

    def pack(x):
        return jnp.where(head0, x[0:c], x[c:c2])

    units = [(bi, slice(p * LANES, (p + 1) * LANES)) for bi in range(nb) for p in range(N_PAIR)]
    pairs = range(len(units))
    kt, rt, vs, kend, bend, ptot, a = [], [], [], [], [], [], []
    for bi, sl in units:
        cum_p = cum[bi][:, sl]
        lw_p = lw_ref[bi, :, sl]
        tot_p = jnp.sum(lw_p, axis=0, keepdims=True)
        p_inv = jnp.exp(-cum_p)
        p_end = jnp.exp(tot_p - cum_p)
        k = k_ref[bi, :, sl]
        b = b_ref[bi, :, sl]
        kt.append(kap_ref[bi, :, sl] * jnp.exp(cum_p - lw_p))
        rt.append(r_ref[bi, :, sl] * jnp.exp(cum_p))
        vs.append(v_ref[bi, :, sl])
        kend.append(k * p_end)
        bend.append(b * p_end)
        ptot.append(jnp.exp(tot_p))
        a.append(_bdot_nt(jnp.concatenate([kt[-1], rt[-1]], axis=0),
                          jnp.concatenate([stack(b * p_inv), stack(k * p_inv)], axis=0)))
    a_ab = [jnp.where(strict, a[p][0:c, 0:c2], 0.0) for p in pairs]
    a_ak = [jnp.where(strict, a[p][0:c, c2:2 * c2], 0.0) for p in pairs]
    a_rb = [jnp.where(upto, a[p][c:c2, 0:c2], 0.0) for p in pairs]
    a_rk = [jnp.where(upto, a[p][c:c2, c2:2 * c2], 0.0) for p in pairs]

    t = [jnp.where(eye, 1.0, 0.0) - a_ab[p] for p in pairs]
    x = [_bdot(a_ab[p], stack(a_ab[p])) for p in pairs]
    n = 2
    while 2 * n < c:
        xt = [_bdot(jnp.concatenate([x[p], t[p]], axis=0), stack(x[p])) for p in pairs]
        t = [t[p] + xt[p][c:c2] for p in pairs]
        x = [xt[p][0:c] for p in pairs]
        n *= 2
    t = [t[p] + _bdot(t[p], stack(x[p])) for p in pairs]

    asv = [_bdot(jnp.concatenate([a_ak[p], a_rk[p]], axis=0), stack(vs[p])) for p in pairs]
    wu = [_bdot(t[p], jnp.concatenate([stack(kt[p]), stack(asv[p][0:c])], axis=1)) for p in pairs]
    w = [wu[p][:, 0:c2] for p in pairs]
    u0 = [wu[p][:, c2:2 * c2] for p in pairs]
    arb = [_bdot(a_rb[p], jnp.concatenate([stack(w[p]), stack(u0[p])], axis=1)) for p in pairs]
    y0 = [asv[p][c:c2] - arb[p][:, c2:2 * c2] for p in pairs]
    y1 = [rt[p] - arb[p][:, 0:c2] for p in pairs]
    m = [jnp.where(eye, ptot[p], 0.0) - pack(_bdot_tn(bend[p], w[p])) for p in pairs]
    nn = [pack(_bdot_tn(jnp.concatenate([kend[p], -bend[p]], axis=0), jnp.concatenate([vs[p], u0[p]], axis=0)))
          for p in pairs]
    for p in pairs:
        h0 = h_ref[p]
        hh = h0.astype(BF16)
        hl = h0 - hh.astype(F32)
        mh, ml = _split2(m[p])
        top = _bdot(jnp.concatenate([mh, ml, y1[p].astype(BF16)], axis=0), stack(hh.astype(F32)))
        y_ref[units[p][0], :, units[p][1]] = top[c2:c2 + c] + y0[p]
        h_ref[p] = top[0:c] + top[c:c2] + _bdot(mh, stack(hl)) + nn[p]


def _rwkv_scan(r, v, kap, lw, kd, bd, *, ctx_len):
    bsz, tt, d = r.shape
    nc = tt // CHUNK
    ncc = ctx_len // CHUNK

    def chunk(dr, s):
        return jnp.where(dr == 0, s, jnp.where(s < ncc, ncc - 1 - s, nc + ncc - 1 - s))

    ub = SCAN_BATCH if bsz % SCAN_BATCH == 0 else 1
    shared = pl.BlockSpec((ub, CHUNK, d), lambda b, dr, s: (b, chunk(dr, s), 0))
    per_dir = pl.BlockSpec((None, ub, CHUNK, d), lambda b, dr, s: (dr, b, chunk(dr, s), 0))
    return pl.pallas_call(
        _scan_kernel,
        grid=(bsz // ub, 2, nc),
        in_specs=[shared, shared, shared, per_dir, per_dir, per_dir],
        out_specs=per_dir,
        out_shape=jax.ShapeDtypeStruct((2, bsz, tt, d), F32),
        scratch_shapes=[pltpu.VMEM((ub * N_PAIR, CHUNK, LANES), F32)],
        compiler_params=_cparams(("parallel", "parallel", "arbitrary")),
        name="rwkv_scan",
    )(r, v, kap, lw, kd, bd)


def _rwkv_out_kernel(y_ref, r_ref, v_ref, kd_ref, lora_ref, gg_ref, gb_ref, rk_ref, gup_ref, e_ref, et_ref, o_ref):
    inv = 1.0 / RWKV_HEAD
    y = y_ref[0] + y_ref[1]
    mu = _head_sum(y, e_ref, et_ref) * inv
    dlt = y - mu
    var = _head_sum(dlt * dlt, e_ref, et_ref) * inv
    yn = dlt * lax.rsqrt(var + GN_EPS) * gg_ref[...] + gb_ref[...]
    bonus = _head_sum(r_ref[...] * (kd_ref[0] + kd_ref[1]) * rk_ref[...], e_ref, et_ref) * v_ref[...]
    g = _bdot(_sigmoid(lora_ref[:, 2 * LANES:3 * LANES]), gup_ref[...])
    o_ref[...] = ((yn + bonus) * g).astype(o_ref.dtype)


def _rwkv_out(y, r, v, kd, proj, gn_g, gn_b, r_k, gate_up, e, et, *, tm):
    bsz, tt, d = r.shape
    one = pl.BlockSpec((None, tm, d), lambda b, i: (b, i, 0))
    two = pl.BlockSpec((2, None, tm, d), lambda b, i: (0, b, i, 0))
    full = lambda shape: pl.BlockSpec(shape, lambda b, i: (0,) * len(shape))
    return pl.pallas_call(
        _rwkv_out_kernel,
        grid=(bsz, tt // tm),
        in_specs=[two, one, one, two,
                  pl.BlockSpec((None, tm, 512), lambda b, i: (b, i, COL_LORA // 512)),
                  full((1, d)), full((1, d)), full((1, d)), full((LANES, d)), full((d, LANES)), full((LANES, d))],
        out_specs=one,
        out_shape=jax.ShapeDtypeStruct((bsz, tt, d), BF16),
        compiler_params=_cparams(("parallel", "parallel")),
        name="rwkv_out",
    )(y, r, v, kd, proj, gn_g.reshape(1, d), gn_b.reshape(1, d), r_k.reshape(1, d), gate_up, e, et)


def _merge_kernel(a_ref, att_ref, conv_ref, rw_ref, ga_ref, gc_ref, gr_ref, mod_ref, g_ref,
                  wa_ref, wc_ref, wr_ref, wo_ref, o_ref):
    dot = functools.partial(jnp.dot, preferred_element_type=F32)
    m = (_sigmoid(ga_ref[...]) * dot(att_ref[...], wa_ref[...])
         + _sigmoid(gc_ref[...]) * dot(conv_ref[...], wc_ref[...])
         + _sigmoid(gr_ref[...]) * dot(rw_ref[...], wr_ref[...]))
    z = dot(m.astype(BF16), wo_ref[...])
    zn = z * lax.rsqrt(jnp.mean(z * z, axis=-1, keepdims=True) + EPS) * g_ref[...]
    o_ref[...] = a_ref[...] + mod_ref[2:3, :] * zn


def _merge(a, att, conv, rw, proj, mod3, g, wa, wc, wr, wo, *, n_ctx_tiles, ctx_row, tm):
    bsz, tt, d = a.shape
    cg = COL_GATE // d
    one = pl.BlockSpec((None, tm, d), lambda b, i: (b, i, 0))
    gate = lambda c: pl.BlockSpec((None, tm, d), lambda b, i: (b, i, cg + c))
    wspec = pl.BlockSpec((d, d), lambda b, i: (0, 0))
    return pl.pallas_call(
        _merge_kernel,
        grid=(bsz, tt // tm),
        in_specs=[one, one, one, one, gate(0), gate(1), gate(2),
                  pl.BlockSpec((None, N_MOD, d), _mod_row(n_ctx_tiles, ctx_row)),
                  pl.BlockSpec((1, d), lambda b, i: (0, 0)),
                  wspec, wspec, wspec, wspec],
        out_specs=one,
        out_shape=jax.ShapeDtypeStruct((bsz, tt, d), F32),
        compiler_params=_cparams(("parallel", "parallel")),
        name="merge",
    )(a, att, conv, rw, proj, proj, proj, mod3, g.reshape(1, d), wa, wc, wr, wo)


def _ffn_tail_kernel(a_ref, zc_ref, zp_ref, zn_ref, cw_ref, mod_ref, g_ref, wd_ref, o_ref, buf_ref,
                     *, n_ctx_tiles, n_tiles, tm):
    first, last = _edge_flags(pl.program_id(1), n_ctx_tiles, n_tiles)
    h = SUBLANES
    buf_ref[0:h, :] = jnp.where(first, 0.0, zp_ref[...])
    buf_ref[h:h + tm, :] = zc_ref[...]
    buf_ref[h + tm:h + tm + h, :] = jnp.where(last, 0.0, zn_ref[...])

    def conv(sl):
        return (buf_ref[h - 1:h - 1 + tm, sl] * cw_ref[0:1, sl] + buf_ref[h:h + tm, sl] * cw_ref[1:2, sl]
                + buf_ref[h + 1:h + 1 + tm, sl] * cw_ref[2:3, sl])

    gate = conv(slice(0, D_FF))
    val = conv(slice(D_FF, 2 * D_FF))
    u = (gate * _sigmoid(gate) * val).astype(BF16)
    z = jnp.dot(u, wd_ref[...], preferred_element_type=F32)
    zn = z * lax.rsqrt(jnp.mean(z * z, axis=-1, keepdims=True) + EPS) * g_ref[...]
    o_ref[...] = a_ref[...] + mod_ref[5:6, :] * zn


def _ffn_tail(a, z, conv_w, mod3, g, wd, *, n_ctx_tiles, ctx_row, tm):
    bsz, tt, d = a.shape
    f2 = 2 * D_FF
    prev, nxt = _halo_specs(tm, SUBLANES, tt, 0, f2)
    one = pl.BlockSpec((None, tm, d), lambda b, i: (b, i, 0))
    return pl.pallas_call(
        functools.partial(_ffn_tail_kernel, n_ctx_tiles=n_ctx_tiles, n_tiles=tt // tm, tm=tm),
        grid=(bsz, tt // tm),
        in_specs=[one, pl.BlockSpec((None, tm, f2), lambda b, i: (b, i, 0)), prev, nxt,
                  pl.BlockSpec((3, f2), lambda b, i: (0, 0)),
                  pl.BlockSpec((None, N_MOD, d), _mod_row(n_ctx_tiles, ctx_row)),
                  pl.BlockSpec((1, d), lambda b, i: (0, 0)),
                  pl.BlockSpec((D_FF, d), lambda b, i: (0, 0))],
        out_specs=one,
        out_shape=jax.ShapeDtypeStruct((bsz, tt, d), F32),
        scratch_shapes=[pltpu.VMEM((tm + 2 * SUBLANES, f2), F32)],
        compiler_params=_cparams(("parallel", "parallel")),
        name="ffn_tail",
    )(a, z, z, z, conv_w, mod3, g.reshape(1, d), wd)


def _rope_tables(ctx_len, seq):
    rows = seq // GRID_W
    row = jnp.repeat(jnp.arange(rows, dtype=F32), GRID_W)
    col = jnp.tile(jnp.arange(GRID_W, dtype=F32), rows)
    inv_freq = ROPE_THETA ** (-jnp.arange(N_FREQ, dtype=F32) / N_FREQ)
    ang_r = row[:, None] * inv_freq
    ang_c = col[:, None] * inv_freq
    cos = jnp.concatenate([jnp.cos(ang_r), jnp.cos(ang_r), jnp.cos(ang_c), jnp.cos(ang_c)], axis=-1)
    sin = jnp.concatenate([-jnp.sin(ang_r), jnp.sin(ang_r), -jnp.sin(ang_c), jnp.sin(ang_c)], axis=-1)
    cos = jnp.concatenate([jnp.ones((ctx_len, HEAD_DIM), F32), cos], axis=0)
    sin = jnp.concatenate([jnp.zeros((ctx_len, HEAD_DIM), F32), sin], axis=0)
    return cos, sin


def _reorder_w_in(w):
    qkv = w[:, 0:1536]
    glu = w[:, 1536:3584]
    rkv = w[:, 3584:6656]
    lora = w[:, 6656:7040]
    gates = w[:, 7040:10112]
    pad = jnp.zeros((w.shape[0], LANES), w.dtype)
    return jnp.concatenate([qkv, lora, pad, glu, rkv, gates], axis=1).astype(BF16)


def _pad_lora_up(up):
    z = jnp.zeros_like(up[0])
    return jnp.stack([jnp.concatenate([up[0], z], axis=0), jnp.concatenate([z, up[1]], axis=0)])


def kernel(x, c, ctx, c_ctx, w_mod, b_mod, g_pre_mix, g_post_mix, g_pre_ffn, g_post_ffn, w_in, q_norm, k_norm,
           w_attn_o, conv_w, conv_b, conv_ln_g, conv_ln_b, w_conv_o, shift_w, decay_w0, decay_up, iclr_a0,
           iclr_up, gate_up, k_k, k_a, r_k, wkv_gn_g, wkv_gn_b, w_rwkv_o, w_out, w_ffn_up, ffn_conv_w, w_ffn_down):
    bsz, seq, d = x.shape
    ctx_len = ctx.shape[1]
    depth = w_mod.shape[0]
    tm = min(ROW_TILE, ctx_len)
    assert d == D_MODEL and ctx_len % tm == 0 and seq % tm == 0 and ctx_len % CHUNK == 0 and seq % CHUNK == 0
    n_ctx_tiles = ctx_len // tm

    a = jnp.concatenate([ctx, x], axis=1)
    mod_rows = -(-(bsz + 1) // SUBLANES) * SUBLANES
    cc = jnp.zeros((mod_rows, d), F32).at[:bsz].set(c).at[bsz].set(c_ctx)
    cos, sin = _rope_tables(ctx_len, seq)
    head = jnp.arange(d, dtype=jnp.int32) // RWKV_HEAD
    e = (head[:, None] == jnp.arange(LANES, dtype=jnp.int32)[None, :]).astype(BF16)
    et = e.T
    segs = (ctx_len + seq) // tm
    seg_row = jnp.array([bsz if s % segs < n_ctx_tiles else s // segs for s in range(bsz * segs)], jnp.int32)

    for l in range(depth):
        mod3 = _mod_call(cc, w_mod[l], b_mod[l]).reshape(mod_rows, N_MOD, d)
        modseg = jnp.take(mod3, seg_row, axis=0)
        kw = dict(n_ctx_tiles=n_ctx_tiles, ctx_row=bsz, tm=tm)
        proj = _nm_matmul(a, modseg, g_pre_mix[l], _reorder_w_in(w_in[l]), shift_idx=0, tn=2048, seg=tm)
        q, k, v = _qk_prep(proj, cos, sin, q_norm[l], k_norm[l], tm=tm)
        att = _attention(q, k, v, ctx_len=ctx_len, tq=tm)
        conv = _conformer(proj, conv_w[l], conv_b[l], conv_ln_g[l], conv_ln_b[l], ctx_len=ctx_len, tm=tm)
        r, vv, kap, lw, kd, bd = _rwkv_prep(proj, shift_w[l], k_k[l], k_a[l], decay_w0[l], _pad_lora_up(decay_up[l]),
                                            iclr_a0[l], _pad_lora_up(iclr_up[l]), e, et, ctx_len=ctx_len, tm=tm)
        y = _rwkv_scan(r, vv, kap, lw, kd, bd, ctx_len=ctx_len)
        rw = _rwkv_out(y, r, vv, kd, proj, wkv_gn_g[l], wkv_gn_b[l], r_k[l], gate_up[l].astype(BF16), e, et, tm=tm)
        a = _merge(a, att, conv, rw, proj, mod3, g_post_mix[l], w_attn_o[l].astype(BF16), w_conv_o[l].astype(BF16),
                   w_rwkv_o[l].astype(BF16), w_out[l].astype(BF16), **kw)
        z = _nm_matmul(a, modseg, g_pre_ffn[l], w_ffn_up[l].astype(BF16), shift_idx=3, tn=D_FF // 2, seg=tm)
        a = _ffn_tail(a, z, ffn_conv_w[l], mod3, g_post_ffn[l], w_ffn_down[l].astype(BF16), **kw)
    return a[:, ctx_len:, :]
```

```python
import functools
import math

import jax
import jax.numpy as jnp
from jax import lax
from jax.experimental import pallas as pl
from jax.experimental.pallas import tpu as pltpu

F32 = jnp.float32
BF16 = jnp.bfloat16

D_MODEL = 1024
GRID_W = 64
N_Q_HEADS = 8
N_KV_HEADS = 2
GQA_GROUP = N_Q_HEADS // N_KV_HEADS
HEAD_DIM = 128
N_FREQ = HEAD_DIM // 4
ROPE_THETA = 10000.0
CONV_K = 31
CONV_HALO = 16
RWKV_HEAD = 64
RWKV_HEADS = D_MODEL // RWKV_HEAD
DECAY_SCALE = math.exp(-0.5)
D_FF = 2816
N_MOD = 6
EPS = 1e-6
LN_EPS = 1e-5
GN_EPS = RWKV_HEAD * 1e-5
SOFTMAX_C = (HEAD_DIM ** -0.5) * math.log2(math.e)
LANES = 128
SUBLANES = 8
CHUNK = 64
N_PAIR = D_MODEL // LANES
SCAN_BATCH = 2
ROW_TILE = 256
VMEM_LIMIT = 48 * 1024 * 1024

COL_Q = 0
COL_KV = 1024
COL_LORA = 1536
COL_GLU = 2048
COL_RKV = 4096
COL_GATE = 7168
N_IN_PAD = 10240


def _cparams(sem):
    return pltpu.CompilerParams(dimension_semantics=sem, vmem_limit_bytes=VMEM_LIMIT)


def _bdot(a, b):
    return jnp.dot(a.astype(BF16), b.astype(BF16), preferred_element_type=F32)


def _bdot_nt(a, b):
    return lax.dot_general(a.astype(BF16), b.astype(BF16), (((1,), (1,)), ((), ())),
                           preferred_element_type=F32)


def _bdot_tn(a, b):
    return lax.dot_general(a.astype(BF16), b.astype(BF16), (((0,), (0,)), ((), ())),
                           preferred_element_type=F32)


def _split2(x):
    hi = x.astype(BF16)
    lo = (x - hi.astype(F32)).astype(BF16)
    return hi, lo


def _dot3(a, b):
    ah, al = _split2(a)
    bh, bl = _split2(b)
    dot = functools.partial(jnp.dot, preferred_element_type=F32)
    return dot(ah, bh) + dot(al, bh) + dot(ah, bl)


def _sigmoid(x):
    return jax.nn.sigmoid(x)


def _head_sum(x, e_ref, et_ref):
    xh, xl = _split2(x)
    dot = functools.partial(jnp.dot, preferred_element_type=F32)
    s = dot(xh, e_ref[...]) + dot(xl, e_ref[...])
    sh, sl = _split2(s)
    return dot(sh, et_ref[...]) + dot(sl, et_ref[...])


def _mod_kernel(c_ref, w_ref, b_ref, o_ref):
    c = c_ref[...]
    o_ref[...] = _dot3(c * _sigmoid(c), w_ref[...]) + b_ref[...]


def _mod_call(cc, w_mod, b_mod):
    rows = cc.shape[0]
    n = w_mod.shape[1]
    tn = 1536
    return pl.pallas_call(
        _mod_kernel,
        grid=(n // tn,),
        in_specs=[pl.BlockSpec((rows, D_MODEL), lambda j: (0, 0)),
                  pl.BlockSpec((D_MODEL, tn), lambda j: (0, j)),
                  pl.BlockSpec((1, tn), lambda j: (0, j))],
        out_specs=pl.BlockSpec((rows, tn), lambda j: (0, j)),
        out_shape=jax.ShapeDtypeStruct((rows, n), F32),
        compiler_params=_cparams(("parallel",)),
        name="mod",
    )(cc, w_mod, b_mod.reshape(1, n))


def _mod_row(n_ctx_tiles, ctx_row):
    return lambda b, i, *_: (jnp.where(i < n_ctx_tiles, ctx_row, b), 0, 0)


def _nm_matmul_kernel(a_ref, mod_ref, g_ref, w_ref, o_ref, h_ref, *, shift_idx, seg, nseg):
    @pl.when(pl.program_id(1) == 0)
    def _():
        for s in range(nseg):
            rows = slice(s * seg, (s + 1) * seg)
            x = a_ref[rows, :]
            y = x * lax.rsqrt(jnp.mean(x * x, axis=-1, keepdims=True) + EPS) * g_ref[...]
            h = y * (1.0 + mod_ref[s, shift_idx + 1:shift_idx + 2, :]) + mod_ref[s, shift_idx:shift_idx + 1, :]
            h_ref[rows, :] = h.astype(BF16)

    o_ref[...] = jnp.dot(h_ref[...], w_ref[...], preferred_element_type=F32).astype(o_ref.dtype)


def _nm_matmul(a, modseg, g, w, *, shift_idx, tn, seg):
    bsz, tt, d = a.shape
    n = w.shape[1]
    rows = bsz * tt
    nseg = max(s for s in (4, 2, 1) if (rows // seg) % s == 0)
    tm = seg * nseg
    out = pl.pallas_call(
        functools.partial(_nm_matmul_kernel, shift_idx=shift_idx, seg=seg, nseg=nseg),
        grid=(rows // tm, n // tn),
        in_specs=[pl.BlockSpec((tm, d), lambda i, j: (i, 0)),
                  pl.BlockSpec((nseg, N_MOD, d), lambda i, j: (i, 0, 0)),
                  pl.BlockSpec((1, d), lambda i, j: (0, 0)),
                  pl.BlockSpec((d, tn), lambda i, j: (0, j))],
        out_specs=pl.BlockSpec((tm, tn), lambda i, j: (i, j)),
        out_shape=jax.ShapeDtypeStruct((rows, n), F32),
        scratch_shapes=[pltpu.VMEM((tm, d), BF16)],
        compiler_params=_cparams(("parallel", "arbitrary")),
        name="norm_mod_matmul",
    )(a.reshape(rows, d), modseg, g.reshape(1, d), w)
    return out.reshape(bsz, tt, n)


def _qk_prep_kernel(q_ref, kv_ref, cos_ref, sin_ref, qn_ref, kn_ref, qo_ref, ko_ref, vo_ref):
    cos = cos_ref[...]
    sin = sin_ref[...]
    lane = lax.broadcasted_iota(jnp.int32, cos.shape, 1)
    first = (lane & (N_FREQ)) == 0

    def norm_rope(x, g):
        y = x * lax.rsqrt(jnp.mean(x * x, axis=-1, keepdims=True) + EPS) * g
        partner = jnp.where(first, pltpu.roll(y, LANES - N_FREQ, 1), pltpu.roll(y, N_FREQ, 1))
        return y * cos + partner * sin

    for h in range(N_Q_HEADS):
        sl = slice(h * HEAD_DIM, (h + 1) * HEAD_DIM)
        qo_ref[:, sl] = (norm_rope(q_ref[:, sl], qn_ref[...]) * SOFTMAX_C).astype(BF16)
    for h in range(N_KV_HEADS):
        sl = slice(h * HEAD_DIM, (h + 1) * HEAD_DIM)
        ko_ref[:, sl] = norm_rope(kv_ref[:, sl], kn_ref[...]).astype(BF16)
    vo_ref[...] = kv_ref[:, N_KV_HEADS * HEAD_DIM:].astype(BF16)


def _qk_prep(proj, cos, sin, q_norm, k_norm, *, tm):
    bsz, tt, _ = proj.shape
    dq = N_Q_HEADS * HEAD_DIM
    dkv = N_KV_HEADS * HEAD_DIM
    return pl.pallas_call(
        _qk_prep_kernel,
        grid=(bsz, tt // tm),
        in_specs=[pl.BlockSpec((None, tm, dq), lambda b, i: (b, i, COL_Q // dq)),
                  pl.BlockSpec((None, tm, 2 * dkv), lambda b, i: (b, i, COL_KV // (2 * dkv))),
                  pl.BlockSpec((tm, HEAD_DIM), lambda b, i: (i, 0)),
                  pl.BlockSpec((tm, HEAD_DIM), lambda b, i: (i, 0)),
                  pl.BlockSpec((1, HEAD_DIM), lambda b, i: (0, 0)),
                  pl.BlockSpec((1, HEAD_DIM), lambda b, i: (0, 0))],
        out_specs=[pl.BlockSpec((None, tm, dq), lambda b, i: (b, i, 0)),
                   pl.BlockSpec((None, tm, dkv), lambda b, i: (b, i, 0)),
                   pl.BlockSpec((None, tm, dkv), lambda b, i: (b, i, 0))],
        out_shape=[jax.ShapeDtypeStruct((bsz, tt, dq), BF16),
                   jax.ShapeDtypeStruct((bsz, tt, dkv), BF16),
                   jax.ShapeDtypeStruct((bsz, tt, dkv), BF16)],
        compiler_params=_cparams(("parallel", "parallel")),
        name="qk_prep",
    )(proj, proj, cos, sin, q_norm.reshape(1, HEAD_DIM), k_norm.reshape(1, HEAD_DIM))


def _attn_kernel(q_ref, k_ref, v_ref, o_ref, *, n_ctx_tiles, ctx_len, skip_ctx):
    tq = q_ref.shape[0]
    tt = k_ref.shape[0]
    half = (tt - ctx_len) // 2

    def attend(spans):
        q = jnp.concatenate([q_ref[:, g * HEAD_DIM:(g + 1) * HEAD_DIM] for g in range(GQA_GROUP)], axis=0)
        m = l = acc = None
        for start, size in spans:
            k = k_ref[start:start + size, :]
            v = v_ref[start:start + size, :]
            s = lax.dot_general(q, k, (((1,), (1,)), ((), ())), preferred_element_type=F32)
            mt = jnp.max(s, axis=-1, keepdims=True)
            if m is None:
                m = mt
                p = jnp.exp2(s - m)
                l = jnp.sum(p, axis=-1, keepdims=True)
                acc = jnp.dot(p.astype(BF16), v, preferred_element_type=F32)
            else:
                m_new = jnp.maximum(m, mt)
                alpha = jnp.exp2(m - m_new)
                p = jnp.exp2(s - m_new)
                l = alpha * l + jnp.sum(p, axis=-1, keepdims=True)
                acc = alpha * acc + jnp.dot(p.astype(BF16), v, preferred_element_type=F32)
                m = m_new
        o = acc / l
        for g in range(GQA_GROUP):
            o_ref[:, g * HEAD_DIM:(g + 1) * HEAD_DIM] = o[g * tq:(g + 1) * tq].astype(o_ref.dtype)

    x_spans = [(0, ctx_len), (ctx_len, half), (ctx_len + half, half)]
    if skip_ctx:
        attend(x_spans)
        return
    i = pl.program_id(2)

    @pl.when(i < n_ctx_tiles)
    def _():
        attend([(0, ctx_len)])

    @pl.when(i >= n_ctx_tiles)
    def _():
        attend(x_spans)


def _attention(q, k, v, *, ctx_len, tq, skip_ctx):
    bsz, tt, dq = q.shape
    gw = GQA_GROUP * HEAD_DIM
    off = ctx_len // tq if skip_ctx else 0
    return pl.pallas_call(
        functools.partial(_attn_kernel, n_ctx_tiles=ctx_len // tq, ctx_len=ctx_len, skip_ctx=skip_ctx),
        grid=(bsz, N_KV_HEADS, tt // tq - off),
        in_specs=[pl.BlockSpec((None, tq, gw), lambda b, h, i: (b, i + off, h)),
                  pl.BlockSpec((None, tt, HEAD_DIM), lambda b, h, i: (b, 0, h)),
                  pl.BlockSpec((None, tt, HEAD_DIM), lambda b, h, i: (b, 0, h))],
        out_specs=pl.BlockSpec((None, tq, gw), lambda b, h, i: (b, i, h)),
        out_shape=jax.ShapeDtypeStruct((bsz, tt - off * tq, dq), BF16),
        compiler_params=_cparams(("parallel", "parallel", "arbitrary")),
        name="attention",
    )(q, k, v)


def _halo_specs(tm, halo, tt, col_block, width, off=0):
    per = tm // halo
    last = tt // halo - 1
    prev = pl.BlockSpec((None, halo, width), lambda b, i: (b, jnp.maximum((i + off) * per - 1, 0), col_block))
    nxt = pl.BlockSpec((None, halo, width), lambda b, i: (b, jnp.minimum((i + off + 1) * per, last), col_block))
    return prev, nxt


def _edge_flags(i, n_ctx_tiles, n_tiles):
    first = jnp.logical_or(i == 0, i == n_ctx_tiles)
    last = jnp.logical_or(i == n_ctx_tiles - 1, i == n_tiles - 1)
    return first, last


def _conformer_kernel(ac_ref, bc_ref, ap_ref, bp_ref, an_ref, bn_ref, w_ref, cb_ref, lg_ref, lb_ref,
                      o_ref, buf_ref, sh_ref, *, n_ctx_tiles, n_tiles, tm, rc, tile0):
    first, last = _edge_flags(pl.program_id(1) + tile0, n_ctx_tiles, n_tiles)
    h = CONV_HALO
    buf_ref[0:h, :] = jnp.where(first, 0.0, ap_ref[...] * _sigmoid(bp_ref[...]))
    buf_ref[h:h + tm, :] = ac_ref[...] * _sigmoid(bc_ref[...])
    buf_ref[h + tm:h + tm + h, :] = jnp.where(last, 0.0, an_ref[...] * _sigmoid(bn_ref[...]))
    n_sh = sh_ref.shape[1]
    for s in range(1, SUBLANES):
        sh_ref[s - 1] = buf_ref[s:s + n_sh, :]
    off = h - CONV_K // 2
    for c in range(tm // rc):
        acc = jnp.zeros((rc, D_MODEL), F32)
        for j in range(CONV_K):
            s = (j + off) % SUBLANES
            r0 = c * rc + (j + off) - s
            tap = buf_ref[r0:r0 + rc, :] if s == 0 else sh_ref[s - 1, r0:r0 + rc, :]
            acc = acc + tap * jnp.concatenate([w_ref[j]] * (rc // SUBLANES), axis=0)
        y = acc + cb_ref[...]
        mu = jnp.mean(y, axis=-1, keepdims=True)
        dlt = y - mu
        var = jnp.mean(dlt * dlt, axis=-1, keepdims=True)
        z = dlt * lax.rsqrt(var + LN_EPS) * lg_ref[...] + lb_ref[...]
        o_ref[c * rc:(c + 1) * rc, :] = (z * _sigmoid(z)).astype(o_ref.dtype)


def _conformer(proj, conv_w, conv_b, ln_g, ln_b, *, ctx_len, tm, skip_ctx):
    bsz, tt, _ = proj.shape
    d = D_MODEL
    ca = COL_GLU // d
    off = ctx_len // tm if skip_ctx else 0
    prev_a, next_a = _halo_specs(tm, CONV_HALO, tt, ca, d, off)
    prev_b, next_b = _halo_specs(tm, CONV_HALO, tt, ca + 1, d, off)
    vec = lambda: pl.BlockSpec((1, d), lambda b, i: (0, 0))
    return pl.pallas_call(
        functools.partial(_conformer_kernel, n_ctx_tiles=ctx_len // tm, n_tiles=tt // tm, tm=tm, rc=32, tile0=off),
        grid=(bsz, tt // tm - off),
        in_specs=[pl.BlockSpec((None, tm, d), lambda b, i: (b, i + off, ca)),
                  pl.BlockSpec((None, tm, d), lambda b, i: (b, i + off, ca + 1)),
                  prev_a, prev_b, next_a, next_b,
                  pl.BlockSpec((CONV_K, SUBLANES, d), lambda b, i: (0, 0, 0)),
                  vec(), vec(), vec()],
        out_specs=pl.BlockSpec((None, tm, d), lambda b, i: (b, i, 0)),
        out_shape=jax.ShapeDtypeStruct((bsz, tt - off * tm, d), BF16),
        scratch_shapes=[pltpu.VMEM((tm + 2 * CONV_HALO, d), F32),
                        pltpu.VMEM((SUBLANES - 1, tm + 2 * CONV_HALO - SUBLANES, d), F32)],
        compiler_params=_cparams(("parallel", "parallel")),
        name="conformer",
    )(proj, proj, proj, proj, proj, proj, jnp.broadcast_to(conv_w[:, None, :], (CONV_K, SUBLANES, d)),
      conv_b.reshape(1, d), ln_g.reshape(1, d), ln_b.reshape(1, d))


def _rwkv_prep_kernel(rc_ref, kc_ref, vc_ref, rp_ref, kp_ref, vp_ref, rn_ref, kn_ref, vn_ref, lora_ref,
                      sw_ref, kk_ref, ka_ref, w0_ref, dup_ref, a0_ref, iup_ref, e_ref, et_ref,
                      r_ref, v_ref, kap_ref, lw_ref, kd_ref, bd_ref, buf_ref,
                      *, n_ctx_tiles, n_tiles, tm):
    first, last = _edge_flags(pl.program_id(1), n_ctx_tiles, n_tiles)
    h = SUBLANES
    d = D_MODEL

    def shift(cur_ref, prev_ref, next_ref, col):
        buf_ref[0:h, :] = jnp.where(first, 0.0, prev_ref[...])
        buf_ref[h:h + tm, :] = cur_ref[...]
        buf_ref[h + tm:h + tm + h, :] = jnp.where(last, 0.0, next_ref[...])
        sl = slice(col * d, (col + 1) * d)
        return (buf_ref[h - 1:h - 1 + tm, :] * sw_ref[0:1, sl] + buf_ref[h:h + tm, :] * sw_ref[1:2, sl]
                + buf_ref[h + 1:h + 1 + tm, :] * sw_ref[2:3, sl])

    r = shift(rc_ref, rp_ref, rn_ref, 0)
    k = shift(kc_ref, kp_ref, kn_ref, 1)
    v = shift(vc_ref, vp_ref, vn_ref, 2)
    r_ref[...] = r
    v_ref[...] = v
    kk = k * kk_ref[...]
    ss = _head_sum(kk * kk, e_ref, et_ref)
    kap = kk * lax.rsqrt(jnp.maximum(ss, 1e-12))
    kap_ref[...] = kap
    tw = jnp.tanh(lora_ref[:, 0:LANES])
    la = lora_ref[:, LANES:2 * LANES]
    for dr in range(2):
        z = w0_ref[dr:dr + 1, :] + _dot3(tw, dup_ref[dr])
        lw_ref[dr] = -DECAY_SCALE * _sigmoid(z)
        a = _sigmoid(a0_ref[dr:dr + 1, :] + _dot3(la, iup_ref[dr]))
        kd_ref[dr] = k * (1.0 + (a - 1.0) * ka_ref[...])
        bd_ref[dr] = a * kap


def _rwkv_prep(proj, shift_w, k_k, k_a, decay_w0, decay_up_pad, iclr_a0, iclr_up_pad, e, et, *, ctx_len, tm):
    bsz, tt, _ = proj.shape
    d = D_MODEL
    c0 = COL_RKV // d
    cur = lambda c: pl.BlockSpec((None, tm, d), lambda b, i: (b, i, c))
    halos = [_halo_specs(tm, SUBLANES, tt, c0 + c, d) for c in range(3)]
    full = lambda shape: pl.BlockSpec(shape, lambda b, i: (0,) * len(shape))
    out1 = pl.BlockSpec((None, tm, d), lambda b, i: (b, i, 0))
    out2 = pl.BlockSpec((2, None, tm, d), lambda b, i: (0, b, i, 0))
    s1 = jax.ShapeDtypeStruct((bsz, tt, d), F32)
    s2 = jax.ShapeDtypeStruct((2, bsz, tt, d), F32)
    return pl.pallas_call(
        functools.partial(_rwkv_prep_kernel, n_ctx_tiles=ctx_len // tm, n_tiles=tt // tm, tm=tm),
        grid=(bsz, tt // tm),
        in_specs=[cur(c0), cur(c0 + 1), cur(c0 + 2),
                  halos[0][0], halos[1][0], halos[2][0], halos[0][1], halos[1][1], halos[2][1],
                  pl.BlockSpec((None, tm, 512), lambda b, i: (b, i, COL_LORA // 512)),
                  full((3, 3 * d)), full((1, d)), full((1, d)), full((2, d)), full((2, LANES, d)),
                  full((2, d)), full((2, LANES, d)), full((d, LANES)), full((LANES, d))],
        out_specs=[out1, out1, out1, out2, out2, out2],
        out_shape=[s1, s1, s1, s2, s2, s2],
        scratch_shapes=[pltpu.VMEM((tm + 2 * SUBLANES, d), F32)],
        compiler_params=_cparams(("parallel", "parallel")),
        name="rwkv_prep",
    )(proj, proj, proj, proj, proj, proj, proj, proj, proj, proj,
      shift_w, k_k.reshape(1, d), k_a.reshape(1, d), decay_w0, decay_up_pad, iclr_a0, iclr_up_pad, e, et)


def _scan_kernel(r_ref, v_ref, kap_ref, lw_ref, k_ref, b_ref, y_ref, h_ref):
    c = CHUNK
    c2 = 2 * c
    sgn = 1 - 2 * pl.program_id(1)

    @pl.when(pl.program_id(2) == 0)
    def _():
        h_ref[...] = jnp.zeros_like(h_ref)

    row = lax.broadcasted_iota(jnp.int32, (c, c), 0)
    col = lax.broadcasted_iota(jnp.int32, (c, c), 1)
    incl = jnp.where((col - row) * sgn <= 0, 1.0, 0.0).astype(BF16)
    dot = functools.partial(jnp.dot, preferred_element_type=F32)
    nb = lw_ref.shape[0]
    cum = []
    for bi in range(nb):
        lw = lw_ref[bi]
        hi = lw.astype(BF16)
        rem = lw - hi.astype(F32)
        mid = rem.astype(BF16)
        lo = (rem - mid.astype(F32)).astype(BF16)
        cum.append(dot(incl, hi) + dot(incl, mid) + dot(incl, lo))

    prow = lax.broadcasted_iota(jnp.int32, (c, c2), 0)
    pcol = lax.broadcasted_iota(jnp.int32, (c, c2), 1)
    order = ((pcol & (c - 1)) - prow) * sgn
    strict = order < 0
    upto = order <= 0
    eye = order == 0
    head0 = pcol < RWKV_HEAD

    def stack(x):
        return jnp.concatenate([jnp.where(head0, x, 0.0), jnp.where(head0, 0.0, x)], axis=0)

    def pack(x):
        return jnp.where(head0, x[0:c], x[c:c2])

    units = [(bi, slice(p * LANES, (p + 1) * LANES)) for bi in range(nb) for p in range(N_PAIR)]
    pairs = range(len(units))
    kt, rt, vs, kend, bend, ptot, a = [], [], [], [], [], [], []
    for bi, sl in units:
        cum_p = cum[bi][:, sl]
        lw_p = lw_ref[bi, :, sl]
        tot_p = jnp.sum(lw_p, axis=0, keepdims=True)
        p_inv = jnp.exp(-cum_p)
        p_end = jnp.exp(tot_p - cum_p)
        k = k_ref[bi, :, sl]
        b = b_ref[bi, :, sl]
        kt.append(kap_ref[bi, :, sl] * jnp.exp(cum_p - lw_p))
        rt.append(r_ref[bi, :, sl] * jnp.exp(cum_p))
        vs.append(v_ref[bi, :, sl])
        kend.append(k * p_end)
        bend.append(b * p_end)
        ptot.append(jnp.exp(tot_p))
        a.append(_bdot_nt(jnp.concatenate([kt[-1], rt[-1]], axis=0),
                          jnp.concatenate([stack(b * p_inv), stack(k * p_inv)], axis=0)))
    a_ab = [jnp.where(strict, a[p][0:c, 0:c2], 0.0) for p in pairs]
    a_ak = [jnp.where(strict, a[p][0:c, c2:2 * c2], 0.0) for p in pairs]
    a_rb = [jnp.where(upto, a[p][c:c2, 0:c2], 0.0) for p in pairs]
    a_rk = [jnp.where(upto, a[p][c:c2, c2:2 * c2], 0.0) for p in pairs]

    t = [jnp.where(eye, 1.0, 0.0) - a_ab[p] for p in pairs]
    x = [_bdot(a_ab[p], stack(a_ab[p])) for p in pairs]
    n = 2
    while 2 * n < c:
        xt = [_bdot(jnp.concatenate([x[p], t[p]], axis=0), stack(x[p])) for p in pairs]
        t = [t[p] + xt[p][c:c2] for p in pairs]
        x = [xt[p][0:c] for p in pairs]
        n *= 2
    t = [t[p] + _bdot(t[p], stack(x[p])) for p in pairs]

    asv = [_bdot(jnp.concatenate([a_ak[p], a_rk[p]], axis=0), stack(vs[p])) for p in pairs]
    wu = [_bdot(t[p], jnp.concatenate([stack(kt[p]), stack(asv[p][0:c])], axis=1)) for p in pairs]
    w = [wu[p][:, 0:c2] for p in pairs]
    u0 = [wu[p][:, c2:2 * c2] for p in pairs]
    arb = [_bdot(a_rb[p], jnp.concatenate([stack(w[p]), stack(u0[p])], axis=1)) for p in pairs]
    y0 = [asv[p][c:c2] - arb[p][:, c2:2 * c2] for p in pairs]
    y1 = [rt[p] - arb[p][:, 0:c2] for p in pairs]
    m = [jnp.where(eye, ptot[p], 0.0) - pack(_bdot_tn(bend[p], w[p])) for p in pairs]
    nn = [pack(_bdot_tn(jnp.concatenate([kend[p], -bend[p]], axis=0), jnp.concatenate([vs[p], u0[p]], axis=0)))
          for p in pairs]
    for p in pairs:
        h0 = h_ref[p]
        hh = h0.astype(BF16)
        hl = h0 - hh.astype(F32)
        mh, ml = _split2(m[p])
        top = _bdot(jnp.concatenate([mh, ml, y1[p].astype(BF16)], axis=0), stack(hh.astype(F32)))
        y_ref[units[p][0], :, units[p][1]] = top[c2:c2 + c] + y0[p]
        h_ref[p] = top[0:c] + top[c:c2] + _bdot(mh, stack(hl)) + nn[p]


def _rwkv_scan(r, v, kap, lw, kd, bd, *, ctx_len):
    bsz, tt, d = r.shape
    nc = tt // CHUNK
    ncc = ctx_len // CHUNK

    def chunk(dr, s):
        return jnp.where(dr == 0, s, jnp.where(s < ncc, ncc - 1 - s, nc + ncc - 1 - s))

    ub = SCAN_BATCH if bsz % SCAN_BATCH == 0 else 1
    shared = pl.BlockSpec((ub, CHUNK, d), lambda b, dr, s: (b, chunk(dr, s), 0))
    per_dir = pl.BlockSpec((None, ub, CHUNK, d), lambda b, dr, s: (dr, b, chunk(dr, s), 0))
    return pl.pallas_call(
        _scan_kernel,
        grid=(bsz // ub, 2, nc),
        in_specs=[shared, shared, shared, per_dir, per_dir, per_dir],
        out_specs=per_dir,
        out_shape=jax.ShapeDtypeStruct((2, bsz, tt, d), F32),
        scratch_shapes=[pltpu.VMEM((ub * N_PAIR, CHUNK, LANES), F32)],
        compiler_params=_cparams(("parallel", "parallel", "arbitrary")),
        name="rwkv_scan",
    )(r, v, kap, lw, kd, bd)


def _merge_kernel(a_ref, att_ref, conv_ref, y_ref, r_ref, v_ref, kd_ref, lora_ref, ga_ref, gc_ref, gr_ref,
                  mod_ref, g_ref, gg_ref, gb_ref, rk_ref, gup_ref, e_ref, et_ref,
                  wa_ref, wc_ref, wr_ref, wo_ref, o_ref):
    dot = functools.partial(jnp.dot, preferred_element_type=F32)
    inv = 1.0 / RWKV_HEAD
    y = y_ref[0] + y_ref[1]
    mu = _head_sum(y, e_ref, et_ref) * inv
    dlt = y - mu
    var = _head_sum(dlt * dlt, e_ref, et_ref) * inv
    yn = dlt * lax.rsqrt(var + GN_EPS) * gg_ref[...] + gb_ref[...]
    bonus = _head_sum(r_ref[...] * (kd_ref[0] + kd_ref[1]) * rk_ref[...], e_ref, et_ref) * v_ref[...]
    gate = _bdot(_sigmoid(lora_ref[:, 2 * LANES:3 * LANES]), gup_ref[...])
    rw = ((yn + bonus) * gate).astype(BF16)
    m = (_sigmoid(ga_ref[...]) * dot(att_ref[...], wa_ref[...])
         + _sigmoid(gc_ref[...]) * dot(conv_ref[...], wc_ref[...])
         + _sigmoid(gr_ref[...]) * dot(rw, wr_ref[...]))
    z = dot(m.astype(BF16), wo_ref[...])
    zn = z * lax.rsqrt(jnp.mean(z * z, axis=-1, keepdims=True) + EPS) * g_ref[...]
    o_ref[...] = a_ref[...] + mod_ref[2:3, :] * zn


def _merge(a, att, conv, y, r, v, kd, proj, mod3, g, gn_g, gn_b, r_k, gate_up, e, et, wa, wc, wr, wo,
           *, n_ctx_tiles, ctx_row, tm, skip_ctx):
    bsz, tt, d = a.shape
    cg = COL_GATE // d
    off = n_ctx_tiles if skip_ctx else 0
    loc = pl.BlockSpec((None, tm, d), lambda b, i: (b, i, 0))
    one = pl.BlockSpec((None, tm, d), lambda b, i: (b, i + off, 0))
    two = pl.BlockSpec((2, None, tm, d), lambda b, i: (0, b, i + off, 0))
    gate = lambda c: pl.BlockSpec((None, tm, d), lambda b, i: (b, i + off, cg + c))
    full = lambda shape: pl.BlockSpec(shape, lambda b, i: (0,) * len(shape), pipeline_mode=pl.Buffered(1))
    return pl.pallas_call(
        _merge_kernel,
        grid=(bsz, tt // tm - off),
        in_specs=[one, loc, loc, two, one, one, two,
                  pl.BlockSpec((None, tm, 512), lambda b, i: (b, i + off, COL_LORA // 512)),
                  gate(0), gate(1), gate(2),
                  pl.BlockSpec((None, N_MOD, d), _mod_row(n_ctx_tiles - off, ctx_row)),
                  full((1, d)), full((1, d)), full((1, d)), full((1, d)), full((LANES, d)),
                  full((d, LANES)), full((LANES, d)),
                  full((d, d)), full((d, d)), full((d, d)), full((d, d))],
        out_specs=loc,
        out_shape=jax.ShapeDtypeStruct((bsz, tt - off * tm, d), F32),
        compiler_params=_cparams(("parallel", "parallel")),
        name="merge",
    )(a, att, conv, y, r, v, kd, proj, proj, proj, proj, mod3, g.reshape(1, d), gn_g.reshape(1, d),
      gn_b.reshape(1, d), r_k.reshape(1, d), gate_up, e, et, wa, wc, wr, wo)


def _ffn_tail_kernel(a_ref, zc_ref, zp_ref, zn_ref, cw_ref, mod_ref, g_ref, wd_ref, o_ref, buf_ref,
                     *, n_ctx_tiles, n_tiles, tm):
    first, last = _edge_flags(pl.program_id(1), n_ctx_tiles, n_tiles)
    h = SUBLANES
    buf_ref[0:h, :] = jnp.where(first, 0.0, zp_ref[...])
    buf_ref[h:h + tm, :] = zc_ref[...]
    buf_ref[h + tm:h + tm + h, :] = jnp.where(last, 0.0, zn_ref[...])

    def conv(sl):
        return (buf_ref[h - 1:h - 1 + tm, sl] * cw_ref[0:1, sl] + buf_ref[h:h + tm, sl] * cw_ref[1:2, sl]
                + buf_ref[h + 1:h + 1 + tm, sl] * cw_ref[2:3, sl])

    gate = conv(slice(0, D_FF))
    val = conv(slice(D_FF, 2 * D_FF))
    u = (gate * _sigmoid(gate) * val).astype(BF16)
    z = jnp.dot(u, wd_ref[...], preferred_element_type=F32)
    zn = z * lax.rsqrt(jnp.mean(z * z, axis=-1, keepdims=True) + EPS) * g_ref[...]
    o_ref[...] = a_ref[...] + mod_ref[5:6, :] * zn


def _ffn_tail(a, z, conv_w, mod3, g, wd, *, n_ctx_tiles, ctx_row, tm):
    bsz, tt, d = a.shape
    f2 = 2 * D_FF
    prev, nxt = _halo_specs(tm, SUBLANES, tt, 0, f2)
    one = pl.BlockSpec((None, tm, d), lambda b, i: (b, i, 0))
    return pl.pallas_call(
        functools.partial(_ffn_tail_kernel, n_ctx_tiles=n_ctx_tiles, n_tiles=tt // tm, tm=tm),
        grid=(bsz, tt // tm),
        in_specs=[one, pl.BlockSpec((None, tm, f2), lambda b, i: (b, i, 0)), prev, nxt,
                  pl.BlockSpec((3, f2), lambda b, i: (0, 0)),
                  pl.BlockSpec((None, N_MOD, d), _mod_row(n_ctx_tiles, ctx_row)),
                  pl.BlockSpec((1, d), lambda b, i: (0, 0)),
                  pl.BlockSpec((D_FF, d), lambda b, i: (0, 0))],
        out_specs=one,
        out_shape=jax.ShapeDtypeStruct((bsz, tt, d), F32),
        scratch_shapes=[pltpu.VMEM((tm + 2 * SUBLANES, f2), F32)],
        compiler_params=_cparams(("parallel", "parallel")),
        name="ffn_tail",
    )(a, z, z, z, conv_w, mod3, g.reshape(1, d), wd)


def _rope_tables(ctx_len, seq):
    rows = seq // GRID_W
    row = jnp.repeat(jnp.arange(rows, dtype=F32), GRID_W)
    col = jnp.tile(jnp.arange(GRID_W, dtype=F32), rows)
    inv_freq = ROPE_THETA ** (-jnp.arange(N_FREQ, dtype=F32) / N_FREQ)
    ang_r = row[:, None] * inv_freq
    ang_c = col[:, None] * inv_freq
    cos = jnp.concatenate([jnp.cos(ang_r), jnp.cos(ang_r), jnp.cos(ang_c), jnp.cos(ang_c)], axis=-1)
    sin = jnp.concatenate([-jnp.sin(ang_r), jnp.sin(ang_r), -jnp.sin(ang_c), jnp.sin(ang_c)], axis=-1)
    cos = jnp.concatenate([jnp.ones((ctx_len, HEAD_DIM), F32), cos], axis=0)
    sin = jnp.concatenate([jnp.zeros((ctx_len, HEAD_DIM), F32), sin], axis=0)
    return cos, sin


def _reorder_w_in(w):
    qkv = w[:, 0:1536]
    glu = w[:, 1536:3584]
    rkv = w[:, 3584:6656]
    lora = w[:, 6656:7040]
    gates = w[:, 7040:10112]
    pad = jnp.zeros((w.shape[0], LANES), w.dtype)
    return jnp.concatenate([qkv, lora, pad, glu, rkv, gates], axis=1).astype(BF16)


def _pad_lora_up(up):
    z = jnp.zeros_like(up[0])
    return jnp.stack([jnp.concatenate([up[0], z], axis=0), jnp.concatenate([z, up[1]], axis=0)])


def kernel(x, c, ctx, c_ctx, w_mod, b_mod, g_pre_mix, g_post_mix, g_pre_ffn, g_post_ffn, w_in, q_norm, k_norm,
           w_attn_o, conv_w, conv_b, conv_ln_g, conv_ln_b, w_conv_o, shift_w, decay_w0, decay_up, iclr_a0,
           iclr_up, gate_up, k_k, k_a, r_k, wkv_gn_g, wkv_gn_b, w_rwkv_o, w_out, w_ffn_up, ffn_conv_w, w_ffn_down):
    bsz, seq, d = x.shape
    ctx_len = ctx.shape[1]
    depth = w_mod.shape[0]
    tm = min(ROW_TILE, ctx_len)
    assert d == D_MODEL and ctx_len % tm == 0 and seq % tm == 0 and ctx_len % CHUNK == 0 and seq % CHUNK == 0
    n_ctx_tiles = ctx_len // tm

    a = jnp.concatenate([ctx, x], axis=1)
    mod_rows = -(-(bsz + 1) // SUBLANES) * SUBLANES
    cc = jnp.zeros((mod_rows, d), F32).at[:bsz].set(c).at[bsz].set(c_ctx)
    cos, sin = _rope_tables(ctx_len, seq)
    head = jnp.arange(d, dtype=jnp.int32) // RWKV_HEAD
    e = (head[:, None] == jnp.arange(LANES, dtype=jnp.int32)[None, :]).astype(BF16)
    et = e.T
    def seg_rows(segs, n_ctx):
        return jnp.array([bsz if s % segs < n_ctx else s // segs for s in range(bsz * segs)], jnp.int32)

    for l in range(depth):
        last = l == depth - 1
        mod3 = _mod_call(cc, w_mod[l], b_mod[l]).reshape(mod_rows, N_MOD, d)
        modseg = jnp.take(mod3, seg_rows((ctx_len + seq) // tm, n_ctx_tiles), axis=0)
        proj = _nm_matmul(a, modseg, g_pre_mix[l], _reorder_w_in(w_in[l]), shift_idx=0, tn=2048, seg=tm)
        q, k, v = _qk_prep(proj, cos, sin, q_norm[l], k_norm[l], tm=tm)
        att = _attention(q, k, v, ctx_len=ctx_len, tq=tm, skip_ctx=last)
        conv = _conformer(proj, conv_w[l], conv_b[l], conv_ln_g[l], conv_ln_b[l], ctx_len=ctx_len, tm=tm,
                          skip_ctx=last)
        r, vv, kap, lw, kd, bd = _rwkv_prep(proj, shift_w[l], k_k[l], k_a[l], decay_w0[l], _pad_lora_up(decay_up[l]),
                                            iclr_a0[l], _pad_lora_up(iclr_up[l]), e, et, ctx_len=ctx_len, tm=tm)
        y = _rwkv_scan(r, vv, kap, lw, kd, bd, ctx_len=ctx_len)
        a = _merge(a, att, conv, y, r, vv, kd, proj, mod3, g_post_mix[l], wkv_gn_g[l], wkv_gn_b[l], r_k[l],
                   gate_up[l].astype(BF16), e, et, w_attn_o[l].astype(BF16), w_conv_o[l].astype(BF16),
                   w_rwkv_o[l].astype(BF16), w_out[l].astype(BF16), n_ctx_tiles=n_ctx_tiles, ctx_row=bsz, tm=tm,
                   skip_ctx=last)
        n_ctx = 0 if last else n_ctx_tiles
        if last:
            modseg = jnp.take(mod3, seg_rows(seq // tm, 0), axis=0)
        z = _nm_matmul(a, modseg, g_pre_ffn[l], w_ffn_up[l].astype(BF16), shift_idx=3, tn=D_FF // 2, seg=tm)
        a = _ffn_tail(a, z, ffn_conv_w[l], mod3, g_post_ffn[l], w_ffn_down[l].astype(BF16),
                      n_ctx_tiles=n_ctx, ctx_row=bsz, tm=tm)
    return a
```

```python
import functools
import math

import jax
import jax.numpy as jnp
from jax import lax
from jax.experimental import pallas as pl
from jax.experimental.pallas import tpu as pltpu

F32 = jnp.float32
BF16 = jnp.bfloat16

D_MODEL = 1024
GRID_W = 64
N_Q_HEADS = 8
N_KV_HEADS = 2
GQA_GROUP = N_Q_HEADS // N_KV_HEADS
HEAD_DIM = 128
N_FREQ = HEAD_DIM // 4
ROPE_THETA = 10000.0
CONV_K = 31
CONV_HALO = 16
RWKV_HEAD = 64
RWKV_HEADS = D_MODEL // RWKV_HEAD
DECAY_SCALE = math.exp(-0.5)
D_FF = 2816
N_MOD = 6
EPS = 1e-6
LN_EPS = 1e-5
GN_EPS = RWKV_HEAD * 1e-5
SOFTMAX_C = (HEAD_DIM ** -0.5) * math.log2(math.e)
LANES = 128
SUBLANES = 8
CHUNK = 64
N_PAIR = D_MODEL // LANES
SCAN_BATCH = 2
ROW_TILE = 256
VMEM_LIMIT = 48 * 1024 * 1024

COL_Q = 0
COL_KV = 1024
COL_LORA = 1536
COL_GLU = 2048
COL_RKV = 4096
COL_GATE = 7168
N_IN_PAD = 10240


def _cparams(sem):
    return pltpu.CompilerParams(dimension_semantics=sem, vmem_limit_bytes=VMEM_LIMIT)


def _bdot(a, b):
    return jnp.dot(a.astype(BF16), b.astype(BF16), preferred_element_type=F32)


def _bdot_nt(a, b):
    return lax.dot_general(a.astype(BF16), b.astype(BF16), (((1,), (1,)), ((), ())),
                           preferred_element_type=F32)


def _bdot_tn(a, b):
    return lax.dot_general(a.astype(BF16), b.astype(BF16), (((0,), (0,)), ((), ())),
                           preferred_element_type=F32)


def _split2(x):
    hi = x.astype(BF16)
    lo = (x - hi.astype(F32)).astype(BF16)
    return hi, lo


def _dot3(a, b):
    ah, al = _split2(a)
    bh, bl = _split2(b)
    dot = functools.partial(jnp.dot, preferred_element_type=F32)
    return dot(ah, bh) + dot(al, bh) + dot(ah, bl)


def _sigmoid(x):
    return jax.nn.sigmoid(x)


def _head_sum(x, e_ref, et_ref):
    xh, xl = _split2(x)
    dot = functools.partial(jnp.dot, preferred_element_type=F32)
    s = dot(xh, e_ref[...]) + dot(xl, e_ref[...])
    sh, sl = _split2(s)
    return dot(sh, et_ref[...]) + dot(sl, et_ref[...])


def _mod_kernel(c_ref, w_ref, b_ref, o_ref):
    c = c_ref[...]
    o_ref[...] = _dot3(c * _sigmoid(c), w_ref[...]) + b_ref[...]


def _mod_call(cc, w_mod, b_mod):
    rows = cc.shape[0]
    n = w_mod.shape[1]
    tn = 1536
    return pl.pallas_call(
        _mod_kernel,
        grid=(n // tn,),
        in_specs=[pl.BlockSpec((rows, D_MODEL), lambda j: (0, 0)),
                  pl.BlockSpec((D_MODEL, tn), lambda j: (0, j)),
                  pl.BlockSpec((1, tn), lambda j: (0, j))],
        out_specs=pl.BlockSpec((rows, tn), lambda j: (0, j)),
        out_shape=jax.ShapeDtypeStruct((rows, n), F32),
        compiler_params=_cparams(("parallel",)),
        name="mod",
    )(cc, w_mod, b_mod.reshape(1, n))


def _mod_row(n_ctx_tiles, ctx_row):
    return lambda b, i, *_: (jnp.where(i < n_ctx_tiles, ctx_row, b), 0, 0)


def _nm_matmul_kernel(a_ref, mod_ref, g_ref, w_ref, o_ref, h_ref, *, shift_idx, seg, nseg):
    @pl.when(pl.program_id(1) == 0)
    def _():
        for s in range(nseg):
            rows = slice(s * seg, (s + 1) * seg)
            x = a_ref[rows, :]
            y = x * lax.rsqrt(jnp.mean(x * x, axis=-1, keepdims=True) + EPS) * g_ref[...]
            h = y * (1.0 + mod_ref[s, shift_idx + 1:shift_idx + 2, :]) + mod_ref[s, shift_idx:shift_idx + 1, :]
            h_ref[rows, :] = h.astype(BF16)

    o_ref[...] = jnp.dot(h_ref[...], w_ref[...], preferred_element_type=F32).astype(o_ref.dtype)


def _nm_matmul(a, modseg, g, w, *, shift_idx, tn, seg):
    bsz, tt, d = a.shape
    n = w.shape[1]
    rows = bsz * tt
    nseg = max(s for s in (4, 2, 1) if (rows // seg) % s == 0)
    tm = seg * nseg
    out = pl.pallas_call(
        functools.partial(_nm_matmul_kernel, shift_idx=shift_idx, seg=seg, nseg=nseg),
        grid=(rows // tm, n // tn),
        in_specs=[pl.BlockSpec((tm, d), lambda i, j: (i, 0)),
                  pl.BlockSpec((nseg, N_MOD, d), lambda i, j: (i, 0, 0)),
                  pl.BlockSpec((1, d), lambda i, j: (0, 0)),
                  pl.BlockSpec((d, tn), lambda i, j: (0, j))],
        out_specs=pl.BlockSpec((tm, tn), lambda i, j: (i, j)),
        out_shape=jax.ShapeDtypeStruct((rows, n), F32),
        scratch_shapes=[pltpu.VMEM((tm, d), BF16)],
        compiler_params=_cparams(("parallel", "arbitrary")),
        name="norm_mod_matmul",
    )(a.reshape(rows, d), modseg, g.reshape(1, d), w)
    return out.reshape(bsz, tt, n)


def _qk_prep_kernel(q_ref, kv_ref, cos_ref, sin_ref, qn_ref, kn_ref, qo_ref, ko_ref, vo_ref):
    cos = cos_ref[...]
    sin = sin_ref[...]
    lane = lax.broadcasted_iota(jnp.int32, cos.shape, 1)
    first = (lane & (N_FREQ)) == 0

    def norm_rope(x, g):
        y = x * lax.rsqrt(jnp.mean(x * x, axis=-1, keepdims=True) + EPS) * g
        partner = jnp.where(first, pltpu.roll(y, LANES - N_FREQ, 1), pltpu.roll(y, N_FREQ, 1))
        return y * cos + partner * sin

    for h in range(N_Q_HEADS):
        sl = slice(h * HEAD_DIM, (h + 1) * HEAD_DIM)
        qo_ref[:, sl] = (norm_rope(q_ref[:, sl], qn_ref[...]) * SOFTMAX_C).astype(BF16)
    for h in range(N_KV_HEADS):
        sl = slice(h * HEAD_DIM, (h + 1) * HEAD_DIM)
        ko_ref[:, sl] = norm_rope(kv_ref[:, sl], kn_ref[...]).astype(BF16)
    vo_ref[...] = kv_ref[:, N_KV_HEADS * HEAD_DIM:].astype(BF16)


def _qk_prep(proj, cos, sin, q_norm, k_norm, *, tm):
    bsz, tt, _ = proj.shape
    dq = N_Q_HEADS * HEAD_DIM
    dkv = N_KV_HEADS * HEAD_DIM
    return pl.pallas_call(
        _qk_prep_kernel,
        grid=(bsz, tt // tm),
        in_specs=[pl.BlockSpec((None, tm, dq), lambda b, i: (b, i, COL_Q // dq)),
                  pl.BlockSpec((None, tm, 2 * dkv), lambda b, i: (b, i, COL_KV // (2 * dkv))),
                  pl.BlockSpec((tm, HEAD_DIM), lambda b, i: (i, 0)),
                  pl.BlockSpec((tm, HEAD_DIM), lambda b, i: (i, 0)),
                  pl.BlockSpec((1, HEAD_DIM), lambda b, i: (0, 0)),
                  pl.BlockSpec((1, HEAD_DIM), lambda b, i: (0, 0))],
        out_specs=[pl.BlockSpec((None, tm, dq), lambda b, i: (b, i, 0)),
                   pl.BlockSpec((None, tm, dkv), lambda b, i: (b, i, 0)),
                   pl.BlockSpec((None, tm, dkv), lambda b, i: (b, i, 0))],
        out_shape=[jax.ShapeDtypeStruct((bsz, tt, dq), BF16),
                   jax.ShapeDtypeStruct((bsz, tt, dkv), BF16),
                   jax.ShapeDtypeStruct((bsz, tt, dkv), BF16)],
        compiler_params=_cparams(("parallel", "parallel")),
        name="qk_prep",
    )(proj, proj, cos, sin, q_norm.reshape(1, HEAD_DIM), k_norm.reshape(1, HEAD_DIM))


def _attn_kernel(q_ref, k_ref, v_ref, o_ref, *, n_ctx_tiles, ctx_len, skip_ctx):
    tq = q_ref.shape[0]
    tt = k_ref.shape[0]
    half = (tt - ctx_len) // 2

    def attend(spans):
        q = jnp.concatenate([q_ref[:, g * HEAD_DIM:(g + 1) * HEAD_DIM] for g in range(GQA_GROUP)], axis=0)
        m = l = acc = None
        for start, size in spans:
            k = k_ref[start:start + size, :]
            v = v_ref[start:start + size, :]
            s = lax.dot_general(q, k, (((1,), (1,)), ((), ())), preferred_element_type=F32)
            mt = jnp.max(s, axis=-1, keepdims=True)
            if m is None:
                m = mt
                p = jnp.exp2(s - m)
                l = jnp.sum(p, axis=-1, keepdims=True)
                acc = jnp.dot(p.astype(BF16), v, preferred_element_type=F32)
            else:
                m_new = jnp.maximum(m, mt)
                alpha = jnp.exp2(m - m_new)
                p = jnp.exp2(s - m_new)
                l = alpha * l + jnp.sum(p, axis=-1, keepdims=True)
                acc = alpha * acc + jnp.dot(p.astype(BF16), v, preferred_element_type=F32)
                m = m_new
        o = acc / l
        for g in range(GQA_GROUP):
            o_ref[:, g * HEAD_DIM:(g + 1) * HEAD_DIM] = o[g * tq:(g + 1) * tq].astype(o_ref.dtype)

    x_spans = [(0, ctx_len), (ctx_len, half), (ctx_len + half, half)]
    if skip_ctx:
        attend(x_spans)
        return
    i = pl.program_id(2)

    @pl.when(i < n_ctx_tiles)
    def _():
        attend([(0, ctx_len)])

    @pl.when(i >= n_ctx_tiles)
    def _():
        attend(x_spans)


def _attention(q, k, v, *, ctx_len, tq, skip_ctx):
    bsz, tt, dq = q.shape
    gw = GQA_GROUP * HEAD_DIM
    off = ctx_len // tq if skip_ctx else 0
    return pl.pallas_call(
        functools.partial(_attn_kernel, n_ctx_tiles=ctx_len // tq, ctx_len=ctx_len, skip_ctx=skip_ctx),
        grid=(bsz, N_KV_HEADS, tt // tq - off),
        in_specs=[pl.BlockSpec((None, tq, gw), lambda b, h, i: (b, i + off, h)),
                  pl.BlockSpec((None, tt, HEAD_DIM), lambda b, h, i: (b, 0, h)),
                  pl.BlockSpec((None, tt, HEAD_DIM), lambda b, h, i: (b, 0, h))],
        out_specs=pl.BlockSpec((None, tq, gw), lambda b, h, i: (b, i, h)),
        out_shape=jax.ShapeDtypeStruct((bsz, tt - off * tq, dq), BF16),
        compiler_params=_cparams(("parallel", "parallel", "arbitrary")),
        name="attention",
    )(q, k, v)


def _halo_specs(tm, halo, tt, col_block, width, off=0):
    per = tm // halo
    last = tt // halo - 1
    prev = pl.BlockSpec((None, halo, width), lambda b, i: (b, jnp.maximum((i + off) * per - 1, 0), col_block))
    nxt = pl.BlockSpec((None, halo, width), lambda b, i: (b, jnp.minimum((i + off + 1) * per, last), col_block))
    return prev, nxt


def _edge_flags(i, n_ctx_tiles, n_tiles):
    first = jnp.logical_or(i == 0, i == n_ctx_tiles)
    last = jnp.logical_or(i == n_ctx_tiles - 1, i == n_tiles - 1)
    return first, last


def _conformer_kernel(ac_ref, bc_ref, ap_ref, bp_ref, an_ref, bn_ref, w_ref, cb_ref, lg_ref, lb_ref,
                      o_ref, buf_ref, sh_ref, *, n_ctx_tiles, n_tiles, tm, rc, tile0):
    first, last = _edge_flags(pl.program_id(1) + tile0, n_ctx_tiles, n_tiles)
    h = CONV_HALO
    buf_ref[0:h, :] = jnp.where(first, 0.0, ap_ref[...] * _sigmoid(bp_ref[...]))
    buf_ref[h:h + tm, :] = ac_ref[...] * _sigmoid(bc_ref[...])
    buf_ref[h + tm:h + tm + h, :] = jnp.where(last, 0.0, an_ref[...] * _sigmoid(bn_ref[...]))
    n_sh = sh_ref.shape[1]
    for s in range(1, SUBLANES):
        sh_ref[s - 1] = buf_ref[s:s + n_sh, :]
    off = h - CONV_K // 2
    for c in range(tm // rc):
        acc = jnp.zeros((rc, D_MODEL), F32)
        for j in range(CONV_K):
            s = (j + off) % SUBLANES
            r0 = c * rc + (j + off) - s
            tap = buf_ref[r0:r0 + rc, :] if s == 0 else sh_ref[s - 1, r0:r0 + rc, :]
            acc = acc + tap * jnp.concatenate([w_ref[j]] * (rc // SUBLANES), axis=0)
        y = acc + cb_ref[...]
        mu = jnp.mean(y, axis=-1, keepdims=True)
        dlt = y - mu
        var = jnp.mean(dlt * dlt, axis=-1, keepdims=True)
        z = dlt * lax.rsqrt(var + LN_EPS) * lg_ref[...] + lb_ref[...]
        o_ref[c * rc:(c + 1) * rc, :] = (z * _sigmoid(z)).astype(o_ref.dtype)


def _conformer(proj, conv_w, conv_b, ln_g, ln_b, *, ctx_len, tm, skip_ctx):
    bsz, tt, _ = proj.shape
    d = D_MODEL
    ca = COL_GLU // d
    off = ctx_len // tm if skip_ctx else 0
    prev_a, next_a = _halo_specs(tm, CONV_HALO, tt, ca, d, off)
    prev_b, next_b = _halo_specs(tm, CONV_HALO, tt, ca + 1, d, off)
    vec = lambda: pl.BlockSpec((1, d), lambda b, i: (0, 0))
    return pl.pallas_call(
        functools.partial(_conformer_kernel, n_ctx_tiles=ctx_len // tm, n_tiles=tt // tm, tm=tm, rc=32, tile0=off),
        grid=(bsz, tt // tm - off),
        in_specs=[pl.BlockSpec((None, tm, d), lambda b, i: (b, i + off, ca)),
                  pl.BlockSpec((None, tm, d), lambda b, i: (b, i + off, ca + 1)),
                  prev_a, prev_b, next_a, next_b,
                  pl.BlockSpec((CONV_K, SUBLANES, d), lambda b, i: (0, 0, 0)),
                  vec(), vec(), vec()],
        out_specs=pl.BlockSpec((None, tm, d), lambda b, i: (b, i, 0)),
        out_shape=jax.ShapeDtypeStruct((bsz, tt - off * tm, d), BF16),
        scratch_shapes=[pltpu.VMEM((tm + 2 * CONV_HALO, d), F32),
                        pltpu.VMEM((SUBLANES - 1, tm + 2 * CONV_HALO - SUBLANES, d), F32)],
        compiler_params=_cparams(("parallel", "parallel")),
        name="conformer",
    )(proj, proj, proj, proj, proj, proj, jnp.broadcast_to(conv_w[:, None, :], (CONV_K, SUBLANES, d)),
      conv_b.reshape(1, d), ln_g.reshape(1, d), ln_b.reshape(1, d))


def _rwkv_prep_kernel(rc_ref, kc_ref, vc_ref, rp_ref, kp_ref, vp_ref, rn_ref, kn_ref, vn_ref, lora_ref,
                      sw_ref, kk_ref, ka_ref, w0_ref, dup_ref, a0_ref, iup_ref, e_ref, et_ref,
                      r_ref, v_ref, kap_ref, lw_ref, kd_ref, bd_ref, buf_ref,
                      *, n_ctx_tiles, n_tiles, tm):
    first, last = _edge_flags(pl.program_id(1), n_ctx_tiles, n_tiles)
    h = SUBLANES
    d = D_MODEL

    def shift(cur_ref, prev_ref, next_ref, col):
        buf_ref[0:h, :] = jnp.where(first, 0.0, prev_ref[...])
        buf_ref[h:h + tm, :] = cur_ref[...]
        buf_ref[h + tm:h + tm + h, :] = jnp.where(last, 0.0, next_ref[...])
        sl = slice(col * d, (col + 1) * d)
        return (buf_ref[h - 1:h - 1 + tm, :] * sw_ref[0:1, sl] + buf_ref[h:h + tm, :] * sw_ref[1:2, sl]
                + buf_ref[h + 1:h + 1 + tm, :] * sw_ref[2:3, sl])

    r = shift(rc_ref, rp_ref, rn_ref, 0)
    k = shift(kc_ref, kp_ref, kn_ref, 1)
    v = shift(vc_ref, vp_ref, vn_ref, 2)
    r_ref[...] = r
    v_ref[...] = v
    kk = k * kk_ref[...]
    ss = _head_sum(kk * kk, e_ref, et_ref)
    kap = kk * lax.rsqrt(jnp.maximum(ss, 1e-12))
    kap_ref[...] = kap
    tw = jnp.tanh(lora_ref[:, 0:LANES])
    la = lora_ref[:, LANES:2 * LANES]
    for dr in range(2):
        z = w0_ref[dr:dr + 1, :] + _dot3(tw, dup_ref[dr])
        lw_ref[dr] = -DECAY_SCALE * _sigmoid(z)
        a = _sigmoid(a0_ref[dr:dr + 1, :] + _dot3(la, iup_ref[dr]))
        kd_ref[dr] = k * (1.0 + (a - 1.0) * ka_ref[...])
        bd_ref[dr] = a * kap


def _rwkv_prep(proj, shift_w, k_k, k_a, decay_w0, decay_up_pad, iclr_a0, iclr_up_pad, e, et, *, ctx_len, tm):
    bsz, tt, _ = proj.shape
    d = D_MODEL
    c0 = COL_RKV // d
    cur = lambda c: pl.BlockSpec((None, tm, d), lambda b, i: (b, i, c))
    halos = [_halo_specs(tm, SUBLANES, tt, c0 + c, d) for c in range(3)]
    full = lambda shape: pl.BlockSpec(shape, lambda b, i: (0,) * len(shape))
    out1 = pl.BlockSpec((None, tm, d), lambda b, i: (b, i, 0))
    out2 = pl.BlockSpec((2, None, tm, d), lambda b, i: (0, b, i, 0))
    s1 = jax.ShapeDtypeStruct((bsz, tt, d), F32)
    s2 = jax.ShapeDtypeStruct((2, bsz, tt, d), F32)
    return pl.pallas_call(
        functools.partial(_rwkv_prep_kernel, n_ctx_tiles=ctx_len // tm, n_tiles=tt // tm, tm=tm),
        grid=(bsz, tt // tm),
        in_specs=[cur(c0), cur(c0 + 1), cur(c0 + 2),
                  halos[0][0], halos[1][0], halos[2][0], halos[0][1], halos[1][1], halos[2][1],
                  pl.BlockSpec((None, tm, 512), lambda b, i: (b, i, COL_LORA // 512)),
                  full((3, 3 * d)), full((1, d)), full((1, d)), full((2, d)), full((2, LANES, d)),
                  full((2, d)), full((2, LANES, d)), full((d, LANES)), full((LANES, d))],
        out_specs=[out1, out1, out1, out2, out2, out2],
        out_shape=[s1, s1, s1, s2, s2, s2],
        scratch_shapes=[pltpu.VMEM((tm + 2 * SUBLANES, d), F32)],
        compiler_params=_cparams(("parallel", "parallel")),
        name="rwkv_prep",
    )(proj, proj, proj, proj, proj, proj, proj, proj, proj, proj,
      shift_w, k_k.reshape(1, d), k_a.reshape(1, d), decay_w0, decay_up_pad, iclr_a0, iclr_up_pad, e, et)


def _scan_kernel(r_ref, v_ref, kap_ref, lw_ref, k_ref, b_ref, y_ref, h_ref):
    c = CHUNK
    c2 = 2 * c
    sgn = 1 - 2 * pl.program_id(1)

    @pl.when(pl.program_id(2) == 0)
    def _():
        h_ref[...] = jnp.zeros_like(h_ref)

    row = lax.broadcasted_iota(jnp.int32, (c, c), 0)
    col = lax.broadcasted_iota(jnp.int32, (c, c), 1)
    incl = jnp.where((col - row) * sgn <= 0, 1.0, 0.0).astype(BF16)
    dot = functools.partial(jnp.dot, preferred_element_type=F32)
    nb = lw_ref.shape[0]
    cum = []
    for bi in range(nb):
        lw = lw_ref[bi]
        hi = lw.astype(BF16)
        rem = lw - hi.astype(F32)
        mid = rem.astype(BF16)
        lo = (rem - mid.astype(F32)).astype(BF16)
        cum.append(dot(incl, hi) + dot(incl, mid) + dot(incl, lo))

    prow = lax.broadcasted_iota(jnp.int32, (c, c2), 0)
    pcol = lax.broadcasted_iota(jnp.int32, (c, c2), 1)
    order = ((pcol & (c - 1)) - prow) * sgn
    strict = order < 0
    upto = order <= 0
    eye = order == 0
    head0 = pcol < RWKV_HEAD

    def stack(x):
        return jnp.concatenate([jnp.where(head0, x, 0.0), jnp.where(head0, 0.0, x)], axis=0)

    def pack(x):
        return jnp.where(head0, x[0:c], x[c:c2])

    units = [(bi, slice(p * LANES, (p + 1) * LANES)) for bi in range(nb) for p in range(N_PAIR)]
    pairs = range(len(units))
    kt, rt, vs, kend, bend, ptot, a = [], [], [], [], [], [], []
    for bi, sl in units:
        cum_p = cum[bi][:, sl]
        lw_p = lw_ref[bi, :, sl]
        tot_p = jnp.sum(lw_p, axis=0, keepdims=True)
        p_inv = jnp.exp(-cum_p)
        p_end = jnp.exp(tot_p - cum_p)
        k = k_ref[bi, :, sl]
        b = b_ref[bi, :, sl]
        kt.append(kap_ref[bi, :, sl] * jnp.exp(cum_p - lw_p))
        rt.append(r_ref[bi, :, sl] * jnp.exp(cum_p))
        vs.append(v_ref[bi, :, sl])
        kend.append(k * p_end)
        bend.append(b * p_end)
        ptot.append(jnp.exp(tot_p))
        a.append(_bdot_nt(jnp.concatenate([kt[-1], rt[-1]], axis=0),
                          jnp.concatenate([stack(b * p_inv), stack(k * p_inv)], axis=0)))
    a_ab = [jnp.where(strict, a[p][0:c, 0:c2], 0.0) for p in pairs]
    a_ak = [jnp.where(strict, a[p][0:c, c2:2 * c2], 0.0) for p in pairs]
    a_rb = [jnp.where(upto, a[p][c:c2, 0:c2], 0.0) for p in pairs]
    a_rk = [jnp.where(upto, a[p][c:c2, c2:2 * c2], 0.0) for p in pairs]

    t = [jnp.where(eye, 1.0, 0.0) - a_ab[p] for p in pairs]
    x = [_bdot(a_ab[p], stack(a_ab[p])) for p in pairs]
    n = 2
    while 2 * n < c:
        xt = [_bdot(jnp.concatenate([x[p], t[p]], axis=0), stack(x[p])) for p in pairs]
        t = [t[p] + xt[p][c:c2] for p in pairs]
        x = [xt[p][0:c] for p in pairs]
        n *= 2
    t = [t[p] + _bdot(t[p], stack(x[p])) for p in pairs]

    asv = [_bdot(jnp.concatenate([a_ak[p], a_rk[p]], axis=0), stack(vs[p])) for p in pairs]
    wu = [_bdot(t[p], jnp.concatenate([stack(kt[p]), stack(asv[p][0:c])], axis=1)) for p in pairs]
    w = [wu[p][:, 0:c2] for p in pairs]
    u0 = [wu[p][:, c2:2 * c2] for p in pairs]
    arb = [_bdot(a_rb[p], jnp.concatenate([stack(w[p]), stack(u0[p])], axis=1)) for p in pairs]
    y0 = [asv[p][c:c2] - arb[p][:, c2:2 * c2] for p in pairs]
    y1 = [rt[p] - arb[p][:, 0:c2] for p in pairs]
    m = [jnp.where(eye, ptot[p], 0.0) - pack(_bdot_tn(bend[p], w[p])) for p in pairs]
    nn = [pack(_bdot_tn(jnp.concatenate([kend[p], -bend[p]], axis=0), jnp.concatenate([vs[p], u0[p]], axis=0)))
          for p in pairs]
    for p in pairs:
        h0 = h_ref[p]
        hh = h0.astype(BF16)
        hl = h0 - hh.astype(F32)
        mh, ml = _split2(m[p])
        top = _bdot(jnp.concatenate([mh, ml, y1[p].astype(BF16)], axis=0), stack(hh.astype(F32)))
        y_ref[units[p][0], :, units[p][1]] = top[c2:c2 + c] + y0[p]
        h_ref[p] = top[0:c] + top[c:c2] + _bdot(mh, stack(hl)) + nn[p]


def _rwkv_scan(r, v, kap, lw, kd, bd, *, ctx_len):
    bsz, tt, d = r.shape
    nc = tt // CHUNK
    ncc = ctx_len // CHUNK

    def chunk(dr, s):
        return jnp.where(dr == 0, s, jnp.where(s < ncc, ncc - 1 - s, nc + ncc - 1 - s))

    ub = SCAN_BATCH if bsz % SCAN_BATCH == 0 else 1
    shared = pl.BlockSpec((ub, CHUNK, d), lambda b, dr, s: (b, chunk(dr, s), 0))
    per_dir = pl.BlockSpec((None, ub, CHUNK, d), lambda b, dr, s: (dr, b, chunk(dr, s), 0))
    return pl.pallas_call(
        _scan_kernel,
        grid=(bsz // ub, 2, nc),
        in_specs=[shared, shared, shared, per_dir, per_dir, per_dir],
        out_specs=per_dir,
        out_shape=jax.ShapeDtypeStruct((2, bsz, tt, d), F32),
        scratch_shapes=[pltpu.VMEM((ub * N_PAIR, CHUNK, LANES), F32)],
        compiler_params=_cparams(("parallel", "parallel", "arbitrary")),
        name="rwkv_scan",
    )(r, v, kap, lw, kd, bd)


def _merge_kernel(a_ref, att_ref, conv_ref, y_ref, r_ref, v_ref, kd_ref, lora_ref, ga_ref, gc_ref, gr_ref,
                  mod_ref, g_ref, gg_ref, gb_ref, rk_ref, gup_ref, e_ref, et_ref,
                  wa_ref, wc_ref, wr_ref, wo_ref, o_ref):
    dot = functools.partial(jnp.dot, preferred_element_type=F32)
    inv = 1.0 / RWKV_HEAD
    y = y_ref[0] + y_ref[1]
    mu = _head_sum(y, e_ref, et_ref) * inv
    dlt = y - mu
    var = _head_sum(dlt * dlt, e_ref, et_ref) * inv
    yn = dlt * lax.rsqrt(var + GN_EPS) * gg_ref[...] + gb_ref[...]
    bonus = _head_sum(r_ref[...] * (kd_ref[0] + kd_ref[1]) * rk_ref[...], e_ref, et_ref) * v_ref[...]
    gate = _bdot(_sigmoid(lora_ref[:, 2 * LANES:3 * LANES]), gup_ref[...])
    rw = ((yn + bonus) * gate).astype(BF16)
    m = (_sigmoid(ga_ref[...]) * dot(att_ref[...], wa_ref[...])
         + _sigmoid(gc_ref[...]) * dot(conv_ref[...], wc_ref[...])
         + _sigmoid(gr_ref[...]) * dot(rw, wr_ref[...]))
    z = dot(m.astype(BF16), wo_ref[...])
    zn = z * lax.rsqrt(jnp.mean(z * z, axis=-1, keepdims=True) + EPS) * g_ref[...]
    o_ref[...] = a_ref[...] + mod_ref[2:3, :] * zn


def _merge(a, att, conv, y, r, v, kd, proj, mod3, g, gn_g, gn_b, r_k, gate_up, e, et, wa, wc, wr, wo,
           *, n_ctx_tiles, ctx_row, tm, skip_ctx):
    bsz, tt, d = a.shape
    cg = COL_GATE // d
    off = n_ctx_tiles if skip_ctx else 0
    loc = pl.BlockSpec((None, tm, d), lambda b, i: (b, i, 0))
    one = pl.BlockSpec((None, tm, d), lambda b, i: (b, i + off, 0))
    two = pl.BlockSpec((2, None, tm, d), lambda b, i: (0, b, i + off, 0))
    gate = lambda c: pl.BlockSpec((None, tm, d), lambda b, i: (b, i + off, cg + c))
    full = lambda shape: pl.BlockSpec(shape, lambda b, i: (0,) * len(shape), pipeline_mode=pl.Buffered(1))
    return pl.pallas_call(
        _merge_kernel,
        grid=(bsz, tt // tm - off),
        in_specs=[one, loc if att.shape[1] < tt else one, loc if conv.shape[1] < tt else one, two, one, one, two,
                  pl.BlockSpec((None, tm, 512), lambda b, i: (b, i + off, COL_LORA // 512)),
                  gate(0), gate(1), gate(2),
                  pl.BlockSpec((None, N_MOD, d), _mod_row(n_ctx_tiles - off, ctx_row)),
                  full((1, d)), full((1, d)), full((1, d)), full((1, d)), full((LANES, d)),
                  full((d, LANES)), full((LANES, d)),
                  full((d, d)), full((d, d)), full((d, d)), full((d, d))],
        out_specs=loc,
        out_shape=jax.ShapeDtypeStruct((bsz, tt - off * tm, d), F32),
        compiler_params=_cparams(("parallel", "parallel")),
        name="merge",
    )(a, att, conv, y, r, v, kd, proj, proj, proj, proj, mod3, g.reshape(1, d), gn_g.reshape(1, d),
      gn_b.reshape(1, d), r_k.reshape(1, d), gate_up, e, et, wa, wc, wr, wo)


def _ffn_tail_kernel(a_ref, zc_ref, zp_ref, zn_ref, cw_ref, mod_ref, g_ref, wd_ref, o_ref, buf_ref,
                     *, n_ctx_tiles, n_tiles, tm):
    first, last = _edge_flags(pl.program_id(1), n_ctx_tiles, n_tiles)
    h = SUBLANES
    buf_ref[0:h, :] = jnp.where(first, 0.0, zp_ref[...])
    buf_ref[h:h + tm, :] = zc_ref[...]
    buf_ref[h + tm:h + tm + h, :] = jnp.where(last, 0.0, zn_ref[...])

    def conv(sl):
        return (buf_ref[h - 1:h - 1 + tm, sl] * cw_ref[0:1, sl] + buf_ref[h:h + tm, sl] * cw_ref[1:2, sl]
                + buf_ref[h + 1:h + 1 + tm, sl] * cw_ref[2:3, sl])

    gate = conv(slice(0, D_FF))
    val = conv(slice(D_FF, 2 * D_FF))
    u = (gate * _sigmoid(gate) * val).astype(BF16)
    z = jnp.dot(u, wd_ref[...], preferred_element_type=F32)
    zn = z * lax.rsqrt(jnp.mean(z * z, axis=-1, keepdims=True) + EPS) * g_ref[...]
    o_ref[...] = a_ref[...] + mod_ref[5:6, :] * zn


def _ffn_tail(a, z, conv_w, mod3, g, wd, *, n_ctx_tiles, ctx_row, tm):
    bsz, tt, d = a.shape
    f2 = 2 * D_FF
    prev, nxt = _halo_specs(tm, SUBLANES, tt, 0, f2)
    one = pl.BlockSpec((None, tm, d), lambda b, i: (b, i, 0))
    return pl.pallas_call(
        functools.partial(_ffn_tail_kernel, n_ctx_tiles=n_ctx_tiles, n_tiles=tt // tm, tm=tm),
        grid=(bsz, tt // tm),
        in_specs=[one, pl.BlockSpec((None, tm, f2), lambda b, i: (b, i, 0)), prev, nxt,
                  pl.BlockSpec((3, f2), lambda b, i: (0, 0)),
                  pl.BlockSpec((None, N_MOD, d), _mod_row(n_ctx_tiles, ctx_row)),
                  pl.BlockSpec((1, d), lambda b, i: (0, 0)),
                  pl.BlockSpec((D_FF, d), lambda b, i: (0, 0))],
        out_specs=one,
        out_shape=jax.ShapeDtypeStruct((bsz, tt, d), F32),
        scratch_shapes=[pltpu.VMEM((tm + 2 * SUBLANES, f2), F32)],
        compiler_params=_cparams(("parallel", "parallel")),
        name="ffn_tail",
    )(a, z, z, z, conv_w, mod3, g.reshape(1, d), wd)


def _rope_tables(ctx_len, seq):
    rows = seq // GRID_W
    row = jnp.repeat(jnp.arange(rows, dtype=F32), GRID_W)
    col = jnp.tile(jnp.arange(GRID_W, dtype=F32), rows)
    inv_freq = ROPE_THETA ** (-jnp.arange(N_FREQ, dtype=F32) / N_FREQ)
    ang_r = row[:, None] * inv_freq
    ang_c = col[:, None] * inv_freq
    cos = jnp.concatenate([jnp.cos(ang_r), jnp.cos(ang_r), jnp.cos(ang_c), jnp.cos(ang_c)], axis=-1)
    sin = jnp.concatenate([-jnp.sin(ang_r), jnp.sin(ang_r), -jnp.sin(ang_c), jnp.sin(ang_c)], axis=-1)
    cos = jnp.concatenate([jnp.ones((ctx_len, HEAD_DIM), F32), cos], axis=0)
    sin = jnp.concatenate([jnp.zeros((ctx_len, HEAD_DIM), F32), sin], axis=0)
    return cos, sin


def _reorder_w_in(w):
    qkv = w[:, 0:1536]
    glu = w[:, 1536:3584]
    rkv = w[:, 3584:6656]
    lora = w[:, 6656:7040]
    gates = w[:, 7040:10112]
    pad = jnp.zeros((w.shape[0], LANES), w.dtype)
    return jnp.concatenate([qkv, lora, pad, glu, rkv, gates], axis=1).astype(BF16)


def _pad_lora_up(up):
    z = jnp.zeros_like(up[0])
    return jnp.stack([jnp.concatenate([up[0], z], axis=0), jnp.concatenate([z, up[1]], axis=0)])


def kernel(x, c, ctx, c_ctx, w_mod, b_mod, g_pre_mix, g_post_mix, g_pre_ffn, g_post_ffn, w_in, q_norm, k_norm,
           w_attn_o, conv_w, conv_b, conv_ln_g, conv_ln_b, w_conv_o, shift_w, decay_w0, decay_up, iclr_a0,
           iclr_up, gate_up, k_k, k_a, r_k, wkv_gn_g, wkv_gn_b, w_rwkv_o, w_out, w_ffn_up, ffn_conv_w, w_ffn_down):
    bsz, seq, d = x.shape
    ctx_len = ctx.shape[1]
    depth = w_mod.shape[0]
    tm = min(ROW_TILE, ctx_len)
    assert d == D_MODEL and ctx_len % tm == 0 and seq % tm == 0 and ctx_len % CHUNK == 0 and seq % CHUNK == 0
    n_ctx_tiles = ctx_len // tm

    a = jnp.concatenate([ctx, x], axis=1)
    mod_rows = -(-(bsz + 1) // SUBLANES) * SUBLANES
    cc = jnp.zeros((mod_rows, d), F32).at[:bsz].set(c).at[bsz].set(c_ctx)
    cos, sin = _rope_tables(ctx_len, seq)
    head = jnp.arange(d, dtype=jnp.int32) // RWKV_HEAD
    e = (head[:, None] == jnp.arange(LANES, dtype=jnp.int32)[None, :]).astype(BF16)
    et = e.T
    def seg_rows(segs, n_ctx):
        return jnp.array([bsz if s % segs < n_ctx else s // segs for s in range(bsz * segs)], jnp.int32)

    for l in range(depth):
        last = l == depth - 1
        mod3 = _mod_call(cc, w_mod[l], b_mod[l]).reshape(mod_rows, N_MOD, d)
        modseg = jnp.take(mod3, seg_rows((ctx_len + seq) // tm, n_ctx_tiles), axis=0)
        proj = _nm_matmul(a, modseg, g_pre_mix[l], _reorder_w_in(w_in[l]), shift_idx=0, tn=2048, seg=tm)
        q, k, v = _qk_prep(proj, cos, sin, q_norm[l], k_norm[l], tm=tm)
        att = _attention(q, k, v, ctx_len=ctx_len, tq=tm, skip_ctx=last)
        conv = _conformer(proj, conv_w[l], conv_b[l], conv_ln_g[l], conv_ln_b[l], ctx_len=ctx_len, tm=tm,
                          skip_ctx=False)
        r, vv, kap, lw, kd, bd = _rwkv_prep(proj, shift_w[l], k_k[l], k_a[l], decay_w0[l], _pad_lora_up(decay_up[l]),
                                            iclr_a0[l], _pad_lora_up(iclr_up[l]), e, et, ctx_len=ctx_len, tm=tm)
        y = _rwkv_scan(r, vv, kap, lw, kd, bd, ctx_len=ctx_len)
        a = _merge(a, att, conv, y, r, vv, kd, proj, mod3, g_post_mix[l], wkv_gn_g[l], wkv_gn_b[l], r_k[l],
                   gate_up[l].astype(BF16), e, et, w_attn_o[l].astype(BF16), w_conv_o[l].astype(BF16),
                   w_rwkv_o[l].astype(BF16), w_out[l].astype(BF16), n_ctx_tiles=n_ctx_tiles, ctx_row=bsz, tm=tm,
                   skip_ctx=last)
        n_ctx = 0 if last else n_ctx_tiles
        if last:
            modseg = jnp.take(mod3, seg_rows(seq // tm, 0), axis=0)
        z = _nm_matmul(a, modseg, g_pre_ffn[l], w_ffn_up[l].astype(BF16), shift_idx=3, tn=D_FF // 2, seg=tm)
        a = _ffn_tail(a, z, ffn_conv_w[l], mod3, g_post_ffn[l], w_ffn_down[l].astype(BF16),
                      n_ctx_tiles=n_ctx, ctx_row=bsz, tm=tm)
    return a
```

```python
import functools
import math

import jax
import jax.numpy as jnp
from jax import lax
from jax.experimental import pallas as pl
from jax.experimental.pallas import tpu as pltpu

F32 = jnp.float32
BF16 = jnp.bfloat16

D_MODEL = 1024
GRID_W = 64
N_Q_HEADS = 8
N_KV_HEADS = 2
GQA_GROUP = N_Q_HEADS // N_KV_HEADS
HEAD_DIM = 128
N_FREQ = HEAD_DIM // 4
ROPE_THETA = 10000.0
CONV_K = 31
CONV_HALO = 16
RWKV_HEAD = 64
RWKV_HEADS = D_MODEL // RWKV_HEAD
DECAY_SCALE = math.exp(-0.5)
D_FF = 2816
N_MOD = 6
EPS = 1e-6
LN_EPS = 1e-5
GN_EPS = RWKV_HEAD * 1e-5
SOFTMAX_C = (HEAD_DIM ** -0.5) * math.log2(math.e)
LANES = 128
SUBLANES = 8
CHUNK = 64
N_PAIR = D_MODEL // LANES
SCAN_BATCH = 2
INV_BASE = 4
ROW_TILE = 256
VMEM_LIMIT = 48 * 1024 * 1024

COL_Q = 0
COL_KV = 1024
COL_LORA = 1536
COL_GLU = 2048
COL_RKV = 4096
COL_GATE = 7168
N_IN_PAD = 10240


def _cparams(sem):
    return pltpu.CompilerParams(dimension_semantics=sem, vmem_limit_bytes=VMEM_LIMIT)


def _bdot(a, b):
    return jnp.dot(a.astype(BF16), b.astype(BF16), preferred_element_type=F32)


def _bdot_nt(a, b):
    return lax.dot_general(a.astype(BF16), b.astype(BF16), (((1,), (1,)), ((), ())),
                           preferred_element_type=F32)


def _bdot_tn(a, b):
    return lax.dot_general(a.astype(BF16), b.astype(BF16), (((0,), (0,)), ((), ())),
                           preferred_element_type=F32)


def _split2(x):
    hi = x.astype(BF16)
    lo = (x - hi.astype(F32)).astype(BF16)
    return hi, lo


def _dot3(a, b):
    ah, al = _split2(a)
    bh, bl = _split2(b)
    dot = functools.partial(jnp.dot, preferred_element_type=F32)
    return dot(ah, bh) + dot(al, bh) + dot(ah, bl)


def _sigmoid(x):
    return jax.nn.sigmoid(x)


def _head_sum(x, e_ref, et_ref):
    xh, xl = _split2(x)
    dot = functools.partial(jnp.dot, preferred_element_type=F32)
    s = dot(xh, e_ref[...]) + dot(xl, e_ref[...])
    sh, sl = _split2(s)
    return dot(sh, et_ref[...]) + dot(sl, et_ref[...])


def _mod_kernel(c_ref, w_ref, b_ref, o_ref):
    c = c_ref[...]
    o_ref[...] = _dot3(c * _sigmoid(c), w_ref[...]) + b_ref[...]


def _mod_call(cc, w_mod, b_mod):
    rows = cc.shape[0]
    n = w_mod.shape[1]
    tn = 1536
    return pl.pallas_call(
        _mod_kernel,
        grid=(n // tn,),
        in_specs=[pl.BlockSpec((rows, D_MODEL), lambda j: (0, 0)),
                  pl.BlockSpec((D_MODEL, tn), lambda j: (0, j)),
                  pl.BlockSpec((1, tn), lambda j: (0, j))],
        out_specs=pl.BlockSpec((rows, tn), lambda j: (0, j)),
        out_shape=jax.ShapeDtypeStruct((rows, n), F32),
        compiler_params=_cparams(("parallel",)),
        name="mod",
    )(cc, w_mod, b_mod.reshape(1, n))


def _mod_row(n_ctx_tiles, ctx_row):
    return lambda b, i, *_: (jnp.where(i < n_ctx_tiles, ctx_row, b), 0, 0)


def _nm_matmul_kernel(a_ref, mod_ref, g_ref, w_ref, o_ref, h_ref, *, shift_idx, seg, nseg):
    @pl.when(pl.program_id(1) == 0)
    def _():
        for s in range(nseg):
            rows = slice(s * seg, (s + 1) * seg)
            x = a_ref[rows, :]
            y = x * lax.rsqrt(jnp.mean(x * x, axis=-1, keepdims=True) + EPS) * g_ref[...]
            h = y * (1.0 + mod_ref[s, shift_idx + 1:shift_idx + 2, :]) + mod_ref[s, shift_idx:shift_idx + 1, :]
            h_ref[rows, :] = h.astype(BF16)

    o_ref[...] = jnp.dot(h_ref[...], w_ref[...], preferred_element_type=F32).astype(o_ref.dtype)


def _nm_matmul(a, modseg, g, w, *, shift_idx, tn, seg):
    bsz, tt, d = a.shape
    n = w.shape[1]
    rows = bsz * tt
    nseg = max(s for s in (4, 2, 1) if (rows // seg) % s == 0)
    tm = seg * nseg
    out = pl.pallas_call(
        functools.partial(_nm_matmul_kernel, shift_idx=shift_idx, seg=seg, nseg=nseg),
        grid=(rows // tm, n // tn),
        in_specs=[pl.BlockSpec((tm, d), lambda i, j: (i, 0)),
                  pl.BlockSpec((nseg, N_MOD, d), lambda i, j: (i, 0, 0)),
                  pl.BlockSpec((1, d), lambda i, j: (0, 0)),
                  pl.BlockSpec((d, tn), lambda i, j: (0, j))],
        out_specs=pl.BlockSpec((tm, tn), lambda i, j: (i, j)),
        out_shape=jax.ShapeDtypeStruct((rows, n), F32),
        scratch_shapes=[pltpu.VMEM((tm, d), BF16)],
        compiler_params=_cparams(("parallel", "arbitrary")),
        name="norm_mod_matmul",
    )(a.reshape(rows, d), modseg, g.reshape(1, d), w)
    return out.reshape(bsz, tt, n)


def _qk_prep_kernel(q_ref, kv_ref, cos_ref, sin_ref, qn_ref, kn_ref, qo_ref, ko_ref, vo_ref):
    cos = cos_ref[...]
    sin = sin_ref[...]
    lane = lax.broadcasted_iota(jnp.int32, cos.shape, 1)
    first = (lane & (N_FREQ)) == 0

    def norm_rope(x, g):
        y = x * lax.rsqrt(jnp.mean(x * x, axis=-1, keepdims=True) + EPS) * g
        partner = jnp.where(first, pltpu.roll(y, LANES - N_FREQ, 1), pltpu.roll(y, N_FREQ, 1))
        return y * cos + partner * sin

    for h in range(N_Q_HEADS):
        sl = slice(h * HEAD_DIM, (h + 1) * HEAD_DIM)
        qo_ref[:, sl] = (norm_rope(q_ref[:, sl], qn_ref[...]) * SOFTMAX_C).astype(BF16)
    for h in range(N_KV_HEADS):
        sl = slice(h * HEAD_DIM, (h + 1) * HEAD_DIM)
        ko_ref[:, sl] = norm_rope(kv_ref[:, sl], kn_ref[...]).astype(BF16)
    vo_ref[...] = kv_ref[:, N_KV_HEADS * HEAD_DIM:].astype(BF16)


def _qk_prep(proj, cos, sin, q_norm, k_norm, *, tm):
    bsz, tt, _ = proj.shape
    dq = N_Q_HEADS * HEAD_DIM
    dkv = N_KV_HEADS * HEAD_DIM
    return pl.pallas_call(
        _qk_prep_kernel,
        grid=(bsz, tt // tm),
        in_specs=[pl.BlockSpec((None, tm, dq), lambda b, i: (b, i, COL_Q // dq)),
                  pl.BlockSpec((None, tm, 2 * dkv), lambda b, i: (b, i, COL_KV // (2 * dkv))),
                  pl.BlockSpec((tm, HEAD_DIM), lambda b, i: (i, 0)),
                  pl.BlockSpec((tm, HEAD_DIM), lambda b, i: (i, 0)),
                  pl.BlockSpec((1, HEAD_DIM), lambda b, i: (0, 0)),
                  pl.BlockSpec((1, HEAD_DIM), lambda b, i: (0, 0))],
        out_specs=[pl.BlockSpec((None, tm, dq), lambda b, i: (b, i, 0)),
                   pl.BlockSpec((None, tm, dkv), lambda b, i: (b, i, 0)),
                   pl.BlockSpec((None, tm, dkv), lambda b, i: (b, i, 0))],
        out_shape=[jax.ShapeDtypeStruct((bsz, tt, dq), BF16),
                   jax.ShapeDtypeStruct((bsz, tt, dkv), BF16),
                   jax.ShapeDtypeStruct((bsz, tt, dkv), BF16)],
        compiler_params=_cparams(("parallel", "parallel")),
        name="qk_prep",
    )(proj, proj, cos, sin, q_norm.reshape(1, HEAD_DIM), k_norm.reshape(1, HEAD_DIM))


def _attn_kernel(q_ref, k_ref, v_ref, o_ref, *, n_ctx_tiles, ctx_len, skip_ctx):
    tq = q_ref.shape[0]
    tt = k_ref.shape[0]
    half = (tt - ctx_len) // 2

    def attend(spans):
        q = jnp.concatenate([q_ref[:, g * HEAD_DIM:(g + 1) * HEAD_DIM] for g in range(GQA_GROUP)], axis=0)
        m = l = acc = None
        for start, size in spans:
            k = k_ref[start:start + size, :]
            v = v_ref[start:start + size, :]
            s = lax.dot_general(q, k, (((1,), (1,)), ((), ())), preferred_element_type=F32)
            mt = jnp.max(s, axis=-1, keepdims=True)
            if m is None:
                m = mt
                p = jnp.exp2(s - m)
                l = jnp.sum(p, axis=-1, keepdims=True)
                acc = jnp.dot(p.astype(BF16), v, preferred_element_type=F32)
            else:
                m_new = jnp.maximum(m, mt)
                alpha = jnp.exp2(m - m_new)
                p = jnp.exp2(s - m_new)
                l = alpha * l + jnp.sum(p, axis=-1, keepdims=True)
                acc = alpha * acc + jnp.dot(p.astype(BF16), v, preferred_element_type=F32)
                m = m_new
        o = acc / l
        for g in range(GQA_GROUP):
            o_ref[:, g * HEAD_DIM:(g + 1) * HEAD_DIM] = o[g * tq:(g + 1) * tq].astype(o_ref.dtype)

    x_spans = [(0, ctx_len), (ctx_len, half), (ctx_len + half, half)]
    if skip_ctx:
        attend(x_spans)
        return
    i = pl.program_id(2)

    @pl.when(i < n_ctx_tiles)
    def _():
        attend([(0, ctx_len)])

    @pl.when(i >= n_ctx_tiles)
    def _():
        attend(x_spans)


def _attention(q, k, v, *, ctx_len, tq, skip_ctx):
    bsz, tt, dq = q.shape
    gw = GQA_GROUP * HEAD_DIM
    off = ctx_len // tq if skip_ctx else 0
    return pl.pallas_call(
        functools.partial(_attn_kernel, n_ctx_tiles=ctx_len // tq, ctx_len=ctx_len, skip_ctx=skip_ctx),
        grid=(bsz, N_KV_HEADS, tt // tq - off),
        in_specs=[pl.BlockSpec((None, tq, gw), lambda b, h, i: (b, i + off, h)),
                  pl.BlockSpec((None, tt, HEAD_DIM), lambda b, h, i: (b, 0, h)),
                  pl.BlockSpec((None, tt, HEAD_DIM), lambda b, h, i: (b, 0, h))],
        out_specs=pl.BlockSpec((None, tq, gw), lambda b, h, i: (b, i, h)),
        out_shape=jax.ShapeDtypeStruct((bsz, tt - off * tq, dq), BF16),
        compiler_params=_cparams(("parallel", "parallel", "arbitrary")),
        name="attention",
    )(q, k, v)


def _halo_specs(tm, halo, tt, col_block, width, off=0):
    per = tm // halo
    last = tt // halo - 1
    prev = pl.BlockSpec((None, halo, width), lambda b, i: (b, jnp.maximum((i + off) * per - 1, 0), col_block))
    nxt = pl.BlockSpec((None, halo, width), lambda b, i: (b, jnp.minimum((i + off + 1) * per, last), col_block))
    return prev, nxt


def _edge_flags(i, n_ctx_tiles, n_tiles):
    first = jnp.logical_or(i == 0, i == n_ctx_tiles)
    last = jnp.logical_or(i == n_ctx_tiles - 1, i == n_tiles - 1)
    return first, last


def _conformer_kernel(ac_ref, bc_ref, ap_ref, bp_ref, an_ref, bn_ref, w_ref, cb_ref, lg_ref, lb_ref,
                      o_ref, buf_ref, sh_ref, *, n_ctx_tiles, n_tiles, tm, rc, tile0):
    first, last = _edge_flags(pl.program_id(1) + tile0, n_ctx_tiles, n_tiles)
    h = CONV_HALO
    buf_ref[0:h, :] = jnp.where(first, 0.0, ap_ref[...] * _sigmoid(bp_ref[...]))
    buf_ref[h:h + tm, :] = ac_ref[...] * _sigmoid(bc_ref[...])
    buf_ref[h + tm:h + tm + h, :] = jnp.where(last, 0.0, an_ref[...] * _sigmoid(bn_ref[...]))
    n_sh = sh_ref.shape[1]
    for s in range(1, SUBLANES):
        sh_ref[s - 1] = buf_ref[s:s + n_sh, :]
    off = h - CONV_K // 2
    for c in range(tm // rc):
        acc = jnp.zeros((rc, D_MODEL), F32)
        for j in range(CONV_K):
            s = (j + off) % SUBLANES
            r0 = c * rc + (j + off) - s
            tap = buf_ref[r0:r0 + rc, :] if s == 0 else sh_ref[s - 1, r0:r0 + rc, :]
            acc = acc + tap * jnp.concatenate([w_ref[j]] * (rc // SUBLANES), axis=0)
        y = acc + cb_ref[...]
        mu = jnp.mean(y, axis=-1, keepdims=True)
        dlt = y - mu
        var = jnp.mean(dlt * dlt, axis=-1, keepdims=True)
        z = dlt * lax.rsqrt(var + LN_EPS) * lg_ref[...] + lb_ref[...]
        o_ref[c * rc:(c + 1) * rc, :] = (z * _sigmoid(z)).astype(o_ref.dtype)


def _conformer(proj, conv_w, conv_b, ln_g, ln_b, *, ctx_len, tm, skip_ctx):
    bsz, tt, _ = proj.shape
    d = D_MODEL
    ca = COL_GLU // d
    off = ctx_len // tm if skip_ctx else 0
    prev_a, next_a = _halo_specs(tm, CONV_HALO, tt, ca, d, off)
    prev_b, next_b = _halo_specs(tm, CONV_HALO, tt, ca + 1, d, off)
    vec = lambda: pl.BlockSpec((1, d), lambda b, i: (0, 0))
    return pl.pallas_call(
        functools.partial(_conformer_kernel, n_ctx_tiles=ctx_len // tm, n_tiles=tt // tm, tm=tm, rc=32, tile0=off),
        grid=(bsz, tt // tm - off),
        in_specs=[pl.BlockSpec((None, tm, d), lambda b, i: (b, i + off, ca)),
                  pl.BlockSpec((None, tm, d), lambda b, i: (b, i + off, ca + 1)),
                  prev_a, prev_b, next_a, next_b,
                  pl.BlockSpec((CONV_K, SUBLANES, d), lambda b, i: (0, 0, 0)),
                  vec(), vec(), vec()],
        out_specs=pl.BlockSpec((None, tm, d), lambda b, i: (b, i, 0)),
        out_shape=jax.ShapeDtypeStruct((bsz, tt - off * tm, d), BF16),
        scratch_shapes=[pltpu.VMEM((tm + 2 * CONV_HALO, d), F32),
                        pltpu.VMEM((SUBLANES - 1, tm + 2 * CONV_HALO - SUBLANES, d), F32)],
        compiler_params=_cparams(("parallel", "parallel")),
        name="conformer",
    )(proj, proj, proj, proj, proj, proj, jnp.broadcast_to(conv_w[:, None, :], (CONV_K, SUBLANES, d)),
      conv_b.reshape(1, d), ln_g.reshape(1, d), ln_b.reshape(1, d))


def _rwkv_prep_kernel(rc_ref, kc_ref, vc_ref, rp_ref, kp_ref, vp_ref, rn_ref, kn_ref, vn_ref, lora_ref,
                      sw_ref, kk_ref, ka_ref, w0_ref, dup_ref, a0_ref, iup_ref, e_ref, et_ref,
                      r_ref, v_ref, kap_ref, lw_ref, kd_ref, bd_ref, buf_ref,
                      *, n_ctx_tiles, n_tiles, tm):
    first, last = _edge_flags(pl.program_id(1), n_ctx_tiles, n_tiles)
    h = SUBLANES
    d = D_MODEL

    def shift(cur_ref, prev_ref, next_ref, col):
        buf_ref[0:h, :] = jnp.where(first, 0.0, prev_ref[...])
        buf_ref[h:h + tm, :] = cur_ref[...]
        buf_ref[h + tm:h + tm + h, :] = jnp.where(last, 0.0, next_ref[...])
        sl = slice(col * d, (col + 1) * d)
        return (buf_ref[h - 1:h - 1 + tm, :] * sw_ref[0:1, sl] + buf_ref[h:h + tm, :] * sw_ref[1:2, sl]
                + buf_ref[h + 1:h + 1 + tm, :] * sw_ref[2:3, sl])

    r = shift(rc_ref, rp_ref, rn_ref, 0)
    k = shift(kc_ref, kp_ref, kn_ref, 1)
    v = shift(vc_ref, vp_ref, vn_ref, 2)
    r_ref[...] = r
    v_ref[...] = v
    kk = k * kk_ref[...]
    ss = _head_sum(kk * kk, e_ref, et_ref)
    kap = kk * lax.rsqrt(jnp.maximum(ss, 1e-12))
    kap_ref[...] = kap
    tw = jnp.tanh(lora_ref[:, 0:LANES])
    la = lora_ref[:, LANES:2 * LANES]
    for dr in range(2):
        z = w0_ref[dr:dr + 1, :] + _dot3(tw, dup_ref[dr])
        lw_ref[dr] = -DECAY_SCALE * _sigmoid(z)
        a = _sigmoid(a0_ref[dr:dr + 1, :] + _dot3(la, iup_ref[dr]))
        kd_ref[dr] = k * (1.0 + (a - 1.0) * ka_ref[...])
        bd_ref[dr] = a * kap


def _rwkv_prep(proj, shift_w, k_k, k_a, decay_w0, decay_up_pad, iclr_a0, iclr_up_pad, e, et, *, ctx_len, tm):
    bsz, tt, _ = proj.shape
    d = D_MODEL
    c0 = COL_RKV // d
    cur = lambda c: pl.BlockSpec((None, tm, d), lambda b, i: (b, i, c))
    halos = [_halo_specs(tm, SUBLANES, tt, c0 + c, d) for c in range(3)]
    full = lambda shape: pl.BlockSpec(shape, lambda b, i: (0,) * len(shape))
    out1 = pl.BlockSpec((None, tm, d), lambda b, i: (b, i, 0))
    out2 = pl.BlockSpec((2, None, tm, d), lambda b, i: (0, b, i, 0))
    s1 = jax.ShapeDtypeStruct((bsz, tt, d), F32)
    s2 = jax.ShapeDtypeStruct((2, bsz, tt, d), F32)
    return pl.pallas_call(
        functools.partial(_rwkv_prep_kernel, n_ctx_tiles=ctx_len // tm, n_tiles=tt // tm, tm=tm),
        grid=(bsz, tt // tm),
        in_specs=[cur(c0), cur(c0 + 1), cur(c0 + 2),
                  halos[0][0], halos[1][0], halos[2][0], halos[0][1], halos[1][1], halos[2][1],
                  pl.BlockSpec((None, tm, 512), lambda b, i: (b, i, COL_LORA // 512)),
                  full((3, 3 * d)), full((1, d)), full((1, d)), full((2, d)), full((2, LANES, d)),
                  full((2, d)), full((2, LANES, d)), full((d, LANES)), full((LANES, d))],
        out_specs=[out1, out1, out1, out2, out2, out2],
        out_shape=[s1, s1, s1, s2, s2, s2],
        scratch_shapes=[pltpu.VMEM((tm + 2 * SUBLANES, d), F32)],
        compiler_params=_cparams(("parallel", "parallel")),
        name="rwkv_prep",
    )(proj, proj, proj, proj, proj, proj, proj, proj, proj, proj,
      shift_w, k_k.reshape(1, d), k_a.reshape(1, d), decay_w0, decay_up_pad, iclr_a0, iclr_up_pad, e, et)


def _scan_kernel(r_ref, v_ref, kap_ref, lw_ref, k_ref, b_ref, y_ref, h_ref):
    c = CHUNK
    c2 = 2 * c
    sgn = 1 - 2 * pl.program_id(1)

    @pl.when(pl.program_id(2) == 0)
    def _():
        h_ref[...] = jnp.zeros_like(h_ref)

    row = lax.broadcasted_iota(jnp.int32, (c, c), 0)
    col = lax.broadcasted_iota(jnp.int32, (c, c), 1)
    incl = jnp.where((col - row) * sgn <= 0, 1.0, 0.0).astype(BF16)
    dot = functools.partial(jnp.dot, preferred_element_type=F32)
    nb = lw_ref.shape[0]
    cum = []
    for bi in range(nb):
        lw = lw_ref[bi]
        hi = lw.astype(BF16)
        rem = lw - hi.astype(F32)
        mid = rem.astype(BF16)
        lo = (rem - mid.astype(F32)).astype(BF16)
        cum.append(dot(incl, hi) + dot(incl, mid) + dot(incl, lo))

    prow = lax.broadcasted_iota(jnp.int32, (c, c2), 0)
    pcol = lax.broadcasted_iota(jnp.int32, (c, c2), 1)
    order = ((pcol & (c - 1)) - prow) * sgn
    strict = order < 0
    upto = order <= 0
    eye = order == 0
    head0 = pcol < RWKV_HEAD

    def stack(x):
        return jnp.concatenate([jnp.where(head0, x, 0.0), jnp.where(head0, 0.0, x)], axis=0)

    def pack(x):
        return jnp.where(head0, x[0:c], x[c:c2])

    units = [(bi, slice(p * LANES, (p + 1) * LANES)) for bi in range(nb) for p in range(N_PAIR)]
    pairs = range(len(units))
    kt, rt, vs, kend, bend, ptot, a = [], [], [], [], [], [], []
    for bi, sl in units:
        cum_p = cum[bi][:, sl]
        lw_p = lw_ref[bi, :, sl]
        tot_p = jnp.sum(lw_p, axis=0, keepdims=True)
        p_inv = jnp.exp(-cum_p)
        p_end = jnp.exp(tot_p - cum_p)
        k = k_ref[bi, :, sl]
        b = b_ref[bi, :, sl]
        kt.append(kap_ref[bi, :, sl] * jnp.exp(cum_p - lw_p))
        rt.append(r_ref[bi, :, sl] * jnp.exp(cum_p))
        vs.append(v_ref[bi, :, sl])
        kend.append(k * p_end)
        bend.append(b * p_end)
        ptot.append(jnp.exp(tot_p))
        a.append(_bdot_nt(jnp.concatenate([kt[-1], rt[-1]], axis=0),
                          jnp.concatenate([stack(b * p_inv), stack(k * p_inv)], axis=0)))
    a_ab = [jnp.where(strict, a[p][0:c, 0:c2], 0.0) for p in pairs]
    a_ak = [jnp.where(strict, a[p][0:c, c2:2 * c2], 0.0) for p in pairs]
    a_rb = [jnp.where(upto, a[p][c:c2, 0:c2], 0.0) for p in pairs]
    a_rk = [jnp.where(upto, a[p][c:c2, c2:2 * c2], 0.0) for p in pairs]

    blk = (pcol & (c - 1)) ^ prow
    ident = jnp.where(eye, 1.0, 0.0)
    l0 = [jnp.where(blk < INV_BASE, a_ab[p], 0.0) for p in pairs]
    sq = [_bdot(l0[p], stack(l0[p])) for p in pairs]
    t = [ident - l0[p] for p in pairs]
    t = [t[p] + _bdot(t[p], stack(sq[p])) for p in pairs]
    s_blk = INV_BASE
    while s_blk < c:
        off = jnp.logical_and(blk >= s_blk, blk < 2 * s_blk)
        tl = [_bdot(t[p], stack(jnp.where(off, a_ab[p], 0.0))) for p in pairs]
        t = [t[p] - _bdot(tl[p], stack(t[p])) for p in pairs]
        s_blk *= 2

    asv = [_bdot(jnp.concatenate([a_ak[p], a_rk[p]], axis=0), stack(vs[p])) for p in pairs]
    wu = [_bdot(t[p], jnp.concatenate([stack(kt[p]), stack(asv[p][0:c])], axis=1)) for p in pairs]
    w = [wu[p][:, 0:c2] for p in pairs]
    u0 = [wu[p][:, c2:2 * c2] for p in pairs]
    arb = [_bdot(a_rb[p], jnp.concatenate([stack(w[p]), stack(u0[p])], axis=1)) for p in pairs]
    y0 = [asv[p][c:c2] - arb[p][:, c2:2 * c2] for p in pairs]
    y1 = [rt[p] - arb[p][:, 0:c2] for p in pairs]
    m = [jnp.where(eye, ptot[p], 0.0) - pack(_bdot_tn(bend[p], w[p])) for p in pairs]
    nn = [pack(_bdot_tn(jnp.concatenate([kend[p], -bend[p]], axis=0), jnp.concatenate([vs[p], u0[p]], axis=0)))
          for p in pairs]
    for p in pairs:
        h0 = h_ref[p]
        hh = h0.astype(BF16)
        hl = h0 - hh.astype(F32)
        mh, ml = _split2(m[p])
        top = _bdot(jnp.concatenate([mh, ml, y1[p].astype(BF16)], axis=0), stack(hh.astype(F32)))
        y_ref[units[p][0], :, units[p][1]] = top[c2:c2 + c] + y0[p]
        h_ref[p] = top[0:c] + top[c:c2] + _bdot(mh, stack(hl)) + nn[p]


def _rwkv_scan(r, v, kap, lw, kd, bd, *, ctx_len):
    bsz, tt, d = r.shape
    nc = tt // CHUNK
    ncc = ctx_len // CHUNK

    def chunk(dr, s):
        return jnp.where(dr == 0, s, jnp.where(s < ncc, ncc - 1 - s, nc + ncc - 1 - s))

    ub = SCAN_BATCH if bsz % SCAN_BATCH == 0 else 1
    shared = pl.BlockSpec((ub, CHUNK, d), lambda b, dr, s: (b, chunk(dr, s), 0))
    per_dir = pl.BlockSpec((None, ub, CHUNK, d), lambda b, dr, s: (dr, b, chunk(dr, s), 0))
    return pl.pallas_call(
        _scan_kernel,
        grid=(bsz // ub, 2, nc),
        in_specs=[shared, shared, shared, per_dir, per_dir, per_dir],
        out_specs=per_dir,
        out_shape=jax.ShapeDtypeStruct((2, bsz, tt, d), F32),
        scratch_shapes=[pltpu.VMEM((ub * N_PAIR, CHUNK, LANES), F32)],
        compiler_params=_cparams(("parallel", "parallel", "arbitrary")),
        name="rwkv_scan",
    )(r, v, kap, lw, kd, bd)


def _merge_kernel(a_ref, att_ref, conv_ref, y_ref, r_ref, v_ref, kd_ref, lora_ref, ga_ref, gc_ref, gr_ref,
                  mod_ref, g_ref, gg_ref, gb_ref, rk_ref, gup_ref, e_ref, et_ref,
                  wa_ref, wc_ref, wr_ref, wo_ref, o_ref):
    dot = functools.partial(jnp.dot, preferred_element_type=F32)
    inv = 1.0 / RWKV_HEAD
    y = y_ref[0] + y_ref[1]
    mu = _head_sum(y, e_ref, et_ref) * inv
    dlt = y - mu
    var = _head_sum(dlt * dlt, e_ref, et_ref) * inv
    yn = dlt * lax.rsqrt(var + GN_EPS) * gg_ref[...] + gb_ref[...]
    bonus = _head_sum(r_ref[...] * (kd_ref[0] + kd_ref[1]) * rk_ref[...], e_ref, et_ref) * v_ref[...]
    gate = _bdot(_sigmoid(lora_ref[:, 2 * LANES:3 * LANES]), gup_ref[...])
    rw = ((yn + bonus) * gate).astype(BF16)
    m = (_sigmoid(ga_ref[...]) * dot(att_ref[...], wa_ref[...])
         + _sigmoid(gc_ref[...]) * dot(conv_ref[...], wc_ref[...])
         + _sigmoid(gr_ref[...]) * dot(rw, wr_ref[...]))
    z = dot(m.astype(BF16), wo_ref[...])
    zn = z * lax.rsqrt(jnp.mean(z * z, axis=-1, keepdims=True) + EPS) * g_ref[...]
    o_ref[...] = a_ref[...] + mod_ref[2:3, :] * zn


def _merge(a, att, conv, y, r, v, kd, proj, mod3, g, gn_g, gn_b, r_k, gate_up, e, et, wa, wc, wr, wo,
           *, n_ctx_tiles, ctx_row, tm, skip_ctx):
    bsz, tt, d = a.shape
    cg = COL_GATE // d
    off = n_ctx_tiles if skip_ctx else 0
    loc = pl.BlockSpec((None, tm, d), lambda b, i: (b, i, 0))
    one = pl.BlockSpec((None, tm, d), lambda b, i: (b, i + off, 0))
    two = pl.BlockSpec((2, None, tm, d), lambda b, i: (0, b, i + off, 0))
    gate = lambda c: pl.BlockSpec((None, tm, d), lambda b, i: (b, i + off, cg + c))
    full = lambda shape: pl.BlockSpec(shape, lambda b, i: (0,) * len(shape), pipeline_mode=pl.Buffered(1))
    return pl.pallas_call(
        _merge_kernel,
        grid=(bsz, tt // tm - off),
        in_specs=[one, loc, loc, two, one, one, two,
                  pl.BlockSpec((None, tm, 512), lambda b, i: (b, i + off, COL_LORA // 512)),
                  gate(0), gate(1), gate(2),
                  pl.BlockSpec((None, N_MOD, d), _mod_row(n_ctx_tiles - off, ctx_row)),
                  full((1, d)), full((1, d)), full((1, d)), full((1, d)), full((LANES, d)),
                  full((d, LANES)), full((LANES, d)),
                  full((d, d)), full((d, d)), full((d, d)), full((d, d))],
        out_specs=loc,
        out_shape=jax.ShapeDtypeStruct((bsz, tt - off * tm, d), F32),
        compiler_params=_cparams(("parallel", "parallel")),
        name="merge",
    )(a, att, conv, y, r, v, kd, proj, proj, proj, proj, mod3, g.reshape(1, d), gn_g.reshape(1, d),
      gn_b.reshape(1, d), r_k.reshape(1, d), gate_up, e, et, wa, wc, wr, wo)


def _ffn_tail_kernel(a_ref, zc_ref, zp_ref, zn_ref, cw_ref, mod_ref, g_ref, wd_ref, o_ref, buf_ref,
                     *, n_ctx_tiles, n_tiles, tm):
    first, last = _edge_flags(pl.program_id(1), n_ctx_tiles, n_tiles)
    h = SUBLANES
    buf_ref[0:h, :] = jnp.where(first, 0.0, zp_ref[...])
    buf_ref[h:h + tm, :] = zc_ref[...]
    buf_ref[h + tm:h + tm + h, :] = jnp.where(last, 0.0, zn_ref[...])

    def conv(sl):
        return (buf_ref[h - 1:h - 1 + tm, sl] * cw_ref[0:1, sl] + buf_ref[h:h + tm, sl] * cw_ref[1:2, sl]
                + buf_ref[h + 1:h + 1 + tm, sl] * cw_ref[2:3, sl])

    gate = conv(slice(0, D_FF))
    val = conv(slice(D_FF, 2 * D_FF))
    u = (gate * _sigmoid(gate) * val).astype(BF16)
    z = jnp.dot(u, wd_ref[...], preferred_element_type=F32)
    zn = z * lax.rsqrt(jnp.mean(z * z, axis=-1, keepdims=True) + EPS) * g_ref[...]
    o_ref[...] = a_ref[...] + mod_ref[5:6, :] * zn


def _ffn_tail(a, z, conv_w, mod3, g, wd, *, n_ctx_tiles, ctx_row, tm):
    bsz, tt, d = a.shape
    f2 = 2 * D_FF
    prev, nxt = _halo_specs(tm, SUBLANES, tt, 0, f2)
    one = pl.BlockSpec((None, tm, d), lambda b, i: (b, i, 0))
    return pl.pallas_call(
        functools.partial(_ffn_tail_kernel, n_ctx_tiles=n_ctx_tiles, n_tiles=tt // tm, tm=tm),
        grid=(bsz, tt // tm),
        in_specs=[one, pl.BlockSpec((None, tm, f2), lambda b, i: (b, i, 0)), prev, nxt,
                  pl.BlockSpec((3, f2), lambda b, i: (0, 0)),
                  pl.BlockSpec((None, N_MOD, d), _mod_row(n_ctx_tiles, ctx_row)),
                  pl.BlockSpec((1, d), lambda b, i: (0, 0)),
                  pl.BlockSpec((D_FF, d), lambda b, i: (0, 0))],
        out_specs=one,
        out_shape=jax.ShapeDtypeStruct((bsz, tt, d), F32),
        scratch_shapes=[pltpu.VMEM((tm + 2 * SUBLANES, f2), F32)],
        compiler_params=_cparams(("parallel", "parallel")),
        name="ffn_tail",
    )(a, z, z, z, conv_w, mod3, g.reshape(1, d), wd)


def _rope_tables(ctx_len, seq):
    rows = seq // GRID_W
    row = jnp.repeat(jnp.arange(rows, dtype=F32), GRID_W)
    col = jnp.tile(jnp.arange(GRID_W, dtype=F32), rows)
    inv_freq = ROPE_THETA ** (-jnp.arange(N_FREQ, dtype=F32) / N_FREQ)
    ang_r = row[:, None] * inv_freq
    ang_c = col[:, None] * inv_freq
    cos = jnp.concatenate([jnp.cos(ang_r), jnp.cos(ang_r), jnp.cos(ang_c), jnp.cos(ang_c)], axis=-1)
    sin = jnp.concatenate([-jnp.sin(ang_r), jnp.sin(ang_r), -jnp.sin(ang_c), jnp.sin(ang_c)], axis=-1)
    cos = jnp.concatenate([jnp.ones((ctx_len, HEAD_DIM), F32), cos], axis=0)
    sin = jnp.concatenate([jnp.zeros((ctx_len, HEAD_DIM), F32), sin], axis=0)
    return cos, sin


def _reorder_w_in(w):
    qkv = w[:, 0:1536]
    glu = w[:, 1536:3584]
    rkv = w[:, 3584:6656]
    lora = w[:, 6656:7040]
    gates = w[:, 7040:10112]
    pad = jnp.zeros((w.shape[0], LANES), w.dtype)
    return jnp.concatenate([qkv, lora, pad, glu, rkv, gates], axis=1).astype(BF16)


def _pad_lora_up(up):
    z = jnp.zeros_like(up[0])
    return jnp.stack([jnp.concatenate([up[0], z], axis=0), jnp.concatenate([z, up[1]], axis=0)])


def kernel(x, c, ctx, c_ctx, w_mod, b_mod, g_pre_mix, g_post_mix, g_pre_ffn, g_post_ffn, w_in, q_norm, k_norm,
           w_attn_o, conv_w, conv_b, conv_ln_g, conv_ln_b, w_conv_o, shift_w, decay_w0, decay_up, iclr_a0,
           iclr_up, gate_up, k_k, k_a, r_k, wkv_gn_g, wkv_gn_b, w_rwkv_o, w_out, w_ffn_up, ffn_conv_w, w_ffn_down):
    bsz, seq, d = x.shape
    ctx_len = ctx.shape[1]
    depth = w_mod.shape[0]
    tm = min(ROW_TILE, ctx_len)
    assert d == D_MODEL and ctx_len % tm == 0 and seq % tm == 0 and ctx_len % CHUNK == 0 and seq % CHUNK == 0
    n_ctx_tiles = ctx_len // tm

    a = jnp.concatenate([ctx, x], axis=1)
    mod_rows = -(-(bsz + 1) // SUBLANES) * SUBLANES
    cc = jnp.zeros((mod_rows, d), F32).at[:bsz].set(c).at[bsz].set(c_ctx)
    cos, sin = _rope_tables(ctx_len, seq)
    head = jnp.arange(d, dtype=jnp.int32) // RWKV_HEAD
    e = (head[:, None] == jnp.arange(LANES, dtype=jnp.int32)[None, :]).astype(BF16)
    et = e.T
    def seg_rows(segs, n_ctx):
        return jnp.array([bsz if s % segs < n_ctx else s // segs for s in range(bsz * segs)], jnp.int32)

    for l in range(depth):
        last = l == depth - 1
        mod3 = _mod_call(cc, w_mod[l], b_mod[l]).reshape(mod_rows, N_MOD, d)
        modseg = jnp.take(mod3, seg_rows((ctx_len + seq) // tm, n_ctx_tiles), axis=0)
        proj = _nm_matmul(a, modseg, g_pre_mix[l], _reorder_w_in(w_in[l]), shift_idx=0, tn=2048, seg=tm)
        q, k, v = _qk_prep(proj, cos, sin, q_norm[l], k_norm[l], tm=tm)
        att = _attention(q, k, v, ctx_len=ctx_len, tq=tm, skip_ctx=last)
        conv = _conformer(proj, conv_w[l], conv_b[l], conv_ln_g[l], conv_ln_b[l], ctx_len=ctx_len, tm=tm,
                          skip_ctx=last)
        r, vv, kap, lw, kd, bd = _rwkv_prep(proj, shift_w[l], k_k[l], k_a[l], decay_w0[l], _pad_lora_up(decay_up[l]),
                                            iclr_a0[l], _pad_lora_up(iclr_up[l]), e, et, ctx_len=ctx_len, tm=tm)
        y = _rwkv_scan(r, vv, kap, lw, kd, bd, ctx_len=ctx_len)
        a = _merge(a, att, conv, y, r, vv, kd, proj, mod3, g_post_mix[l], wkv_gn_g[l], wkv_gn_b[l], r_k[l],
                   gate_up[l].astype(BF16), e, et, w_attn_o[l].astype(BF16), w_conv_o[l].astype(BF16),
                   w_rwkv_o[l].astype(BF16), w_out[l].astype(BF16), n_ctx_tiles=n_ctx_tiles, ctx_row=bsz, tm=tm,
                   skip_ctx=last)
        n_ctx = 0 if last else n_ctx_tiles
        if last:
            modseg = jnp.take(mod3, seg_rows(seq // tm, 0), axis=0)
        z = _nm_matmul(a, modseg, g_pre_ffn[l], w_ffn_up[l].astype(BF16), shift_idx=3, tn=D_FF // 2, seg=tm)
        a = _ffn_tail(a, z, ffn_conv_w[l], mod3, g_post_ffn[l], w_ffn_down[l].astype(BF16),
                      n_ctx_tiles=n_ctx, ctx_row=bsz, tm=tm)
    return a
```

```python
import functools
import math

import jax
import jax.numpy as jnp
from jax import lax
from jax.experimental import pallas as pl
from jax.experimental.pallas import tpu as pltpu

F32 = jnp.float32
BF16 = jnp.bfloat16

D_MODEL = 1024
GRID_W = 64
N_Q_HEADS = 8
N_KV_HEADS = 2
GQA_GROUP = N_Q_HEADS // N_KV_HEADS
HEAD_DIM = 128
N_FREQ = HEAD_DIM // 4
ROPE_THETA = 10000.0
CONV_K = 31
CONV_HALO = 16
RWKV_HEAD = 64
RWKV_HEADS = D_MODEL // RWKV_HEAD
DECAY_SCALE = math.exp(-0.5)
D_FF = 2816
N_MOD = 6
EPS = 1e-6
LN_EPS = 1e-5
GN_EPS = RWKV_HEAD * 1e-5
SOFTMAX_C = (HEAD_DIM ** -0.5) * math.log2(math.e)
LANES = 128
SUBLANES = 8
CHUNK = 64
N_PAIR = D_MODEL // LANES
SCAN_BATCH = 2
INV_BASE = 4
ROW_TILE = 256
ATTN_SPAN = 256
VMEM_LIMIT = 48 * 1024 * 1024

COL_Q = 0
COL_KV = 1024
COL_LORA = 1536
COL_GLU = 2048
COL_RKV = 4096
COL_GATE = 7168
N_IN_PAD = 10240


def _cparams(sem):
    return pltpu.CompilerParams(dimension_semantics=sem, vmem_limit_bytes=VMEM_LIMIT)


def _bdot(a, b):
    return jnp.dot(a.astype(BF16), b.astype(BF16), preferred_element_type=F32)


def _bdot_nt(a, b):
    return lax.dot_general(a.astype(BF16), b.astype(BF16), (((1,), (1,)), ((), ())),
                           preferred_element_type=F32)


def _bdot_tn(a, b):
    return lax.dot_general(a.astype(BF16), b.astype(BF16), (((0,), (0,)), ((), ())),
                           preferred_element_type=F32)


def _split2(x):
    hi = x.astype(BF16)
    lo = (x - hi.astype(F32)).astype(BF16)
    return hi, lo


def _dot3(a, b):
    ah, al = _split2(a)
    bh, bl = _split2(b)
    dot = functools.partial(jnp.dot, preferred_element_type=F32)
    return dot(ah, bh) + dot(al, bh) + dot(ah, bl)


def _sigmoid(x):
    return jax.nn.sigmoid(x)


def _head_sum(x, e_ref, et_ref):
    xh, xl = _split2(x)
    dot = functools.partial(jnp.dot, preferred_element_type=F32)
    s = dot(xh, e_ref[...]) + dot(xl, e_ref[...])
    sh, sl = _split2(s)
    return dot(sh, et_ref[...]) + dot(sl, et_ref[...])


def _mod_kernel(c_ref, w_ref, b_ref, o_ref):
    c = c_ref[...]
    o_ref[...] = _dot3(c * _sigmoid(c), w_ref[...]) + b_ref[...]


def _mod_call(cc, w_mod, b_mod):
    rows = cc.shape[0]
    n = w_mod.shape[1]
    tn = 1536
    return pl.pallas_call(
        _mod_kernel,
        grid=(n // tn,),
        in_specs=[pl.BlockSpec((rows, D_MODEL), lambda j: (0, 0)),
                  pl.BlockSpec((D_MODEL, tn), lambda j: (0, j)),
                  pl.BlockSpec((1, tn), lambda j: (0, j))],
        out_specs=pl.BlockSpec((rows, tn), lambda j: (0, j)),
        out_shape=jax.ShapeDtypeStruct((rows, n), F32),
        compiler_params=_cparams(("parallel",)),
        name="mod",
    )(cc, w_mod, b_mod.reshape(1, n))


def _mod_row(n_ctx_tiles, ctx_row):
    return lambda b, i, *_: (jnp.where(i < n_ctx_tiles, ctx_row, b), 0, 0)


def _nm_matmul_kernel(a_ref, mod_ref, g_ref, w_ref, o_ref, h_ref, *, shift_idx, seg, nseg):
    @pl.when(pl.program_id(1) == 0)
    def _():
        for s in range(nseg):
            rows = slice(s * seg, (s + 1) * seg)
            x = a_ref[rows, :]
            y = x * lax.rsqrt(jnp.mean(x * x, axis=-1, keepdims=True) + EPS) * g_ref[...]
            h = y * (1.0 + mod_ref[s, shift_idx + 1:shift_idx + 2, :]) + mod_ref[s, shift_idx:shift_idx + 1, :]
            h_ref[rows, :] = h.astype(BF16)

    o_ref[...] = jnp.dot(h_ref[...], w_ref[...], preferred_element_type=F32).astype(o_ref.dtype)


def _nm_matmul(a, modseg, g, w, *, shift_idx, tn, seg):
    bsz, tt, d = a.shape
    n = w.shape[1]
    rows = bsz * tt
    nseg = max(s for s in (4, 2, 1) if (rows // seg) % s == 0)
    tm = seg * nseg
    out = pl.pallas_call(
        functools.partial(_nm_matmul_kernel, shift_idx=shift_idx, seg=seg, nseg=nseg),
        grid=(rows // tm, n // tn),
        in_specs=[pl.BlockSpec((tm, d), lambda i, j: (i, 0)),
                  pl.BlockSpec((nseg, N_MOD, d), lambda i, j: (i, 0, 0)),
                  pl.BlockSpec((1, d), lambda i, j: (0, 0)),
                  pl.BlockSpec((d, tn), lambda i, j: (0, j))],
        out_specs=pl.BlockSpec((tm, tn), lambda i, j: (i, j)),
        out_shape=jax.ShapeDtypeStruct((rows, n), F32),
        scratch_shapes=[pltpu.VMEM((tm, d), BF16)],
        compiler_params=_cparams(("parallel", "arbitrary")),
        name="norm_mod_matmul",
    )(a.reshape(rows, d), modseg, g.reshape(1, d), w)
    return out.reshape(bsz, tt, n)


def _qk_prep_kernel(q_ref, kv_ref, cos_ref, sin_ref, qn_ref, kn_ref, qo_ref, ko_ref, vo_ref):
    cos = cos_ref[...]
    sin = sin_ref[...]
    lane = lax.broadcasted_iota(jnp.int32, cos.shape, 1)
    first = (lane & (N_FREQ)) == 0

    heads = ([(q_ref, h, qn_ref, qo_ref, SOFTMAX_C) for h in range(N_Q_HEADS)]
             + [(kv_ref, h, kn_ref, ko_ref, 1.0) for h in range(N_KV_HEADS)])
    sls = [slice(h * HEAD_DIM, (h + 1) * HEAD_DIM) for _, h, _, _, _ in heads]
    xs = [src[:, sl] for (src, _, _, _, _), sl in zip(heads, sls)]
    ms = [jnp.mean(x * x, axis=-1, keepdims=True) for x in xs]
    ys = [x * lax.rsqrt(m + EPS) * (g[...] * c) for x, m, (_, _, g, _, c) in zip(xs, ms, heads)]
    up = [pltpu.roll(y, LANES - N_FREQ, 1) for y in ys]
    dn = [pltpu.roll(y, N_FREQ, 1) for y in ys]
    for y, u, d, (_, _, _, dst, _), sl in zip(ys, up, dn, heads, sls):
        dst[:, sl] = (y * cos + jnp.where(first, u, d) * sin).astype(BF16)
    vo_ref[...] = kv_ref[:, N_KV_HEADS * HEAD_DIM:].astype(BF16)


def _qk_prep(proj, cos, sin, q_norm, k_norm, *, tm):
    bsz, tt, _ = proj.shape
    dq = N_Q_HEADS * HEAD_DIM
    dkv = N_KV_HEADS * HEAD_DIM
    return pl.pallas_call(
        _qk_prep_kernel,
        grid=(bsz, tt // tm),
        in_specs=[pl.BlockSpec((None, tm, dq), lambda b, i: (b, i, COL_Q // dq)),
                  pl.BlockSpec((None, tm, 2 * dkv), lambda b, i: (b, i, COL_KV // (2 * dkv))),
                  pl.BlockSpec((tm, HEAD_DIM), lambda b, i: (i, 0)),
                  pl.BlockSpec((tm, HEAD_DIM), lambda b, i: (i, 0)),
                  pl.BlockSpec((1, HEAD_DIM), lambda b, i: (0, 0)),
                  pl.BlockSpec((1, HEAD_DIM), lambda b, i: (0, 0))],
        out_specs=[pl.BlockSpec((None, tm, dq), lambda b, i: (b, i, 0)),
                   pl.BlockSpec((None, tm, dkv), lambda b, i: (b, i, 0)),
                   pl.BlockSpec((None, tm, dkv), lambda b, i: (b, i, 0))],
        out_shape=[jax.ShapeDtypeStruct((bsz, tt, dq), BF16),
                   jax.ShapeDtypeStruct((bsz, tt, dkv), BF16),
                   jax.ShapeDtypeStruct((bsz, tt, dkv), BF16)],
        compiler_params=_cparams(("parallel", "parallel")),
        name="qk_prep",
    )(proj, proj, cos, sin, q_norm.reshape(1, HEAD_DIM), k_norm.reshape(1, HEAD_DIM))


def _attn_kernel(q_ref, k_ref, v_ref, o_ref, *, n_ctx_tiles, ctx_len, skip_ctx):
    tq = q_ref.shape[0]
    tt = k_ref.shape[0]

    def attend(spans):
        q = jnp.concatenate([q_ref[:, g * HEAD_DIM:(g + 1) * HEAD_DIM] for g in range(GQA_GROUP)], axis=0)
        m = l = acc = None
        for start, size in spans:
            k = k_ref[start:start + size, :]
            v = v_ref[start:start + size, :]
            s = lax.dot_general(q, k, (((1,), (1,)), ((), ())), preferred_element_type=F32)
            mt = jnp.max(s, axis=-1, keepdims=True)
            if m is None:
                m = mt
                p = jnp.exp2(s - m)
                l = jnp.sum(p, axis=-1, keepdims=True)
                acc = jnp.dot(p.astype(BF16), v, preferred_element_type=F32)
            else:
                m_new = jnp.maximum(m, mt)
                alpha = jnp.exp2(m - m_new)
                p = jnp.exp2(s - m_new)
                l = alpha * l + jnp.sum(p, axis=-1, keepdims=True)
                acc = alpha * acc + jnp.dot(p.astype(BF16), v, preferred_element_type=F32)
                m = m_new
        o = acc / l
        for g in range(GQA_GROUP):
            o_ref[:, g * HEAD_DIM:(g + 1) * HEAD_DIM] = o[g * tq:(g + 1) * tq].astype(o_ref.dtype)

    x_spans = [(start, min(ATTN_SPAN, tt - start)) for start in range(0, tt, ATTN_SPAN)]
    if skip_ctx:
        attend(x_spans)
        return
    i = pl.program_id(2)

    @pl.when(i < n_ctx_tiles)
    def _():
        attend([(0, ctx_len)])

    @pl.when(i >= n_ctx_tiles)
    def _():
        attend(x_spans)


def _attention(q, k, v, *, ctx_len, tq, skip_ctx):
    bsz, tt, dq = q.shape
    gw = GQA_GROUP * HEAD_DIM
    off = ctx_len // tq if skip_ctx else 0
    return pl.pallas_call(
        functools.partial(_attn_kernel, n_ctx_tiles=ctx_len // tq, ctx_len=ctx_len, skip_ctx=skip_ctx),
        grid=(bsz, N_KV_HEADS, tt // tq - off),
        in_specs=[pl.BlockSpec((None, tq, gw), lambda b, h, i: (b, i + off, h)),
                  pl.BlockSpec((None, tt, HEAD_DIM), lambda b, h, i: (b, 0, h)),
                  pl.BlockSpec((None, tt, HEAD_DIM), lambda b, h, i: (b, 0, h))],
        out_specs=pl.BlockSpec((None, tq, gw), lambda b, h, i: (b, i, h)),
        out_shape=jax.ShapeDtypeStruct((bsz, tt - off * tq, dq), BF16),
        compiler_params=_cparams(("parallel", "parallel", "arbitrary")),
        name="attention",
    )(q, k, v)


def _halo_specs(tm, halo, tt, col_block, width, off=0):
    per = tm // halo
    last = tt // halo - 1
    prev = pl.BlockSpec((None, halo, width), lambda b, i: (b, jnp.maximum((i + off) * per - 1, 0), col_block))
    nxt = pl.BlockSpec((None, halo, width), lambda b, i: (b, jnp.minimum((i + off + 1) * per, last), col_block))
    return prev, nxt


def _edge_flags(i, n_ctx_tiles, n_tiles):
    first = jnp.logical_or(i == 0, i == n_ctx_tiles)
    last = jnp.logical_or(i == n_ctx_tiles - 1, i == n_tiles - 1)
    return first, last


def _conformer_kernel(ac_ref, bc_ref, ap_ref, bp_ref, an_ref, bn_ref, w_ref, cb_ref, lg_ref, lb_ref,
                      o_ref, buf_ref, sh_ref, *, n_ctx_tiles, n_tiles, tm, rc, tile0):
    first, last = _edge_flags(pl.program_id(1) + tile0, n_ctx_tiles, n_tiles)
    h = CONV_HALO
    buf_ref[0:h, :] = jnp.where(first, 0.0, ap_ref[...] * _sigmoid(bp_ref[...]))
    buf_ref[h:h + tm, :] = ac_ref[...] * _sigmoid(bc_ref[...])
    buf_ref[h + tm:h + tm + h, :] = jnp.where(last, 0.0, an_ref[...] * _sigmoid(bn_ref[...]))
    n_sh = sh_ref.shape[1]
    for s in range(1, SUBLANES):
        sh_ref[s - 1] = buf_ref[s:s + n_sh, :]
    off = h - CONV_K // 2
    for c in range(tm // rc):
        acc = jnp.zeros((rc, D_MODEL), F32)
        for j in range(CONV_K):
            s = (j + off) % SUBLANES
            r0 = c * rc + (j + off) - s
            tap = buf_ref[r0:r0 + rc, :] if s == 0 else sh_ref[s - 1, r0:r0 + rc, :]
            acc = acc + tap * jnp.concatenate([w_ref[j]] * (rc // SUBLANES), axis=0)
        y = acc + cb_ref[...]
        mu = jnp.mean(y, axis=-1, keepdims=True)
        dlt = y - mu
        var = jnp.mean(dlt * dlt, axis=-1, keepdims=True)
        z = dlt * lax.rsqrt(var + LN_EPS) * lg_ref[...] + lb_ref[...]
        o_ref[c * rc:(c + 1) * rc, :] = (z * _sigmoid(z)).astype(o_ref.dtype)


def _conformer(proj, conv_w, conv_b, ln_g, ln_b, *, ctx_len, tm, skip_ctx):
    bsz, tt, _ = proj.shape
    d = D_MODEL
    ca = COL_GLU // d
    off = ctx_len // tm if skip_ctx else 0
    prev_a, next_a = _halo_specs(tm, CONV_HALO, tt, ca, d, off)
    prev_b, next_b = _halo_specs(tm, CONV_HALO, tt, ca + 1, d, off)
    vec = lambda: pl.BlockSpec((1, d), lambda b, i: (0, 0))
    return pl.pallas_call(
        functools.partial(_conformer_kernel, n_ctx_tiles=ctx_len // tm, n_tiles=tt // tm, tm=tm, rc=32, tile0=off),
        grid=(bsz, tt // tm - off),
        in_specs=[pl.BlockSpec((None, tm, d), lambda b, i: (b, i + off, ca)),
                  pl.BlockSpec((None, tm, d), lambda b, i: (b, i + off, ca + 1)),
                  prev_a, prev_b, next_a, next_b,
                  pl.BlockSpec((CONV_K, SUBLANES, d), lambda b, i: (0, 0, 0)),
                  vec(), vec(), vec()],
        out_specs=pl.BlockSpec((None, tm, d), lambda b, i: (b, i, 0)),
        out_shape=jax.ShapeDtypeStruct((bsz, tt - off * tm, d), BF16),
        scratch_shapes=[pltpu.VMEM((tm + 2 * CONV_HALO, d), F32),
                        pltpu.VMEM((SUBLANES - 1, tm + 2 * CONV_HALO - SUBLANES, d), F32)],
        compiler_params=_cparams(("parallel", "parallel")),
        name="conformer",
    )(proj, proj, proj, proj, proj, proj, jnp.broadcast_to(conv_w[:, None, :], (CONV_K, SUBLANES, d)),
      conv_b.reshape(1, d), ln_g.reshape(1, d), ln_b.reshape(1, d))


def _rwkv_prep_kernel(rc_ref, kc_ref, vc_ref, rp_ref, kp_ref, vp_ref, rn_ref, kn_ref, vn_ref, lora_ref,
                      sw_ref, kk_ref, ka_ref, w0_ref, dup_ref, a0_ref, iup_ref, e_ref, et_ref,
                      r_ref, v_ref, kap_ref, lw_ref, kd_ref, bd_ref, buf_ref,
                      *, n_ctx_tiles, n_tiles, tm):
    first, last = _edge_flags(pl.program_id(1), n_ctx_tiles, n_tiles)
    h = SUBLANES
    d = D_MODEL

    def shift(cur_ref, prev_ref, next_ref, col):
        buf_ref[0:h, :] = jnp.where(first, 0.0, prev_ref[...])
        buf_ref[h:h + tm, :] = cur_ref[...]
        buf_ref[h + tm:h + tm + h, :] = jnp.where(last, 0.0, next_ref[...])
        sl = slice(col * d, (col + 1) * d)
        return (buf_ref[h - 1:h - 1 + tm, :] * sw_ref[0:1, sl] + buf_ref[h:h + tm, :] * sw_ref[1:2, sl]
                + buf_ref[h + 1:h + 1 + tm, :] * sw_ref[2:3, sl])

    r = shift(rc_ref, rp_ref, rn_ref, 0)
    k = shift(kc_ref, kp_ref, kn_ref, 1)
    v = shift(vc_ref, vp_ref, vn_ref, 2)
    r_ref[...] = r
    v_ref[...] = v
    kk = k * kk_ref[...]
    ss = _head_sum(kk * kk, e_ref, et_ref)
    kap = kk * lax.rsqrt(jnp.maximum(ss, 1e-12))
    kap_ref[...] = kap
    tw = jnp.tanh(lora_ref[:, 0:LANES])
    la = lora_ref[:, LANES:2 * LANES]
    for dr in range(2):
        z = w0_ref[dr:dr + 1, :] + _dot3(tw, dup_ref[dr])
        lw_ref[dr] = -DECAY_SCALE * _sigmoid(z)
        a = _sigmoid(a0_ref[dr:dr + 1, :] + _dot3(la, iup_ref[dr]))
        kd_ref[dr] = k * (1.0 + (a - 1.0) * ka_ref[...])
        bd_ref[dr] = a * kap


def _rwkv_prep(proj, shift_w, k_k, k_a, decay_w0, decay_up_pad, iclr_a0, iclr_up_pad, e, et, *, ctx_len, tm):
    bsz, tt, _ = proj.shape
    d = D_MODEL
    c0 = COL_RKV // d
    cur = lambda c: pl.BlockSpec((None, tm, d), lambda b, i: (b, i, c))
    halos = [_halo_specs(tm, SUBLANES, tt, c0 + c, d) for c in range(3)]
    full = lambda shape: pl.BlockSpec(shape, lambda b, i: (0,) * len(shape))
    out1 = pl.BlockSpec((None, tm, d), lambda b, i: (b, i, 0))
    out2 = pl.BlockSpec((2, None, tm, d), lambda b, i: (0, b, i, 0))
    s1 = jax.ShapeDtypeStruct((bsz, tt, d), F32)
    s2 = jax.ShapeDtypeStruct((2, bsz, tt, d), F32)
    return pl.pallas_call(
        functools.partial(_rwkv_prep_kernel, n_ctx_tiles=ctx_len // tm, n_tiles=tt // tm, tm=tm),
        grid=(bsz, tt // tm),
        in_specs=[cur(c0), cur(c0 + 1), cur(c0 + 2),
                  halos[0][0], halos[1][0], halos[2][0], halos[0][1], halos[1][1], halos[2][1],
                  pl.BlockSpec((None, tm, 512), lambda b, i: (b, i, COL_LORA // 512)),
                  full((3, 3 * d)), full((1, d)), full((1, d)), full((2, d)), full((2, LANES, d)),
                  full((2, d)), full((2, LANES, d)), full((d, LANES)), full((LANES, d))],
        out_specs=[out1, out1, out1, out2, out2, out2],
        out_shape=[s1, s1, s1, s2, s2, s2],
        scratch_shapes=[pltpu.VMEM((tm + 2 * SUBLANES, d), F32)],
        compiler_params=_cparams(("parallel", "parallel")),
        name="rwkv_prep",
    )(proj, proj, proj, proj, proj, proj, proj, proj, proj, proj,
      shift_w, k_k.reshape(1, d), k_a.reshape(1, d), decay_w0, decay_up_pad, iclr_a0, iclr_up_pad, e, et)


def _scan_kernel(r_ref, v_ref, kap_ref, lw_ref, k_ref, b_ref, y_ref, h_ref):
    c = CHUNK
    c2 = 2 * c
    sgn = 1 - 2 * pl.program_id(1)

    @pl.when(pl.program_id(2) == 0)
    def _():
        h_ref[...] = jnp.zeros_like(h_ref)

    row = lax.broadcasted_iota(jnp.int32, (c, c), 0)
    col = lax.broadcasted_iota(jnp.int32, (c, c), 1)
    incl = jnp.where((col - row) * sgn <= 0, 1.0, 0.0).astype(BF16)
    dot = functools.partial(jnp.dot, preferred_element_type=F32)
    nb = lw_ref.shape[0]
    cum = []
    for bi in range(nb):
        lw = lw_ref[bi]
        hi = lw.astype(BF16)
        rem = lw - hi.astype(F32)
        mid = rem.astype(BF16)
        lo = (rem - mid.astype(F32)).astype(BF16)
        cum.append(dot(incl, hi) + dot(incl, mid) + dot(incl, lo))

    prow = lax.broadcasted_iota(jnp.int32, (c, c2), 0)
    pcol = lax.broadcasted_iota(jnp.int32, (c, c2), 1)
    order = ((pcol & (c - 1)) - prow) * sgn
    strict = order < 0
    upto = order <= 0
    eye = order == 0
    head0 = pcol < RWKV_HEAD

    def stack(x):
        return jnp.concatenate([jnp.where(head0, x, 0.0), jnp.where(head0, 0.0, x)], axis=0)

    def pack(x):
        return jnp.where(head0, x[0:c], x[c:c2])

    units = [(bi, slice(p * LANES, (p + 1) * LANES)) for bi in range(nb) for p in range(N_PAIR)]
    pairs = range(len(units))
    kt, rt, vs, kend, bend, ptot, a = [], [], [], [], [], [], []
    for bi, sl in units:
        cum_p = cum[bi][:, sl]
        lw_p = lw_ref[bi, :, sl]
        tot_p = jnp.sum(lw_p, axis=0, keepdims=True)
        p_inv = jnp.exp(-cum_p)
        p_end = jnp.exp(tot_p - cum_p)
        k = k_ref[bi, :, sl]
        b = b_ref[bi, :, sl]
        kt.append(kap_ref[bi, :, sl] * jnp.exp(cum_p - lw_p))
        rt.append(r_ref[bi, :, sl] * jnp.exp(cum_p))
        vs.append(v_ref[bi, :, sl])
        kend.append(k * p_end)
        bend.append(b * p_end)
        ptot.append(jnp.exp(tot_p))
        a.append(_bdot_nt(jnp.concatenate([kt[-1], rt[-1]], axis=0),
                          jnp.concatenate([stack(b * p_inv), stack(k * p_inv)], axis=0)))
    a_ab = [jnp.where(strict, a[p][0:c, 0:c2], 0.0) for p in pairs]
    a_ak = [jnp.where(strict, a[p][0:c, c2:2 * c2], 0.0) for p in pairs]
    a_rb = [jnp.where(upto, a[p][c:c2, 0:c2], 0.0) for p in pairs]
    a_rk = [jnp.where(upto, a[p][c:c2, c2:2 * c2], 0.0) for p in pairs]

    blk = (pcol & (c - 1)) ^ prow
    ident = jnp.where(eye, 1.0, 0.0)
    l0 = [jnp.where(blk < INV_BASE, a_ab[p], 0.0) for p in pairs]
    sq = [_bdot(l0[p], stack(l0[p])) for p in pairs]
    t = [ident - l0[p] for p in pairs]
    t = [t[p] + _bdot(t[p], stack(sq[p])) for p in pairs]
    s_blk = INV_BASE
    while s_blk < c:
        off = jnp.logical_and(blk >= s_blk, blk < 2 * s_blk)
        tl = [_bdot(t[p], stack(jnp.where(off, a_ab[p], 0.0))) for p in pairs]
        t = [t[p] - _bdot(tl[p], stack(t[p])) for p in pairs]
        s_blk *= 2

    asv = [_bdot(jnp.concatenate([a_ak[p], a_rk[p]], axis=0), stack(vs[p])) for p in pairs]
    wu = [_bdot(t[p], jnp.concatenate([stack(kt[p]), stack(asv[p][0:c])], axis=1)) for p in pairs]
    w = [wu[p][:, 0:c2] for p in pairs]
    u0 = [wu[p][:, c2:2 * c2] for p in pairs]
    arb = [_bdot(a_rb[p], jnp.concatenate([stack(w[p]), stack(u0[p])], axis=1)) for p in pairs]
    y0 = [asv[p][c:c2] - arb[p][:, c2:2 * c2] for p in pairs]
    y1 = [rt[p] - arb[p][:, 0:c2] for p in pairs]
    m = [jnp.where(eye, ptot[p], 0.0) - pack(_bdot_tn(bend[p], w[p])) for p in pairs]
    nn = [pack(_bdot_tn(jnp.concatenate([kend[p], -bend[p]], axis=0), jnp.concatenate([vs[p], u0[p]], axis=0)))
          for p in pairs]
    for p in pairs:
        h0 = h_ref[p]
        hh = h0.astype(BF16)
        hl = h0 - hh.astype(F32)
        mh, ml = _split2(m[p])
        top = _bdot(jnp.concatenate([mh, ml, y1[p].astype(BF16)], axis=0), stack(hh.astype(F32)))
        y_ref[units[p][0], :, units[p][1]] = top[c2:c2 + c] + y0[p]
        h_ref[p] = top[0:c] + top[c:c2] + _bdot(mh, stack(hl)) + nn[p]


def _rwkv_scan(r, v, kap, lw, kd, bd, *, ctx_len):
    bsz, tt, d = r.shape
    nc = tt // CHUNK
    ncc = ctx_len // CHUNK

    def chunk(dr, s):
        return jnp.where(dr == 0, s, jnp.where(s < ncc, ncc - 1 - s, nc + ncc - 1 - s))

    ub = SCAN_BATCH if bsz % SCAN_BATCH == 0 else 1
    shared = pl.BlockSpec((ub, CHUNK, d), lambda b, dr, s: (b, chunk(dr, s), 0))
    per_dir = pl.BlockSpec((None, ub, CHUNK, d), lambda b, dr, s: (dr, b, chunk(dr, s), 0))
    return pl.pallas_call(
        _scan_kernel,
        grid=(bsz // ub, 2, nc),
        in_specs=[shared, shared, shared, per_dir, per_dir, per_dir],
        out_specs=per_dir,
        out_shape=jax.ShapeDtypeStruct((2, bsz, tt, d), F32),
        scratch_shapes=[pltpu.VMEM((ub * N_PAIR, CHUNK, LANES), F32)],
        compiler_params=_cparams(("parallel", "parallel", "arbitrary")),
        name="rwkv_scan",
    )(r, v, kap, lw, kd, bd)


def _merge_kernel(a_ref, att_ref, conv_ref, y_ref, r_ref, v_ref, kd_ref, lora_ref, ga_ref, gc_ref, gr_ref,
                  mod_ref, g_ref, gg_ref, gb_ref, rk_ref, gup_ref, e_ref, et_ref,
                  wa_ref, wc_ref, wr_ref, wo_ref, o_ref):
    dot = functools.partial(jnp.dot, preferred_element_type=F32)
    inv = 1.0 / RWKV_HEAD
    y = y_ref[0] + y_ref[1]
    mu = _head_sum(y, e_ref, et_ref) * inv
    dlt = y - mu
    var = _head_sum(dlt * dlt, e_ref, et_ref) * inv
    yn = dlt * lax.rsqrt(var + GN_EPS) * gg_ref[...] + gb_ref[...]
    bonus = _head_sum(r_ref[...] * (kd_ref[0] + kd_ref[1]) * rk_ref[...], e_ref, et_ref) * v_ref[...]
    gate = _bdot(_sigmoid(lora_ref[:, 2 * LANES:3 * LANES]), gup_ref[...])
    rw = ((yn + bonus) * gate).astype(BF16)
    m = (_sigmoid(ga_ref[...]) * dot(att_ref[...], wa_ref[...])
         + _sigmoid(gc_ref[...]) * dot(conv_ref[...], wc_ref[...])
         + _sigmoid(gr_ref[...]) * dot(rw, wr_ref[...]))
    z = dot(m.astype(BF16), wo_ref[...])
    zn = z * lax.rsqrt(jnp.mean(z * z, axis=-1, keepdims=True) + EPS) * g_ref[...]
    o_ref[...] = a_ref[...] + mod_ref[2:3, :] * zn


def _merge(a, att, conv, y, r, v, kd, proj, mod3, g, gn_g, gn_b, r_k, gate_up, e, et, wa, wc, wr, wo,
           *, n_ctx_tiles, ctx_row, tm, skip_ctx):
    bsz, tt, d = a.shape
    cg = COL_GATE // d
    off = n_ctx_tiles if skip_ctx else 0
    loc = pl.BlockSpec((None, tm, d), lambda b, i: (b, i, 0))
    one = pl.BlockSpec((None, tm, d), lambda b, i: (b, i + off, 0))
    two = pl.BlockSpec((2, None, tm, d), lambda b, i: (0, b, i + off, 0))
    gate = lambda c: pl.BlockSpec((None, tm, d), lambda b, i: (b, i + off, cg + c))
    full = lambda shape: pl.BlockSpec(shape, lambda b, i: (0,) * len(shape), pipeline_mode=pl.Buffered(1))
    return pl.pallas_call(
        _merge_kernel,
        grid=(bsz, tt // tm - off),
        in_specs=[one, loc, loc, two, one, one, two,
                  pl.BlockSpec((None, tm, 512), lambda b, i: (b, i + off, COL_LORA // 512)),
                  gate(0), gate(1), gate(2),
                  pl.BlockSpec((None, N_MOD, d), _mod_row(n_ctx_tiles - off, ctx_row)),
                  full((1, d)), full((1, d)), full((1, d)), full((1, d)), full((LANES, d)),
                  full((d, LANES)), full((LANES, d)),
                  full((d, d)), full((d, d)), full((d, d)), full((d, d))],
        out_specs=loc,
        out_shape=jax.ShapeDtypeStruct((bsz, tt - off * tm, d), F32),
        compiler_params=_cparams(("parallel", "parallel")),
        name="merge",
    )(a, att, conv, y, r, v, kd, proj, proj, proj, proj, mod3, g.reshape(1, d), gn_g.reshape(1, d),
      gn_b.reshape(1, d), r_k.reshape(1, d), gate_up, e, et, wa, wc, wr, wo)


def _ffn_tail_kernel(a_ref, zc_ref, zp_ref, zn_ref, cw_ref, mod_ref, g_ref, wd_ref, o_ref, buf_ref,
                     *, n_ctx_tiles, n_tiles, tm):
    first, last = _edge_flags(pl.program_id(1), n_ctx_tiles, n_tiles)
    h = SUBLANES
    buf_ref[0:h, :] = jnp.where(first, 0.0, zp_ref[...])
    buf_ref[h:h + tm, :] = zc_ref[...]
    buf_ref[h + tm:h + tm + h, :] = jnp.where(last, 0.0, zn_ref[...])

    def conv(sl):
        return (buf_ref[h - 1:h - 1 + tm, sl] * cw_ref[0:1, sl] + buf_ref[h:h + tm, sl] * cw_ref[1:2, sl]
                + buf_ref[h + 1:h + 1 + tm, sl] * cw_ref[2:3, sl])

    gate = conv(slice(0, D_FF))
    val = conv(slice(D_FF, 2 * D_FF))
    u = (gate * _sigmoid(gate) * val).astype(BF16)
    z = jnp.dot(u, wd_ref[...], preferred_element_type=F32)
    zn = z * lax.rsqrt(jnp.mean(z * z, axis=-1, keepdims=True) + EPS) * g_ref[...]
    o_ref[...] = a_ref[...] + mod_ref[5:6, :] * zn


def _ffn_tail(a, z, conv_w, mod3, g, wd, *, n_ctx_tiles, ctx_row, tm):
    bsz, tt, d = a.shape
    f2 = 2 * D_FF
    prev, nxt = _halo_specs(tm, SUBLANES, tt, 0, f2)
    one = pl.BlockSpec((None, tm, d), lambda b, i: (b, i, 0))
    return pl.pallas_call(
        functools.partial(_ffn_tail_kernel, n_ctx_tiles=n_ctx_tiles, n_tiles=tt // tm, tm=tm),
        grid=(bsz, tt // tm),
        in_specs=[one, pl.BlockSpec((None, tm, f2), lambda b, i: (b, i, 0)), prev, nxt,
                  pl.BlockSpec((3, f2), lambda b, i: (0, 0)),
                  pl.BlockSpec((None, N_MOD, d), _mod_row(n_ctx_tiles, ctx_row)),
                  pl.BlockSpec((1, d), lambda b, i: (0, 0)),
                  pl.BlockSpec((D_FF, d), lambda b, i: (0, 0))],
        out_specs=one,
        out_shape=jax.ShapeDtypeStruct((bsz, tt, d), F32),
        scratch_shapes=[pltpu.VMEM((tm + 2 * SUBLANES, f2), F32)],
        compiler_params=_cparams(("parallel", "parallel")),
        name="ffn_tail",
    )(a, z, z, z, conv_w, mod3, g.reshape(1, d), wd)


def _rope_tables(ctx_len, seq):
    rows = seq // GRID_W
    row = jnp.repeat(jnp.arange(rows, dtype=F32), GRID_W)
    col = jnp.tile(jnp.arange(GRID_W, dtype=F32), rows)
    inv_freq = ROPE_THETA ** (-jnp.arange(N_FREQ, dtype=F32) / N_FREQ)
    ang_r = row[:, None] * inv_freq
    ang_c = col[:, None] * inv_freq
    cos = jnp.concatenate([jnp.cos(ang_r), jnp.cos(ang_r), jnp.cos(ang_c), jnp.cos(ang_c)], axis=-1)
    sin = jnp.concatenate([-jnp.sin(ang_r), jnp.sin(ang_r), -jnp.sin(ang_c), jnp.sin(ang_c)], axis=-1)
    cos = jnp.concatenate([jnp.ones((ctx_len, HEAD_DIM), F32), cos], axis=0)
    sin = jnp.concatenate([jnp.zeros((ctx_len, HEAD_DIM), F32), sin], axis=0)
    return cos, sin


def _reorder_w_in(w):
    qkv = w[:, 0:1536]
    glu = w[:, 1536:3584]
    rkv = w[:, 3584:6656]
    lora = w[:, 6656:7040]
    gates = w[:, 7040:10112]
    pad = jnp.zeros((w.shape[0], LANES), w.dtype)
    return jnp.concatenate([qkv, lora, pad, glu, rkv, gates], axis=1).astype(BF16)


def _pad_lora_up(up):
    z = jnp.zeros_like(up[0])
    return jnp.stack([jnp.concatenate([up[0], z], axis=0), jnp.concatenate([z, up[1]], axis=0)])


def kernel(x, c, ctx, c_ctx, w_mod, b_mod, g_pre_mix, g_post_mix, g_pre_ffn, g_post_ffn, w_in, q_norm, k_norm,
           w_attn_o, conv_w, conv_b, conv_ln_g, conv_ln_b, w_conv_o, shift_w, decay_w0, decay_up, iclr_a0,
           iclr_up, gate_up, k_k, k_a, r_k, wkv_gn_g, wkv_gn_b, w_rwkv_o, w_out, w_ffn_up, ffn_conv_w, w_ffn_down):
    bsz, seq, d = x.shape
    ctx_len = ctx.shape[1]
    depth = w_mod.shape[0]
    tm = min(ROW_TILE, ctx_len)
    assert d == D_MODEL and ctx_len % tm == 0 and seq % tm == 0 and ctx_len % CHUNK == 0 and seq % CHUNK == 0
    n_ctx_tiles = ctx_len // tm

    a = jnp.concatenate([ctx, x], axis=1)
    mod_rows = -(-(bsz + 1) // SUBLANES) * SUBLANES
    cc = jnp.zeros((mod_rows, d), F32).at[:bsz].set(c).at[bsz].set(c_ctx)
    cos, sin = _rope_tables(ctx_len, seq)
    head = jnp.arange(d, dtype=jnp.int32) // RWKV_HEAD
    e = (head[:, None] == jnp.arange(LANES, dtype=jnp.int32)[None, :]).astype(BF16)
    et = e.T
    def seg_rows(segs, n_ctx):
        return jnp.array([bsz if s % segs < n_ctx else s // segs for s in range(bsz * segs)], jnp.int32)

    for l in range(depth):
        last = l == depth - 1
        mod3 = _mod_call(cc, w_mod[l], b_mod[l]).reshape(mod_rows, N_MOD, d)
        modseg = jnp.take(mod3, seg_rows((ctx_len + seq) // tm, n_ctx_tiles), axis=0)
        proj = _nm_matmul(a, modseg, g_pre_mix[l], _reorder_w_in(w_in[l]), shift_idx=0, tn=2048, seg=tm)
        q, k, v = _qk_prep(proj, cos, sin, q_norm[l], k_norm[l], tm=tm)
        att = _attention(q, k, v, ctx_len=ctx_len, tq=tm, skip_ctx=last)
        conv = _conformer(proj, conv_w[l], conv_b[l], conv_ln_g[l], conv_ln_b[l], ctx_len=ctx_len, tm=tm,
                          skip_ctx=last)
        r, vv, kap, lw, kd, bd = _rwkv_prep(proj, shift_w[l], k_k[l], k_a[l], decay_w0[l], _pad_lora_up(decay_up[l]),
                                            iclr_a0[l], _pad_lora_up(iclr_up[l]), e, et, ctx_len=ctx_len, tm=tm)
        y = _rwkv_scan(r, vv, kap, lw, kd, bd, ctx_len=ctx_len)
        a = _merge(a, att, conv, y, r, vv, kd, proj, mod3, g_post_mix[l], wkv_gn_g[l], wkv_gn_b[l], r_k[l],
                   gate_up[l].astype(BF16), e, et, w_attn_o[l].astype(BF16), w_conv_o[l].astype(BF16),
                   w_rwkv_o[l].astype(BF16), w_out[l].astype(BF16), n_ctx_tiles=n_ctx_tiles, ctx_row=bsz, tm=tm,
                   skip_ctx=last)
        n_ctx = 0 if last else n_ctx_tiles
        if last:
            modseg = jnp.take(mod3, seg_rows(seq // tm, 0), axis=0)
        z = _nm_matmul(a, modseg, g_pre_ffn[l], w_ffn_up[l].astype(BF16), shift_idx=3, tn=D_FF // 2, seg=tm)
        a = _ffn_tail(a, z, ffn_conv_w[l], mod3, g_post_ffn[l], w_ffn_down[l].astype(BF16),
                      n_ctx_tiles=n_ctx, ctx_row=bsz, tm=tm)
    return a
```

```python
import functools
import math

import jax
import jax.numpy as jnp
from jax import lax
from jax.experimental import pallas as pl
from jax.experimental.pallas import tpu as pltpu

F32 = jnp.float32
BF16 = jnp.bfloat16

D_MODEL = 1024
GRID_W = 64
N_Q_HEADS = 8
N_KV_HEADS = 2
GQA_GROUP = N_Q_HEADS // N_KV_HEADS
HEAD_DIM = 128
N_FREQ = HEAD_DIM // 4
ROPE_THETA = 10000.0
CONV_K = 31
CONV_HALO = 16
RWKV_HEAD = 64
RWKV_HEADS = D_MODEL // RWKV_HEAD
DECAY_SCALE = math.exp(-0.5)
D_FF = 2816
N_MOD = 6
EPS = 1e-6
LN_EPS = 1e-5
GN_EPS = RWKV_HEAD * 1e-5
SOFTMAX_C = (HEAD_DIM ** -0.5) * math.log2(math.e)
LANES = 128
SUBLANES = 8
CHUNK = 64
N_PAIR = D_MODEL // LANES
SCAN_BATCH = 2
INV_BASE = 4
ROW_TILE = 256
CONV_ROWS = 32
MOD_TN = 1536
PROJ_TN = 2048
LORA_BLOCK = 512
ATTN_SPAN = 256
VMEM_LIMIT = 48 * 1024 * 1024

COL_Q = 0
COL_KV = 1024
COL_LORA = 1536
COL_GLU = 2048
COL_RKV = 4096
COL_GATE = 7168
N_IN_PAD = 10240


def _cparams(sem):
    return pltpu.CompilerParams(dimension_semantics=sem, vmem_limit_bytes=VMEM_LIMIT)


def _bdot(a, b):
    return jnp.dot(a.astype(BF16), b.astype(BF16), preferred_element_type=F32)


def _bdot_nt(a, b):
    return lax.dot_general(a.astype(BF16), b.astype(BF16), (((1,), (1,)), ((), ())),
                           preferred_element_type=F32)


def _bdot_tn(a, b):
    return lax.dot_general(a.astype(BF16), b.astype(BF16), (((0,), (0,)), ((), ())),
                           preferred_element_type=F32)


def _split2(x):
    hi = x.astype(BF16)
    lo = (x - hi.astype(F32)).astype(BF16)
    return hi, lo


def _dot3_presplit(a, bh, bl):
    ah, al = _split2(a)
    dot = functools.partial(jnp.dot, preferred_element_type=F32)
    return dot(ah, bh) + dot(al, bh) + dot(ah, bl)


def _dot3(a, b):
    return _dot3_presplit(a, *_split2(b))


def _sigmoid(x):
    return jax.nn.sigmoid(x)


def _head_sum(x, e_ref, et_ref):
    xh, xl = _split2(x)
    dot = functools.partial(jnp.dot, preferred_element_type=F32)
    s = dot(xh, e_ref[...]) + dot(xl, e_ref[...])
    sh, sl = _split2(s)
    return dot(sh, et_ref[...]) + dot(sl, et_ref[...])


def _mod_kernel(c_ref, w_ref, b_ref, o_ref):
    c = c_ref[...]
    o_ref[...] = _dot3(c * _sigmoid(c), w_ref[...]) + b_ref[...]


def _mod_call(cc, w_mod, b_mod):
    rows = cc.shape[0]
    n = w_mod.shape[1]
    tn = MOD_TN
    return pl.pallas_call(
        _mod_kernel,
        grid=(n // tn,),
        in_specs=[pl.BlockSpec((rows, D_MODEL), lambda j: (0, 0)),
                  pl.BlockSpec((D_MODEL, tn), lambda j: (0, j)),
                  pl.BlockSpec((1, tn), lambda j: (0, j))],
        out_specs=pl.BlockSpec((rows, tn), lambda j: (0, j)),
        out_shape=jax.ShapeDtypeStruct((rows, n), F32),
        compiler_params=_cparams(("parallel",)),
        name="mod",
    )(cc, w_mod, b_mod.reshape(1, n))


def _mod_row(n_ctx_tiles, ctx_row):
    return lambda b, i, *_: (jnp.where(i < n_ctx_tiles, ctx_row, b), 0, 0)


def _nm_matmul_kernel(a_ref, mod_ref, g_ref, w_ref, o_ref, h_ref, *, shift_idx, seg, nseg):
    @pl.when(pl.program_id(1) == 0)
    def _():
        for s in range(nseg):
            rows = slice(s * seg, (s + 1) * seg)
            x = a_ref[rows, :]
            y = x * lax.rsqrt(jnp.mean(x * x, axis=-1, keepdims=True) + EPS) * g_ref[...]
            h = y * (1.0 + mod_ref[s, shift_idx + 1:shift_idx + 2, :]) + mod_ref[s, shift_idx:shift_idx + 1, :]
            h_ref[rows, :] = h.astype(BF16)

    o_ref[...] = jnp.dot(h_ref[...], w_ref[...], preferred_element_type=F32).astype(o_ref.dtype)


def _nm_matmul(a, modseg, g, w, *, shift_idx, tn, seg, max_seg):
    bsz, tt, d = a.shape
    n = w.shape[1]
    rows = bsz * tt
    nseg = max(s for s in (4, 2, 1) if s <= max_seg and (rows // seg) % s == 0)
    tm = seg * nseg
    out = pl.pallas_call(
        functools.partial(_nm_matmul_kernel, shift_idx=shift_idx, seg=seg, nseg=nseg),
        grid=(rows // tm, n // tn),
        in_specs=[pl.BlockSpec((tm, d), lambda i, j: (i, 0)),
                  pl.BlockSpec((nseg, N_MOD, d), lambda i, j: (i, 0, 0)),
                  pl.BlockSpec((1, d), lambda i, j: (0, 0)),
                  pl.BlockSpec((d, tn), lambda i, j: (0, j))],
        out_specs=pl.BlockSpec((tm, tn), lambda i, j: (i, j)),
        out_shape=jax.ShapeDtypeStruct((rows, n), F32),
        scratch_shapes=[pltpu.VMEM((tm, d), BF16)],
        compiler_params=_cparams(("parallel", "arbitrary")),
        name="norm_mod_matmul",
    )(a.reshape(rows, d), modseg, g.reshape(1, d), w)
    return out.reshape(bsz, tt, n)


def _qk_prep_kernel(q_ref, kv_ref, cos_ref, sin_ref, qn_ref, kn_ref, qo_ref, ko_ref, vo_ref):
    cos = cos_ref[...]
    sin = sin_ref[...]
    lane = lax.broadcasted_iota(jnp.int32, cos.shape, 1)
    first = (lane & (N_FREQ)) == 0

    heads = ([(q_ref, h, qn_ref, qo_ref, SOFTMAX_C) for h in range(N_Q_HEADS)]
             + [(kv_ref, h, kn_ref, ko_ref, 1.0) for h in range(N_KV_HEADS)])
    sls = [slice(h * HEAD_DIM, (h + 1) * HEAD_DIM) for _, h, _, _, _ in heads]
    xs = [src[:, sl] for (src, _, _, _, _), sl in zip(heads, sls)]
    ms = [jnp.mean(x * x, axis=-1, keepdims=True) for x in xs]
    ys = [x * lax.rsqrt(m + EPS) * (g[...] * c) for x, m, (_, _, g, _, c) in zip(xs, ms, heads)]
    up = [pltpu.roll(y, LANES - N_FREQ, 1) for y in ys]
    dn = [pltpu.roll(y, N_FREQ, 1) for y in ys]
    for y, u, d, (_, _, _, dst, _), sl in zip(ys, up, dn, heads, sls):
        dst[:, sl] = (y * cos + jnp.where(first, u, d) * sin).astype(BF16)
    vo_ref[...] = kv_ref[:, N_KV_HEADS * HEAD_DIM:].astype(BF16)


def _qk_prep(proj, cos, sin, q_norm, k_norm, *, tm):
    bsz, tt, _ = proj.shape
    dq = N_Q_HEADS * HEAD_DIM
    dkv = N_KV_HEADS * HEAD_DIM
    return pl.pallas_call(
        _qk_prep_kernel,
        grid=(bsz, tt // tm),
        in_specs=[pl.BlockSpec((None, tm, dq), lambda b, i: (b, i, COL_Q // dq)),
                  pl.BlockSpec((None, tm, 2 * dkv), lambda b, i: (b, i, COL_KV // (2 * dkv))),
                  pl.BlockSpec((tm, HEAD_DIM), lambda b, i: (i, 0)),
                  pl.BlockSpec((tm, HEAD_DIM), lambda b, i: (i, 0)),
                  pl.BlockSpec((1, HEAD_DIM), lambda b, i: (0, 0)),
                  pl.BlockSpec((1, HEAD_DIM), lambda b, i: (0, 0))],
        out_specs=[pl.BlockSpec((None, tm, dq), lambda b, i: (b, i, 0)),
                   pl.BlockSpec((None, tm, dkv), lambda b, i: (b, i, 0)),
                   pl.BlockSpec((None, tm, dkv), lambda b, i: (b, i, 0))],
        out_shape=[jax.ShapeDtypeStruct((bsz, tt, dq), BF16),
                   jax.ShapeDtypeStruct((bsz, tt, dkv), BF16),
                   jax.ShapeDtypeStruct((bsz, tt, dkv), BF16)],
        compiler_params=_cparams(("parallel", "parallel")),
        name="qk_prep",
    )(proj, proj, cos, sin, q_norm.reshape(1, HEAD_DIM), k_norm.reshape(1, HEAD_DIM))


def _attn_kernel(q_ref, k_ref, v_ref, o_ref, *, n_ctx_tiles, ctx_len, skip_ctx):
    tq = q_ref.shape[0]
    tt = k_ref.shape[0]

    def attend(spans):
        q = jnp.concatenate([q_ref[:, g * HEAD_DIM:(g + 1) * HEAD_DIM] for g in range(GQA_GROUP)], axis=0)
        m = l = acc = None
        for start, size in spans:
            k = k_ref[start:start + size, :]
            v = v_ref[start:start + size, :]
            s = lax.dot_general(q, k, (((1,), (1,)), ((), ())), preferred_element_type=F32)
            mt = jnp.max(s, axis=-1, keepdims=True)
            if m is None:
                m = mt
                p = jnp.exp2(s - m)
                l = jnp.sum(p, axis=-1, keepdims=True)
                acc = jnp.dot(p.astype(BF16), v, preferred_element_type=F32)
            else:
                m_new = jnp.maximum(m, mt)
                alpha = jnp.exp2(m - m_new)
                p = jnp.exp2(s - m_new)
                l = alpha * l + jnp.sum(p, axis=-1, keepdims=True)
                acc = alpha * acc + jnp.dot(p.astype(BF16), v, preferred_element_type=F32)
                m = m_new
        o = acc / l
        for g in range(GQA_GROUP):
            o_ref[:, g * HEAD_DIM:(g + 1) * HEAD_DIM] = o[g * tq:(g + 1) * tq].astype(o_ref.dtype)

    x_spans = [(start, min(ATTN_SPAN, tt - start)) for start in range(0, tt, ATTN_SPAN)]
    if skip_ctx:
        attend(x_spans)
        return
    i = pl.program_id(2)

    @pl.when(i < n_ctx_tiles)
    def _():
        attend([(0, ctx_len)])

    @pl.when(i >= n_ctx_tiles)
    def _():
        attend(x_spans)


def _attention(q, k, v, *, ctx_len, tq, skip_ctx):
    bsz, tt, dq = q.shape
    gw = GQA_GROUP * HEAD_DIM
    off = ctx_len // tq if skip_ctx else 0
    return pl.pallas_call(
        functools.partial(_attn_kernel, n_ctx_tiles=ctx_len // tq, ctx_len=ctx_len, skip_ctx=skip_ctx),
        grid=(bsz, N_KV_HEADS, tt // tq - off),
        in_specs=[pl.BlockSpec((None, tq, gw), lambda b, h, i: (b, i + off, h)),
                  pl.BlockSpec((None, tt, HEAD_DIM), lambda b, h, i: (b, 0, h)),
                  pl.BlockSpec((None, tt, HEAD_DIM), lambda b, h, i: (b, 0, h))],
        out_specs=pl.BlockSpec((None, tq, gw), lambda b, h, i: (b, i, h)),
        out_shape=jax.ShapeDtypeStruct((bsz, tt - off * tq, dq), BF16),
        compiler_params=_cparams(("parallel", "parallel", "arbitrary")),
        name="attention",
    )(q, k, v)


def _halo_specs(tm, halo, tt, col_block, width, off=0):
    per = tm // halo
    last = tt // halo - 1
    prev = pl.BlockSpec((None, halo, width), lambda b, i: (b, jnp.maximum((i + off) * per - 1, 0), col_block))
    nxt = pl.BlockSpec((None, halo, width), lambda b, i: (b, jnp.minimum((i + off + 1) * per, last), col_block))
    return prev, nxt


def _edge_flags(i, n_ctx_tiles, n_tiles):
    first = jnp.logical_or(i == 0, i == n_ctx_tiles)
    last = jnp.logical_or(i == n_ctx_tiles - 1, i == n_tiles - 1)
    return first, last


def _conformer_kernel(cur_ref, prev_ref, next_ref, w_ref, cb_ref, lg_ref, lb_ref,
                      o_ref, buf_ref, sh_ref, *, n_ctx_tiles, n_tiles, tm, rc, tile0):
    first, last = _edge_flags(pl.program_id(1) + tile0, n_ctx_tiles, n_tiles)
    h = CONV_HALO
    d = D_MODEL

    def glu(ref):
        return ref[:, 0:d] * _sigmoid(ref[:, d:2 * d])

    buf_ref[0:h, :] = jnp.where(first, 0.0, glu(prev_ref))
    buf_ref[h:h + tm, :] = glu(cur_ref)
    buf_ref[h + tm:h + tm + h, :] = jnp.where(last, 0.0, glu(next_ref))
    n_sh = sh_ref.shape[1]
    for s in range(1, SUBLANES):
        sh_ref[s - 1] = buf_ref[s:s + n_sh, :]
    lead = h - CONV_K // 2
    for c in range(tm // rc):
        acc = jnp.zeros((rc, D_MODEL), F32)
        for j in range(CONV_K):
            s = (j + lead) % SUBLANES
            r0 = c * rc + (j + lead) - s
            tap = buf_ref[r0:r0 + rc, :] if s == 0 else sh_ref[s - 1, r0:r0 + rc, :]
            acc = acc + tap * jnp.concatenate([w_ref[j]] * (rc // SUBLANES), axis=0)
        y = acc + cb_ref[...]
        mu = jnp.mean(y, axis=-1, keepdims=True)
        dlt = y - mu
        var = jnp.mean(dlt * dlt, axis=-1, keepdims=True)
        z = dlt * lax.rsqrt(var + LN_EPS) * lg_ref[...] + lb_ref[...]
        o_ref[c * rc:(c + 1) * rc, :] = (z * _sigmoid(z)).astype(o_ref.dtype)


def _conformer(proj, conv_w, conv_b, ln_g, ln_b, *, ctx_len, tm, skip_ctx):
    bsz, tt, _ = proj.shape
    d = D_MODEL
    cg = COL_GLU // (2 * d)
    off = ctx_len // tm if skip_ctx else 0
    prev, nxt = _halo_specs(tm, CONV_HALO, tt, cg, 2 * d, off)
    vec = lambda: pl.BlockSpec((1, d), lambda b, i: (0, 0))
    return pl.pallas_call(
        functools.partial(_conformer_kernel, n_ctx_tiles=ctx_len // tm, n_tiles=tt // tm, tm=tm, rc=CONV_ROWS,
                          tile0=off),
        grid=(bsz, tt // tm - off),
        in_specs=[pl.BlockSpec((None, tm, 2 * d), lambda b, i: (b, i + off, cg)), prev, nxt,
                  pl.BlockSpec((CONV_K, SUBLANES, d), lambda b, i: (0, 0, 0)),
                  vec(), vec(), vec()],
        out_specs=pl.BlockSpec((None, tm, d), lambda b, i: (b, i, 0)),
        out_shape=jax.ShapeDtypeStruct((bsz, tt - off * tm, d), BF16),
        scratch_shapes=[pltpu.VMEM((tm + 2 * CONV_HALO, d), F32),
                        pltpu.VMEM((SUBLANES - 1, tm + 2 * CONV_HALO - SUBLANES, d), F32)],
        compiler_params=_cparams(("parallel", "parallel")),
        name="conformer",
    )(proj, proj, proj, jnp.broadcast_to(conv_w[:, None, :], (CONV_K, SUBLANES, d)),
      conv_b.reshape(1, d), ln_g.reshape(1, d), ln_b.reshape(1, d))


def _rwkv_prep_kernel(rc_ref, kc_ref, vc_ref, rp_ref, kp_ref, vp_ref, rn_ref, kn_ref, vn_ref, lora_ref,
                      sw_ref, kk_ref, ka_ref, w0_ref, dup_ref, a0_ref, iup_ref, e_ref, et_ref,
                      r_ref, v_ref, kap_ref, lw_ref, kd_ref, bd_ref, buf_ref,
                      *, n_ctx_tiles, n_tiles, tm):
    first, last = _edge_flags(pl.program_id(1), n_ctx_tiles, n_tiles)
    h = SUBLANES
    d = D_MODEL

    def shift(cur_ref, prev_ref, next_ref, col):
        buf_ref[0:h, :] = jnp.where(first, 0.0, prev_ref[...])
        buf_ref[h:h + tm, :] = cur_ref[...]
        buf_ref[h + tm:h + tm + h, :] = jnp.where(last, 0.0, next_ref[...])
        sl = slice(col * d, (col + 1) * d)
        return (buf_ref[h - 1:h - 1 + tm, :] * sw_ref[0:1, sl] + buf_ref[h:h + tm, :] * sw_ref[1:2, sl]
                + buf_ref[h + 1:h + 1 + tm, :] * sw_ref[2:3, sl])

    r = shift(rc_ref, rp_ref, rn_ref, 0)
    k = shift(kc_ref, kp_ref, kn_ref, 1)
    v = shift(vc_ref, vp_ref, vn_ref, 2)
    r_ref[...] = r
    v_ref[...] = v
    kk = k * kk_ref[...]
    ss = _head_sum(kk * kk, e_ref, et_ref)
    kap = kk * lax.rsqrt(jnp.maximum(ss, 1e-12))
    kap_ref[...] = kap
    tw = jnp.tanh(lora_ref[:, 0:LANES])
    la = lora_ref[:, LANES:2 * LANES]
    for dr in range(2):
        z = w0_ref[dr:dr + 1, :] + _dot3_presplit(tw, dup_ref[dr, 0], dup_ref[dr, 1])
        lw_ref[dr] = -DECAY_SCALE * _sigmoid(z)
        a = _sigmoid(a0_ref[dr:dr + 1, :] + _dot3_presplit(la, iup_ref[dr, 0], iup_ref[dr, 1]))
        kd_ref[dr] = k * (1.0 + (a - 1.0) * ka_ref[...])
        bd_ref[dr] = a * kap


def _rwkv_prep(proj, shift_w, k_k, k_a, decay_w0, decay_up_pad, iclr_a0, iclr_up_pad, e, et, *, ctx_len, tm):
    bsz, tt, _ = proj.shape
    d = D_MODEL
    c0 = COL_RKV // d
    cur = lambda c: pl.BlockSpec((None, tm, d), lambda b, i: (b, i, c))
    halos = [_halo_specs(tm, SUBLANES, tt, c0 + c, d) for c in range(3)]
    full = lambda shape: pl.BlockSpec(shape, lambda b, i: (0,) * len(shape))
    out1 = pl.BlockSpec((None, tm, d), lambda b, i: (b, i, 0))
    out2 = pl.BlockSpec((2, None, tm, d), lambda b, i: (0, b, i, 0))
    s1 = jax.ShapeDtypeStruct((bsz, tt, d), F32)
    s2 = jax.ShapeDtypeStruct((2, bsz, tt, d), F32)
    return pl.pallas_call(
        functools.partial(_rwkv_prep_kernel, n_ctx_tiles=ctx_len // tm, n_tiles=tt // tm, tm=tm),
        grid=(bsz, tt // tm),
        in_specs=[cur(c0), cur(c0 + 1), cur(c0 + 2),
                  halos[0][0], halos[1][0], halos[2][0], halos[0][1], halos[1][1], halos[2][1],
                  pl.BlockSpec((None, tm, LORA_BLOCK), lambda b, i: (b, i, COL_LORA // LORA_BLOCK)),
                  full((3, 3 * d)), full((1, d)), full((1, d)), full((2, d)), full((2, 2, LANES, d)),
                  full((2, d)), full((2, 2, LANES, d)), full((d, LANES)), full((LANES, d))],
        out_specs=[out1, out1, out1, out2, out2, out2],
        out_shape=[s1, s1, s1, s2, s2, s2],
        scratch_shapes=[pltpu.VMEM((tm + 2 * SUBLANES, d), F32)],
        compiler_params=_cparams(("parallel", "parallel")),
        name="rwkv_prep",
    )(proj, proj, proj, proj, proj, proj, proj, proj, proj, proj,
      shift_w, k_k.reshape(1, d), k_a.reshape(1, d), decay_w0, decay_up_pad, iclr_a0, iclr_up_pad, e, et)


def _scan_kernel(r_ref, v_ref, kap_ref, lw_ref, k_ref, b_ref, y_ref, h_ref):
    c = CHUNK
    c2 = 2 * c
    sgn = 1 - 2 * pl.program_id(1)

    @pl.when(pl.program_id(2) == 0)
    def _():
        h_ref[...] = jnp.zeros_like(h_ref)

    row = lax.broadcasted_iota(jnp.int32, (c, c), 0)
    col = lax.broadcasted_iota(jnp.int32, (c, c), 1)
    incl = jnp.where((col - row) * sgn <= 0, 1.0, 0.0).astype(BF16)
    dot = functools.partial(jnp.dot, preferred_element_type=F32)
    nb = lw_ref.shape[0]
    cum = []
    for bi in range(nb):
        lw = lw_ref[bi]
        hi = lw.astype(BF16)
        rem = lw - hi.astype(F32)
        mid = rem.astype(BF16)
        lo = (rem - mid.astype(F32)).astype(BF16)
        cum.append(dot(incl, hi) + dot(incl, mid) + dot(incl, lo))

    prow = lax.broadcasted_iota(jnp.int32, (c, c2), 0)
    pcol = lax.broadcasted_iota(jnp.int32, (c, c2), 1)
    order = ((pcol & (c - 1)) - prow) * sgn
    strict = order < 0
    upto = order <= 0
    eye = order == 0
    head0 = pcol < RWKV_HEAD

    def stack(x):
        return jnp.concatenate([jnp.where(head0, x, 0.0), jnp.where(head0, 0.0, x)], axis=0)

    def pack(x):
        return jnp.where(head0, x[0:c], x[c:c2])

    units = [(bi, slice(p * LANES, (p + 1) * LANES)) for bi in range(nb) for p in range(N_PAIR)]
    pairs = range(len(units))
    kt, rt, vs, kend, bend, ptot, a = [], [], [], [], [], [], []
    for bi, sl in units:
        cum_p = cum[bi][:, sl]
        lw_p = lw_ref[bi, :, sl]
        tot_p = jnp.sum(lw_p, axis=0, keepdims=True)
        p_inv = jnp.exp(-cum_p)
        p_end = jnp.exp(tot_p - cum_p)
        k = k_ref[bi, :, sl]
        b = b_ref[bi, :, sl]
        kt.append(kap_ref[bi, :, sl] * jnp.exp(cum_p - lw_p))
        rt.append(r_ref[bi, :, sl] * jnp.exp(cum_p))
        vs.append(v_ref[bi, :, sl])
        kend.append(k * p_end)
        bend.append(b * p_end)
        ptot.append(jnp.exp(tot_p))
        a.append(_bdot_nt(jnp.concatenate([kt[-1], rt[-1]], axis=0),
                          jnp.concatenate([stack(b * p_inv), stack(k * p_inv)], axis=0)))
    a_ab = [jnp.where(strict, a[p][0:c, 0:c2], 0.0) for p in pairs]
    a_ak = [jnp.where(strict, a[p][0:c, c2:2 * c2], 0.0) for p in pairs]
    a_rb = [jnp.where(upto, a[p][c:c2, 0:c2], 0.0) for p in pairs]
    a_rk = [jnp.where(upto, a[p][c:c2, c2:2 * c2], 0.0) for p in pairs]

    blk = (pcol & (c - 1)) ^ prow
    ident = jnp.where(eye, 1.0, 0.0)
    l0 = [jnp.where(blk < INV_BASE, a_ab[p], 0.0) for p in pairs]
    sq = [_bdot(l0[p], stack(l0[p])) for p in pairs]
    t = [ident - l0[p] for p in pairs]
    t = [t[p] + _bdot(t[p], stack(sq[p])) for p in pairs]
    s_blk = INV_BASE
    while s_blk < c:
        off = jnp.logical_and(blk >= s_blk, blk < 2 * s_blk)
        tl = [_bdot(t[p], stack(jnp.where(off, a_ab[p], 0.0))) for p in pairs]
        t = [t[p] - _bdot(tl[p], stack(t[p])) for p in pairs]
        s_blk *= 2

    asv = [_bdot(jnp.concatenate([a_ak[p], a_rk[p]], axis=0), stack(vs[p])) for p in pairs]
    wu = [_bdot(t[p], jnp.concatenate([stack(kt[p]), stack(asv[p][0:c])], axis=1)) for p in pairs]
    w = [wu[p][:, 0:c2] for p in pairs]
    u0 = [wu[p][:, c2:2 * c2] for p in pairs]
    arb = [_bdot(a_rb[p], jnp.concatenate([stack(w[p]), stack(u0[p])], axis=1)) for p in pairs]
    y0 = [asv[p][c:c2] - arb[p][:, c2:2 * c2] for p in pairs]
    y1 = [rt[p] - arb[p][:, 0:c2] for p in pairs]
    m = [jnp.where(eye, ptot[p], 0.0) - pack(_bdot_tn(bend[p], w[p])) for p in pairs]
    nn = [pack(_bdot_tn(jnp.concatenate([kend[p], -bend[p]], axis=0), jnp.concatenate([vs[p], u0[p]], axis=0)))
          for p in pairs]
    for p in pairs:
        h0 = h_ref[p]
        hh = h0.astype(BF16)
        hl = h0 - hh.astype(F32)
        mh, ml = _split2(m[p])
        top = _bdot(jnp.concatenate([mh, ml, y1[p].astype(BF16)], axis=0), stack(hh.astype(F32)))
        y_ref[units[p][0], :, units[p][1]] = top[c2:c2 + c] + y0[p]
        h_ref[p] = top[0:c] + top[c:c2] + _bdot(mh, stack(hl)) + nn[p]


def _rwkv_scan(r, v, kap, lw, kd, bd, *, ctx_len):
    bsz, tt, d = r.shape
    nc = tt // CHUNK
    ncc = ctx_len // CHUNK

    def chunk(dr, s):
        return jnp.where(dr == 0, s, jnp.where(s < ncc, ncc - 1 - s, nc + ncc - 1 - s))

    ub = SCAN_BATCH if bsz % SCAN_BATCH == 0 else 1
    shared = pl.BlockSpec((ub, CHUNK, d), lambda b, dr, s: (b, chunk(dr, s), 0))
    per_dir = pl.BlockSpec((None, ub, CHUNK, d), lambda b, dr, s: (dr, b, chunk(dr, s), 0))
    return pl.pallas_call(
        _scan_kernel,
        grid=(bsz // ub, 2, nc),
        in_specs=[shared, shared, shared, per_dir, per_dir, per_dir],
        out_specs=per_dir,
        out_shape=jax.ShapeDtypeStruct((2, bsz, tt, d), F32),
        scratch_shapes=[pltpu.VMEM((ub * N_PAIR, CHUNK, LANES), F32)],
        compiler_params=_cparams(("parallel", "parallel", "arbitrary")),
        name="rwkv_scan",
    )(r, v, kap, lw, kd, bd)


def _merge_kernel(a_ref, att_ref, conv_ref, y_ref, r_ref, v_ref, kd_ref, lora_ref, ga_ref, gc_ref, gr_ref,
                  mod_ref, g_ref, gg_ref, gb_ref, rk_ref, gup_ref, e_ref, et_ref,
                  wa_ref, wc_ref, wr_ref, wo_ref, o_ref):
    dot = functools.partial(jnp.dot, preferred_element_type=F32)
    inv = 1.0 / RWKV_HEAD
    y = y_ref[0] + y_ref[1]
    mu = _head_sum(y, e_ref, et_ref) * inv
    dlt = y - mu
    var = _head_sum(dlt * dlt, e_ref, et_ref) * inv
    yn = dlt * lax.rsqrt(var + GN_EPS) * gg_ref[...] + gb_ref[...]
    bonus = _head_sum(r_ref[...] * (kd_ref[0] + kd_ref[1]) * rk_ref[...], e_ref, et_ref) * v_ref[...]
    gate = _bdot(_sigmoid(lora_ref[:, 2 * LANES:3 * LANES]), gup_ref[...])
    rw = ((yn + bonus) * gate).astype(BF16)
    m = (_sigmoid(ga_ref[...]) * dot(att_ref[...], wa_ref[...])
         + _sigmoid(gc_ref[...]) * dot(conv_ref[...], wc_ref[...])
         + _sigmoid(gr_ref[...]) * dot(rw, wr_ref[...]))
    z = dot(m.astype(BF16), wo_ref[...])
    zn = z * lax.rsqrt(jnp.mean(z * z, axis=-1, keepdims=True) + EPS) * g_ref[...]
    o_ref[...] = a_ref[...] + mod_ref[2:3, :] * zn


def _merge(a, att, conv, y, r, v, kd, proj, mod3, g, gn_g, gn_b, r_k, gate_up, e, et, wa, wc, wr, wo,
           *, n_ctx_tiles, ctx_row, tm, skip_ctx):
    bsz, tt, d = a.shape
    cg = COL_GATE // d
    off = n_ctx_tiles if skip_ctx else 0
    loc = pl.BlockSpec((None, tm, d), lambda b, i: (b, i, 0))
    one = pl.BlockSpec((None, tm, d), lambda b, i: (b, i + off, 0))
    two = pl.BlockSpec((2, None, tm, d), lambda b, i: (0, b, i + off, 0))
    gate = lambda c: pl.BlockSpec((None, tm, d), lambda b, i: (b, i + off, cg + c))
    full = lambda shape: pl.BlockSpec(shape, lambda b, i: (0,) * len(shape), pipeline_mode=pl.Buffered(1))
    return pl.pallas_call(
        _merge_kernel,
        grid=(bsz, tt // tm - off),
        in_specs=[one, loc, loc, two, one, one, two,
                  pl.BlockSpec((None, tm, LORA_BLOCK), lambda b, i: (b, i + off, COL_LORA // LORA_BLOCK)),
                  gate(0), gate(1), gate(2),
                  pl.BlockSpec((None, N_MOD, d), _mod_row(n_ctx_tiles - off, ctx_row)),
                  full((1, d)), full((1, d)), full((1, d)), full((1, d)), full((LANES, d)),
                  full((d, LANES)), full((LANES, d)),
                  full((d, d)), full((d, d)), full((d, d)), full((d, d))],
        out_specs=loc,
        out_shape=jax.ShapeDtypeStruct((bsz, tt - off * tm, d), F32),
        compiler_params=_cparams(("parallel", "parallel")),
        name="merge",
    )(a, att, conv, y, r, v, kd, proj, proj, proj, proj, mod3, g.reshape(1, d), gn_g.reshape(1, d),
      gn_b.reshape(1, d), r_k.reshape(1, d), gate_up, e, et, wa, wc, wr, wo)


def _ffn_tail_kernel(a_ref, zc_ref, zp_ref, zn_ref, cw_ref, mod_ref, g_ref, wd_ref, o_ref, buf_ref,
                     *, n_ctx_tiles, n_tiles, tm):
    first, last = _edge_flags(pl.program_id(1), n_ctx_tiles, n_tiles)
    h = SUBLANES
    buf_ref[0:h, :] = jnp.where(first, 0.0, zp_ref[...])
    buf_ref[h:h + tm, :] = zc_ref[...]
    buf_ref[h + tm:h + tm + h, :] = jnp.where(last, 0.0, zn_ref[...])

    def conv(sl):
        return (buf_ref[h - 1:h - 1 + tm, sl] * cw_ref[0:1, sl] + buf_ref[h:h + tm, sl] * cw_ref[1:2, sl]
                + buf_ref[h + 1:h + 1 + tm, sl] * cw_ref[2:3, sl])

    gate = conv(slice(0, D_FF))
    val = conv(slice(D_FF, 2 * D_FF))
    u = (gate * _sigmoid(gate) * val).astype(BF16)
    z = jnp.dot(u, wd_ref[...], preferred_element_type=F32)
    zn = z * lax.rsqrt(jnp.mean(z * z, axis=-1, keepdims=True) + EPS) * g_ref[...]
    o_ref[...] = a_ref[...] + mod_ref[5:6, :] * zn


def _ffn_tail(a, z, conv_w, mod3, g, wd, *, n_ctx_tiles, ctx_row, tm):
    bsz, tt, d = a.shape
    f2 = 2 * D_FF
    prev, nxt = _halo_specs(tm, SUBLANES, tt, 0, f2)
    one = pl.BlockSpec((None, tm, d), lambda b, i: (b, i, 0))
    return pl.pallas_call(
        functools.partial(_ffn_tail_kernel, n_ctx_tiles=n_ctx_tiles, n_tiles=tt // tm, tm=tm),
        grid=(bsz, tt // tm),
        in_specs=[one, pl.BlockSpec((None, tm, f2), lambda b, i: (b, i, 0)), prev, nxt,
                  pl.BlockSpec((3, f2), lambda b, i: (0, 0)),
                  pl.BlockSpec((None, N_MOD, d), _mod_row(n_ctx_tiles, ctx_row)),
                  pl.BlockSpec((1, d), lambda b, i: (0, 0)),
                  pl.BlockSpec((D_FF, d), lambda b, i: (0, 0))],
        out_specs=one,
        out_shape=jax.ShapeDtypeStruct((bsz, tt, d), F32),
        scratch_shapes=[pltpu.VMEM((tm + 2 * SUBLANES, f2), F32)],
        compiler_params=_cparams(("parallel", "parallel")),
        name="ffn_tail",
    )(a, z, z, z, conv_w, mod3, g.reshape(1, d), wd)


def _rope_tables(ctx_len, seq):
    rows = seq // GRID_W
    row = jnp.repeat(jnp.arange(rows, dtype=F32), GRID_W)
    col = jnp.tile(jnp.arange(GRID_W, dtype=F32), rows)
    inv_freq = ROPE_THETA ** (-jnp.arange(N_FREQ, dtype=F32) / N_FREQ)
    ang_r = row[:, None] * inv_freq
    ang_c = col[:, None] * inv_freq
    cos = jnp.concatenate([jnp.cos(ang_r), jnp.cos(ang_r), jnp.cos(ang_c), jnp.cos(ang_c)], axis=-1)
    sin = jnp.concatenate([-jnp.sin(ang_r), jnp.sin(ang_r), -jnp.sin(ang_c), jnp.sin(ang_c)], axis=-1)
    cos = jnp.concatenate([jnp.ones((ctx_len, HEAD_DIM), F32), cos], axis=0)
    sin = jnp.concatenate([jnp.zeros((ctx_len, HEAD_DIM), F32), sin], axis=0)
    return cos, sin


def _reorder_w_in(w):
    qkv = w[:, 0:1536]
    glu = w[:, 1536:3584]
    rkv = w[:, 3584:6656]
    lora = w[:, 6656:7040]
    gates = w[:, 7040:10112]
    pad = jnp.zeros((w.shape[0], LANES), w.dtype)
    return jnp.concatenate([qkv, lora, pad, glu, rkv, gates], axis=1).astype(BF16)


def _pad_lora_up(up):
    z = jnp.zeros_like(up[0])
    w = jnp.stack([jnp.concatenate([up[0], z], axis=0), jnp.concatenate([z, up[1]], axis=0)])
    hi = w.astype(BF16)
    return jnp.stack([hi, (w - hi.astype(F32)).astype(BF16)], axis=1)


def kernel(x, c, ctx, c_ctx, w_mod, b_mod, g_pre_mix, g_post_mix, g_pre_ffn, g_post_ffn, w_in, q_norm, k_norm,
           w_attn_o, conv_w, conv_b, conv_ln_g, conv_ln_b, w_conv_o, shift_w, decay_w0, decay_up, iclr_a0,
           iclr_up, gate_up, k_k, k_a, r_k, wkv_gn_g, wkv_gn_b, w_rwkv_o, w_out, w_ffn_up, ffn_conv_w, w_ffn_down):
    bsz, seq, d = x.shape
    ctx_len = ctx.shape[1]
    depth = w_mod.shape[0]
    tm = min(ROW_TILE, ctx_len)
    assert d == D_MODEL and ctx_len % tm == 0 and seq % tm == 0 and ctx_len % CHUNK == 0 and seq % CHUNK == 0
    n_ctx_tiles = ctx_len // tm

    a = jnp.concatenate([ctx, x], axis=1)
    mod_rows = -(-(bsz + 1) // SUBLANES) * SUBLANES
    cc = jnp.zeros((mod_rows, d), F32).at[:bsz].set(c).at[bsz].set(c_ctx)
    cos, sin = _rope_tables(ctx_len, seq)
    head = jnp.arange(d, dtype=jnp.int32) // RWKV_HEAD
    e = (head[:, None] == jnp.arange(LANES, dtype=jnp.int32)[None, :]).astype(BF16)
    et = e.T
    def seg_rows(segs, n_ctx):
        return jnp.array([bsz if s % segs < n_ctx else s // segs for s in range(bsz * segs)], jnp.int32)

    for l in range(depth):
        last = l == depth - 1
        mod3 = _mod_call(cc, w_mod[l], b_mod[l]).reshape(mod_rows, N_MOD, d)
        modseg = jnp.take(mod3, seg_rows((ctx_len + seq) // tm, n_ctx_tiles), axis=0)
        proj = _nm_matmul(a, modseg, g_pre_mix[l], _reorder_w_in(w_in[l]), shift_idx=0, tn=PROJ_TN, seg=tm,
                          max_seg=4)
        q, k, v = _qk_prep(proj, cos, sin, q_norm[l], k_norm[l], tm=tm)
        att = _attention(q, k, v, ctx_len=ctx_len, tq=tm, skip_ctx=last)
        conv = _conformer(proj, conv_w[l], conv_b[l], conv_ln_g[l], conv_ln_b[l], ctx_len=ctx_len, tm=tm,
                          skip_ctx=last)
        r, vv, kap, lw, kd, bd = _rwkv_prep(proj, shift_w[l], k_k[l], k_a[l], decay_w0[l], _pad_lora_up(decay_up[l]),
                                            iclr_a0[l], _pad_lora_up(iclr_up[l]), e, et, ctx_len=ctx_len, tm=tm)
        y = _rwkv_scan(r, vv, kap, lw, kd, bd, ctx_len=ctx_len)
        a = _merge(a, att, conv, y, r, vv, kd, proj, mod3, g_post_mix[l], wkv_gn_g[l], wkv_gn_b[l], r_k[l],
                   gate_up[l].astype(BF16), e, et, w_attn_o[l].astype(BF16), w_conv_o[l].astype(BF16),
                   w_rwkv_o[l].astype(BF16), w_out[l].astype(BF16), n_ctx_tiles=n_ctx_tiles, ctx_row=bsz, tm=tm,
                   skip_ctx=last)
        n_ctx = 0 if last else n_ctx_tiles
        if last:
            modseg = jnp.take(mod3, seg_rows(seq // tm, 0), axis=0)
        z = _nm_matmul(a, modseg, g_pre_ffn[l], w_ffn_up[l].astype(BF16), shift_idx=3, tn=D_FF // 2, seg=tm,
                       max_seg=4)
        a = _ffn_tail(a, z, ffn_conv_w[l], mod3, g_post_ffn[l], w_ffn_down[l].astype(BF16),
                      n_ctx_tiles=n_ctx, ctx_row=bsz, tm=tm)
    return a
```

```python
import functools
import math

import jax
import jax.numpy as jnp
from jax import lax
from jax.experimental import pallas as pl
from jax.experimental.pallas import tpu as pltpu

F32 = jnp.float32
BF16 = jnp.bfloat16

D_MODEL = 1024
GRID_W = 64
N_Q_HEADS = 8
N_KV_HEADS = 2
GQA_GROUP = N_Q_HEADS // N_KV_HEADS
HEAD_DIM = 128
N_FREQ = HEAD_DIM // 4
ROPE_THETA = 10000.0
CONV_K = 31
CONV_HALO = 16
RWKV_HEAD = 64
RWKV_HEADS = D_MODEL // RWKV_HEAD
DECAY_SCALE = math.exp(-0.5)
D_FF = 2816
N_MOD = 6
EPS = 1e-6
LN_EPS = 1e-5
GN_EPS = RWKV_HEAD * 1e-5
SOFTMAX_C = (HEAD_DIM ** -0.5) * math.log2(math.e)
LANES = 128
SUBLANES = 8
CHUNK = 64
N_PAIR = D_MODEL // LANES
SCAN_BATCH = 2
INV_BASE = 4
NEUMANN_RHO = 1.0
ROW_TILE = 256
CONV_ROWS = 32
MOD_TN = 1536
PROJ_TN = 2048
LORA_BLOCK = 512
ATTN_SPAN = 256
VMEM_LIMIT = 48 * 1024 * 1024

COL_Q = 0
COL_KV = 1024
COL_LORA = 1536
COL_GLU = 2048
COL_RKV = 4096
COL_GATE = 7168
N_IN_PAD = 10240


def _cparams(sem):
    return pltpu.CompilerParams(dimension_semantics=sem, vmem_limit_bytes=VMEM_LIMIT)


def _bdot(a, b):
    return jnp.dot(a.astype(BF16), b.astype(BF16), preferred_element_type=F32)


def _bdot_nt(a, b):
    return lax.dot_general(a.astype(BF16), b.astype(BF16), (((1,), (1,)), ((), ())),
                           preferred_element_type=F32)


def _bdot_tn(a, b):
    return lax.dot_general(a.astype(BF16), b.astype(BF16), (((0,), (0,)), ((), ())),
                           preferred_element_type=F32)


def _split2(x):
    hi = x.astype(BF16)
    lo = (x - hi.astype(F32)).astype(BF16)
    return hi, lo


def _dot3_presplit(a, bh, bl):
    ah, al = _split2(a)
    dot = functools.partial(jnp.dot, preferred_element_type=F32)
    return dot(ah, bh) + dot(al, bh) + dot(ah, bl)


def _dot3(a, b):
    return _dot3_presplit(a, *_split2(b))


def _sigmoid(x):
    return jax.nn.sigmoid(x)


def _head_sum(x, e_ref, et_ref):
    xh, xl = _split2(x)
    dot = functools.partial(jnp.dot, preferred_element_type=F32)
    s = dot(xh, e_ref[...]) + dot(xl, e_ref[...])
    sh, sl = _split2(s)
    return dot(sh, et_ref[...]) + dot(sl, et_ref[...])


def _mod_kernel(c_ref, w_ref, b_ref, o_ref):
    c = c_ref[...]
    o_ref[...] = _dot3(c * _sigmoid(c), w_ref[...]) + b_ref[...]


def _mod_call(cc, w_mod, b_mod):
    rows = cc.shape[0]
    n = w_mod.shape[1]
    tn = MOD_TN
    return pl.pallas_call(
        _mod_kernel,
        grid=(n // tn,),
        in_specs=[pl.BlockSpec((rows, D_MODEL), lambda j: (0, 0)),
                  pl.BlockSpec((D_MODEL, tn), lambda j: (0, j)),
                  pl.BlockSpec((1, tn), lambda j: (0, j))],
        out_specs=pl.BlockSpec((rows, tn), lambda j: (0, j)),
        out_shape=jax.ShapeDtypeStruct((rows, n), F32),
        compiler_params=_cparams(("parallel",)),
        name="mod",
    )(cc, w_mod, b_mod.reshape(1, n))


def _mod_row(n_ctx_tiles, ctx_row):
    return lambda b, i, *_: (jnp.where(i < n_ctx_tiles, ctx_row, b), 0, 0)


def _nm_matmul_kernel(a_ref, mod_ref, g_ref, w_ref, o_ref, h_ref, *, shift_idx, seg, nseg):
    @pl.when(pl.program_id(1) == 0)
    def _():
        for s in range(nseg):
            rows = slice(s * seg, (s + 1) * seg)
            x = a_ref[rows, :]
            y = x * lax.rsqrt(jnp.mean(x * x, axis=-1, keepdims=True) + EPS) * g_ref[...]
            h = y * (1.0 + mod_ref[s, shift_idx + 1:shift_idx + 2, :]) + mod_ref[s, shift_idx:shift_idx + 1, :]
            h_ref[rows, :] = h.astype(BF16)

    o_ref[...] = jnp.dot(h_ref[...], w_ref[...], preferred_element_type=F32).astype(o_ref.dtype)


def _nm_matmul(a, modseg, g, w, *, shift_idx, tn, seg, max_seg):
    bsz, tt, d = a.shape
    n = w.shape[1]
    rows = bsz * tt
    nseg = max(s for s in (4, 2, 1) if s <= max_seg and (rows // seg) % s == 0)
    tm = seg * nseg
    out = pl.pallas_call(
        functools.partial(_nm_matmul_kernel, shift_idx=shift_idx, seg=seg, nseg=nseg),
        grid=(rows // tm, n // tn),
        in_specs=[pl.BlockSpec((tm, d), lambda i, j: (i, 0)),
                  pl.BlockSpec((nseg, N_MOD, d), lambda i, j: (i, 0, 0)),
                  pl.BlockSpec((1, d), lambda i, j: (0, 0)),
                  pl.BlockSpec((d, tn), lambda i, j: (0, j))],
        out_specs=pl.BlockSpec((tm, tn), lambda i, j: (i, j)),
        out_shape=jax.ShapeDtypeStruct((rows, n), F32),
        scratch_shapes=[pltpu.VMEM((tm, d), BF16)],
        compiler_params=_cparams(("parallel", "arbitrary")),
        name="norm_mod_matmul",
    )(a.reshape(rows, d), modseg, g.reshape(1, d), w)
    return out.reshape(bsz, tt, n)


def _qk_prep_kernel(q_ref, kv_ref, cos_ref, sin_ref, qn_ref, kn_ref, qo_ref, ko_ref, vo_ref):
    cos = cos_ref[...]
    sin = sin_ref[...]
    lane = lax.broadcasted_iota(jnp.int32, cos.shape, 1)
    first = (lane & (N_FREQ)) == 0

    heads = ([(q_ref, h, qn_ref, qo_ref, SOFTMAX_C) for h in range(N_Q_HEADS)]
             + [(kv_ref, h, kn_ref, ko_ref, 1.0) for h in range(N_KV_HEADS)])
    sls = [slice(h * HEAD_DIM, (h + 1) * HEAD_DIM) for _, h, _, _, _ in heads]
    xs = [src[:, sl] for (src, _, _, _, _), sl in zip(heads, sls)]
    ms = [jnp.mean(x * x, axis=-1, keepdims=True) for x in xs]
    ys = [x * lax.rsqrt(m + EPS) * (g[...] * c) for x, m, (_, _, g, _, c) in zip(xs, ms, heads)]
    up = [pltpu.roll(y, LANES - N_FREQ, 1) for y in ys]
    dn = [pltpu.roll(y, N_FREQ, 1) for y in ys]
    for y, u, d, (_, _, _, dst, _), sl in zip(ys, up, dn, heads, sls):
        dst[:, sl] = (y * cos + jnp.where(first, u, d) * sin).astype(BF16)
    vo_ref[...] = kv_ref[:, N_KV_HEADS * HEAD_DIM:].astype(BF16)


def _qk_prep(proj, cos, sin, q_norm, k_norm, *, tm):
    bsz, tt, _ = proj.shape
    dq = N_Q_HEADS * HEAD_DIM
    dkv = N_KV_HEADS * HEAD_DIM
    return pl.pallas_call(
        _qk_prep_kernel,
        grid=(bsz, tt // tm),
        in_specs=[pl.BlockSpec((None, tm, dq), lambda b, i: (b, i, COL_Q // dq)),
                  pl.BlockSpec((None, tm, 2 * dkv), lambda b, i: (b, i, COL_KV // (2 * dkv))),
                  pl.BlockSpec((tm, HEAD_DIM), lambda b, i: (i, 0)),
                  pl.BlockSpec((tm, HEAD_DIM), lambda b, i: (i, 0)),
                  pl.BlockSpec((1, HEAD_DIM), lambda b, i: (0, 0)),
                  pl.BlockSpec((1, HEAD_DIM), lambda b, i: (0, 0))],
        out_specs=[pl.BlockSpec((None, tm, dq), lambda b, i: (b, i, 0)),
                   pl.BlockSpec((None, tm, dkv), lambda b, i: (b, i, 0)),
                   pl.BlockSpec((None, tm, dkv), lambda b, i: (b, i, 0))],
        out_shape=[jax.ShapeDtypeStruct((bsz, tt, dq), BF16),
                   jax.ShapeDtypeStruct((bsz, tt, dkv), BF16),
                   jax.ShapeDtypeStruct((bsz, tt, dkv), BF16)],
        compiler_params=_cparams(("parallel", "parallel")),
        name="qk_prep",
    )(proj, proj, cos, sin, q_norm.reshape(1, HEAD_DIM), k_norm.reshape(1, HEAD_DIM))


def _attn_kernel(q_ref, k_ref, v_ref, o_ref, *, n_ctx_tiles, ctx_len, skip_ctx):
    tq = q_ref.shape[0]
    tt = k_ref.shape[0]

    def attend(spans):
        q = jnp.concatenate([q_ref[:, g * HEAD_DIM:(g + 1) * HEAD_DIM] for g in range(GQA_GROUP)], axis=0)
        m = l = acc = None
        for start, size in spans:
            k = k_ref[start:start + size, :]
            v = v_ref[start:start + size, :]
            s = lax.dot_general(q, k, (((1,), (1,)), ((), ())), preferred_element_type=F32)
            mt = jnp.max(s, axis=-1, keepdims=True)
            if m is None:
                m = mt
                p = jnp.exp2(s - m)
                l = jnp.sum(p, axis=-1, keepdims=True)
                acc = jnp.dot(p.astype(BF16), v, preferred_element_type=F32)
            else:
                m_new = jnp.maximum(m, mt)
                alpha = jnp.exp2(m - m_new)
                p = jnp.exp2(s - m_new)
                l = alpha * l + jnp.sum(p, axis=-1, keepdims=True)
                acc = alpha * acc + jnp.dot(p.astype(BF16), v, preferred_element_type=F32)
                m = m_new
        o = acc / l
        for g in range(GQA_GROUP):
            o_ref[:, g * HEAD_DIM:(g + 1) * HEAD_DIM] = o[g * tq:(g + 1) * tq].astype(o_ref.dtype)

    x_spans = [(start, min(ATTN_SPAN, tt - start)) for start in range(0, tt, ATTN_SPAN)]
    if skip_ctx:
        attend(x_spans)
        return
    i = pl.program_id(2)

    @pl.when(i < n_ctx_tiles)
    def _():
        attend([(0, ctx_len)])

    @pl.when(i >= n_ctx_tiles)
    def _():
        attend(x_spans)


def _attention(q, k, v, *, ctx_len, tq, skip_ctx):
    bsz, tt, dq = q.shape
    gw = GQA_GROUP * HEAD_DIM
    off = ctx_len // tq if skip_ctx else 0
    return pl.pallas_call(
        functools.partial(_attn_kernel, n_ctx_tiles=ctx_len // tq, ctx_len=ctx_len, skip_ctx=skip_ctx),
        grid=(bsz, N_KV_HEADS, tt // tq - off),
        in_specs=[pl.BlockSpec((None, tq, gw), lambda b, h, i: (b, i + off, h)),
                  pl.BlockSpec((None, tt, HEAD_DIM), lambda b, h, i: (b, 0, h)),
                  pl.BlockSpec((None, tt, HEAD_DIM), lambda b, h, i: (b, 0, h))],
        out_specs=pl.BlockSpec((None, tq, gw), lambda b, h, i: (b, i, h)),
        out_shape=jax.ShapeDtypeStruct((bsz, tt - off * tq, dq), BF16),
        compiler_params=_cparams(("parallel", "parallel", "arbitrary")),
        name="attention",
    )(q, k, v)


def _halo_specs(tm, halo, tt, col_block, width, off=0):
    per = tm // halo
    last = tt // halo - 1
    prev = pl.BlockSpec((None, halo, width), lambda b, i: (b, jnp.maximum((i + off) * per - 1, 0), col_block))
    nxt = pl.BlockSpec((None, halo, width), lambda b, i: (b, jnp.minimum((i + off + 1) * per, last), col_block))
    return prev, nxt


def _edge_flags(i, n_ctx_tiles, n_tiles):
    first = jnp.logical_or(i == 0, i == n_ctx_tiles)
    last = jnp.logical_or(i == n_ctx_tiles - 1, i == n_tiles - 1)
    return first, last


def _conformer_kernel(cur_ref, prev_ref, next_ref, w_ref, cb_ref, lg_ref, lb_ref,
                      o_ref, buf_ref, sh_ref, *, n_ctx_tiles, n_tiles, tm, rc, tile0):
    first, last = _edge_flags(pl.program_id(1) + tile0, n_ctx_tiles, n_tiles)
    h = CONV_HALO
    d = D_MODEL

    def glu(ref):
        return ref[:, 0:d] * _sigmoid(ref[:, d:2 * d])

    buf_ref[0:h, :] = jnp.where(first, 0.0, glu(prev_ref))
    buf_ref[h:h + tm, :] = glu(cur_ref)
    buf_ref[h + tm:h + tm + h, :] = jnp.where(last, 0.0, glu(next_ref))
    n_sh = sh_ref.shape[1]
    for s in range(1, SUBLANES):
        sh_ref[s - 1] = buf_ref[s:s + n_sh, :]
    lead = h - CONV_K // 2
    for c in range(tm // rc):
        acc = jnp.zeros((rc, D_MODEL), F32)
        for j in range(CONV_K):
            s = (j + lead) % SUBLANES
            r0 = c * rc + (j + lead) - s
            tap = buf_ref[r0:r0 + rc, :] if s == 0 else sh_ref[s - 1, r0:r0 + rc, :]
            acc = acc + tap * jnp.concatenate([w_ref[j]] * (rc // SUBLANES), axis=0)
        y = acc + cb_ref[...]
        mu = jnp.mean(y, axis=-1, keepdims=True)
        dlt = y - mu
        var = jnp.mean(dlt * dlt, axis=-1, keepdims=True)
        z = dlt * lax.rsqrt(var + LN_EPS) * lg_ref[...] + lb_ref[...]
        o_ref[c * rc:(c + 1) * rc, :] = (z * _sigmoid(z)).astype(o_ref.dtype)


def _conformer(proj, conv_w, conv_b, ln_g, ln_b, *, ctx_len, tm, skip_ctx):
    bsz, tt, _ = proj.shape
    d = D_MODEL
    cg = COL_GLU // (2 * d)
    off = ctx_len // tm if skip_ctx else 0
    prev, nxt = _halo_specs(tm, CONV_HALO, tt, cg, 2 * d, off)
    vec = lambda: pl.BlockSpec((1, d), lambda b, i: (0, 0))
    return pl.pallas_call(
        functools.partial(_conformer_kernel, n_ctx_tiles=ctx_len // tm, n_tiles=tt // tm, tm=tm, rc=CONV_ROWS,
                          tile0=off),
        grid=(bsz, tt // tm - off),
        in_specs=[pl.BlockSpec((None, tm, 2 * d), lambda b, i: (b, i + off, cg)), prev, nxt,
                  pl.BlockSpec((CONV_K, SUBLANES, d), lambda b, i: (0, 0, 0)),
                  vec(), vec(), vec()],
        out_specs=pl.BlockSpec((None, tm, d), lambda b, i: (b, i, 0)),
        out_shape=jax.ShapeDtypeStruct((bsz, tt - off * tm, d), BF16),
        scratch_shapes=[pltpu.VMEM((tm + 2 * CONV_HALO, d), F32),
                        pltpu.VMEM((SUBLANES - 1, tm + 2 * CONV_HALO - SUBLANES, d), F32)],
        compiler_params=_cparams(("parallel", "parallel")),
        name="conformer",
    )(proj, proj, proj, jnp.broadcast_to(conv_w[:, None, :], (CONV_K, SUBLANES, d)),
      conv_b.reshape(1, d), ln_g.reshape(1, d), ln_b.reshape(1, d))


def _rwkv_prep_kernel(rc_ref, kc_ref, vc_ref, rp_ref, kp_ref, vp_ref, rn_ref, kn_ref, vn_ref, lora_ref,
                      sw_ref, kk_ref, ka_ref, w0_ref, dup_ref, a0_ref, iup_ref, e_ref, et_ref,
                      r_ref, v_ref, kap_ref, lw_ref, kd_ref, bd_ref, buf_ref,
                      *, n_ctx_tiles, n_tiles, tm):
    first, last = _edge_flags(pl.program_id(1), n_ctx_tiles, n_tiles)
    h = SUBLANES
    d = D_MODEL

    def shift(cur_ref, prev_ref, next_ref, col):
        buf_ref[0:h, :] = jnp.where(first, 0.0, prev_ref[...])
        buf_ref[h:h + tm, :] = cur_ref[...]
        buf_ref[h + tm:h + tm + h, :] = jnp.where(last, 0.0, next_ref[...])
        sl = slice(col * d, (col + 1) * d)
        return (buf_ref[h - 1:h - 1 + tm, :] * sw_ref[0:1, sl] + buf_ref[h:h + tm, :] * sw_ref[1:2, sl]
                + buf_ref[h + 1:h + 1 + tm, :] * sw_ref[2:3, sl])

    r = shift(rc_ref, rp_ref, rn_ref, 0)
    k = shift(kc_ref, kp_ref, kn_ref, 1)
    v = shift(vc_ref, vp_ref, vn_ref, 2)
    r_ref[...] = r
    v_ref[...] = v
    kk = k * kk_ref[...]
    ss = _head_sum(kk * kk, e_ref, et_ref)
    kap = kk * lax.rsqrt(jnp.maximum(ss, 1e-12))
    kap_ref[...] = kap
    tw = jnp.tanh(lora_ref[:, 0:LANES])
    la = lora_ref[:, LANES:2 * LANES]
    for dr in range(2):
        z = w0_ref[dr:dr + 1, :] + _dot3_presplit(tw, dup_ref[dr, 0], dup_ref[dr, 1])
        lw_ref[dr] = -DECAY_SCALE * _sigmoid(z)
        a = _sigmoid(a0_ref[dr:dr + 1, :] + _dot3_presplit(la, iup_ref[dr, 0], iup_ref[dr, 1]))
        kd_ref[dr] = k * (1.0 + (a - 1.0) * ka_ref[...])
        bd_ref[dr] = a * kap


def _rwkv_prep(proj, shift_w, k_k, k_a, decay_w0, decay_up_pad, iclr_a0, iclr_up_pad, e, et, *, ctx_len, tm):
    bsz, tt, _ = proj.shape
    d = D_MODEL
    c0 = COL_RKV // d
    cur = lambda c: pl.BlockSpec((None, tm, d), lambda b, i: (b, i, c))
    halos = [_halo_specs(tm, SUBLANES, tt, c0 + c, d) for c in range(3)]
    full = lambda shape: pl.BlockSpec(shape, lambda b, i: (0,) * len(shape))
    out1 = pl.BlockSpec((None, tm, d), lambda b, i: (b, i, 0))
    out2 = pl.BlockSpec((2, None, tm, d), lambda b, i: (0, b, i, 0))
    s1 = jax.ShapeDtypeStruct((bsz, tt, d), F32)
    s2 = jax.ShapeDtypeStruct((2, bsz, tt, d), F32)
    return pl.pallas_call(
        functools.partial(_rwkv_prep_kernel, n_ctx_tiles=ctx_len // tm, n_tiles=tt // tm, tm=tm),
        grid=(bsz, tt // tm),
        in_specs=[cur(c0), cur(c0 + 1), cur(c0 + 2),
                  halos[0][0], halos[1][0], halos[2][0], halos[0][1], halos[1][1], halos[2][1],
                  pl.BlockSpec((None, tm, LORA_BLOCK), lambda b, i: (b, i, COL_LORA // LORA_BLOCK)),
                  full((3, 3 * d)), full((1, d)), full((1, d)), full((2, d)), full((2, 2, LANES, d)),
                  full((2, d)), full((2, 2, LANES, d)), full((d, LANES)), full((LANES, d))],
        out_specs=[out1, out1, out1, out2, out2, out2],
        out_shape=[s1, s1, s1, s2, s2, s2],
        scratch_shapes=[pltpu.VMEM((tm + 2 * SUBLANES, d), F32)],
        compiler_params=_cparams(("parallel", "parallel")),
        name="rwkv_prep",
    )(proj, proj, proj, proj, proj, proj, proj, proj, proj, proj,
      shift_w, k_k.reshape(1, d), k_a.reshape(1, d), decay_w0, decay_up_pad, iclr_a0, iclr_up_pad, e, et)


def _scan_kernel(r_ref, v_ref, kap_ref, lw_ref, k_ref, b_ref, y_ref, h_ref, t_ref):
    c = CHUNK
    c2 = 2 * c
    sgn = 1 - 2 * pl.program_id(1)

    @pl.when(pl.program_id(2) == 0)
    def _():
        h_ref[...] = jnp.zeros_like(h_ref)

    row = lax.broadcasted_iota(jnp.int32, (c, c), 0)
    col = lax.broadcasted_iota(jnp.int32, (c, c), 1)
    incl = jnp.where((col - row) * sgn <= 0, 1.0, 0.0).astype(BF16)
    dot = functools.partial(jnp.dot, preferred_element_type=F32)
    nb = lw_ref.shape[0]
    cum = []
    for bi in range(nb):
        lw = lw_ref[bi]
        hi = lw.astype(BF16)
        rem = lw - hi.astype(F32)
        mid = rem.astype(BF16)
        lo = (rem - mid.astype(F32)).astype(BF16)
        cum.append(dot(incl, hi) + dot(incl, mid) + dot(incl, lo))

    prow = lax.broadcasted_iota(jnp.int32, (c, c2), 0)
    pcol = lax.broadcasted_iota(jnp.int32, (c, c2), 1)
    order = ((pcol & (c - 1)) - prow) * sgn
    strict = order < 0
    upto = order <= 0
    eye = order == 0
    head0 = pcol < RWKV_HEAD

    def stack(x):
        return jnp.concatenate([jnp.where(head0, x, 0.0), jnp.where(head0, 0.0, x)], axis=0)

    def pack(x):
        return jnp.where(head0, x[0:c], x[c:c2])

    units = [(bi, slice(p * LANES, (p + 1) * LANES)) for bi in range(nb) for p in range(N_PAIR)]
    pairs = list(range(len(units)))
    kt, rt, vs, kend, bend, ptot, a = [], [], [], [], [], [], []
    for bi, sl in units:
        cum_p = cum[bi][:, sl]
        lw_p = lw_ref[bi, :, sl]
        tot_p = jnp.sum(lw_p, axis=0, keepdims=True)
        p_inv = jnp.exp(-cum_p)
        p_end = jnp.exp(tot_p - cum_p)
        k = k_ref[bi, :, sl]
        b = b_ref[bi, :, sl]
        kt.append(kap_ref[bi, :, sl] * jnp.exp(cum_p - lw_p))
        rt.append(r_ref[bi, :, sl] * jnp.exp(cum_p))
        vs.append(v_ref[bi, :, sl])
        kend.append(k * p_end)
        bend.append(b * p_end)
        ptot.append(jnp.exp(tot_p))
        a.append(_bdot_nt(jnp.concatenate([kt[-1], rt[-1]], axis=0),
                          jnp.concatenate([stack(b * p_inv), stack(k * p_inv)], axis=0)))
    a_ab = [jnp.where(strict, a[p][0:c, 0:c2], 0.0) for p in pairs]
    a_ak = [jnp.where(strict, a[p][0:c, c2:2 * c2], 0.0) for p in pairs]
    a_rb = [jnp.where(upto, a[p][c:c2, 0:c2], 0.0) for p in pairs]
    a_rk = [jnp.where(upto, a[p][c:c2, c2:2 * c2], 0.0) for p in pairs]

    asv = [_bdot(jnp.concatenate([a_ak[p], a_rk[p]], axis=0), stack(vs[p])) for p in pairs]

    rho = None
    for p in pairs:
        mag = jnp.abs(a_ab[p])
        for part in (jnp.where(head0, mag, 0.0), jnp.where(head0, 0.0, mag)):
            rs = jnp.sum(part, axis=-1, keepdims=True)
            rho = rs if rho is None else jnp.maximum(rho, rs)
    tame = jnp.max(rho) <= NEUMANN_RHO
    ident = jnp.where(eye, 1.0, 0.0)

    @pl.when(tame)
    def _():
        t = [ident - a_ab[p] for p in pairs]
        x = [_bdot(a_ab[p], stack(a_ab[p])) for p in pairs]
        n = 2
        while 2 * n < c:
            xt = [_bdot(jnp.concatenate([x[p], t[p]], axis=0), stack(x[p])) for p in pairs]
            t = [t[p] + xt[p][c:c2] for p in pairs]
            x = [xt[p][0:c] for p in pairs]
            n *= 2
        for p in pairs:
            t_ref[p] = t[p] + _bdot(t[p], stack(x[p]))

    @pl.when(jnp.logical_not(tame))
    def _():
        blk = (pcol & (c - 1)) ^ prow
        l0 = [jnp.where(blk < INV_BASE, a_ab[p], 0.0) for p in pairs]
        sq = [_bdot(l0[p], stack(l0[p])) for p in pairs]
        t = [ident - l0[p] for p in pairs]
        t = [t[p] + _bdot(t[p], stack(sq[p])) for p in pairs]
        s_blk = INV_BASE
        while s_blk < c:
            off = jnp.logical_and(blk >= s_blk, blk < 2 * s_blk)
            tl = [_bdot(t[p], stack(jnp.where(off, a_ab[p], 0.0))) for p in pairs]
            t = [t[p] - _bdot(tl[p], stack(t[p])) for p in pairs]
            s_blk *= 2
        for p in pairs:
            t_ref[p] = t[p]

    t = [t_ref[p] for p in pairs]
    wu = [_bdot(t[p], jnp.concatenate([stack(kt[p]), stack(asv[p][0:c])], axis=1)) for p in pairs]
    w = [wu[p][:, 0:c2] for p in pairs]
    u0 = [wu[p][:, c2:2 * c2] for p in pairs]
    arb = [_bdot(a_rb[p], jnp.concatenate([stack(w[p]), stack(u0[p])], axis=1)) for p in pairs]
    y0 = [asv[p][c:c2] - arb[p][:, c2:2 * c2] for p in pairs]
    y1 = [rt[p] - arb[p][:, 0:c2] for p in pairs]
    m = [jnp.where(eye, ptot[p], 0.0) - pack(_bdot_tn(bend[p], w[p])) for p in pairs]
    nn = [pack(_bdot_tn(jnp.concatenate([kend[p], -bend[p]], axis=0), jnp.concatenate([vs[p], u0[p]], axis=0)))
          for p in pairs]
    for p in pairs:
        h0 = h_ref[p]
        hh = h0.astype(BF16)
        hl = h0 - hh.astype(F32)
        mh, ml = _split2(m[p])
        top = _bdot(jnp.concatenate([mh, ml, y1[p].astype(BF16)], axis=0), stack(hh.astype(F32)))
        y_ref[units[p][0], :, units[p][1]] = top[c2:c2 + c] + y0[p]
        h_ref[p] = top[0:c] + top[c:c2] + _bdot(mh, stack(hl)) + nn[p]


def _rwkv_scan(r, v, kap, lw, kd, bd, *, ctx_len):
    bsz, tt, d = r.shape
    nc = tt // CHUNK
    ncc = ctx_len // CHUNK

    def chunk(dr, s):
        return jnp.where(dr == 0, s, jnp.where(s < ncc, ncc - 1 - s, nc + ncc - 1 - s))

    ub = SCAN_BATCH if bsz % SCAN_BATCH == 0 else 1
    shared = pl.BlockSpec((ub, CHUNK, d), lambda b, dr, s: (b, chunk(dr, s), 0))
    per_dir = pl.BlockSpec((None, ub, CHUNK, d), lambda b, dr, s: (dr, b, chunk(dr, s), 0))
    return pl.pallas_call(
        _scan_kernel,
        grid=(bsz // ub, 2, nc),
        in_specs=[shared, shared, shared, per_dir, per_dir, per_dir],
        out_specs=per_dir,
        out_shape=jax.ShapeDtypeStruct((2, bsz, tt, d), F32),
        scratch_shapes=[pltpu.VMEM((ub * N_PAIR, CHUNK, LANES), F32),
                        pltpu.VMEM((ub * N_PAIR, CHUNK, LANES), F32)],
        compiler_params=_cparams(("parallel", "parallel", "arbitrary")),
        name="rwkv_scan",
    )(r, v, kap, lw, kd, bd)


def _merge_kernel(a_ref, att_ref, conv_ref, y_ref, r_ref, v_ref, kd_ref, lora_ref, ga_ref, gc_ref, gr_ref,
                  mod_ref, g_ref, gg_ref, gb_ref, rk_ref, gup_ref, e_ref, et_ref,
                  wa_ref, wc_ref, wr_ref, wo_ref, o_ref):
    dot = functools.partial(jnp.dot, preferred_element_type=F32)
    inv = 1.0 / RWKV_HEAD
    y = y_ref[0] + y_ref[1]
    mu = _head_sum(y, e_ref, et_ref) * inv
    dlt = y - mu
    var = _head_sum(dlt * dlt, e_ref, et_ref) * inv
    yn = dlt * lax.rsqrt(var + GN_EPS) * gg_ref[...] + gb_ref[...]
    bonus = _head_sum(r_ref[...] * (kd_ref[0] + kd_ref[1]) * rk_ref[...], e_ref, et_ref) * v_ref[...]
    gate = _bdot(_sigmoid(lora_ref[:, 2 * LANES:3 * LANES]), gup_ref[...])
    rw = ((yn + bonus) * gate).astype(BF16)
    m = (_sigmoid(ga_ref[...]) * dot(att_ref[...], wa_ref[...])
         + _sigmoid(gc_ref[...]) * dot(conv_ref[...], wc_ref[...])
         + _sigmoid(gr_ref[...]) * dot(rw, wr_ref[...]))
    z = dot(m.astype(BF16), wo_ref[...])
    zn = z * lax.rsqrt(jnp.mean(z * z, axis=-1, keepdims=True) + EPS) * g_ref[...]
    o_ref[...] = a_ref[...] + mod_ref[2:3, :] * zn


def _merge(a, att, conv, y, r, v, kd, proj, mod3, g, gn_g, gn_b, r_k, gate_up, e, et, wa, wc, wr, wo,
           *, n_ctx_tiles, ctx_row, tm, skip_ctx):
    bsz, tt, d = a.shape
    cg = COL_GATE // d
    off = n_ctx_tiles if skip_ctx else 0
    loc = pl.BlockSpec((None, tm, d), lambda b, i: (b, i, 0))
    one = pl.BlockSpec((None, tm, d), lambda b, i: (b, i + off, 0))
    two = pl.BlockSpec((2, None, tm, d), lambda b, i: (0, b, i + off, 0))
    gate = lambda c: pl.BlockSpec((None, tm, d), lambda b, i: (b, i + off, cg + c))
    full = lambda shape: pl.BlockSpec(shape, lambda b, i: (0,) * len(shape), pipeline_mode=pl.Buffered(1))
    return pl.pallas_call(
        _merge_kernel,
        grid=(bsz, tt // tm - off),
        in_specs=[one, loc, loc, two, one, one, two,
                  pl.BlockSpec((None, tm, LORA_BLOCK), lambda b, i: (b, i + off, COL_LORA // LORA_BLOCK)),
                  gate(0), gate(1), gate(2),
                  pl.BlockSpec((None, N_MOD, d), _mod_row(n_ctx_tiles - off, ctx_row)),
                  full((1, d)), full((1, d)), full((1, d)), full((1, d)), full((LANES, d)),
                  full((d, LANES)), full((LANES, d)),
                  full((d, d)), full((d, d)), full((d, d)), full((d, d))],
        out_specs=loc,
        out_shape=jax.ShapeDtypeStruct((bsz, tt - off * tm, d), F32),
        compiler_params=_cparams(("parallel", "parallel")),
        name="merge",
    )(a, att, conv, y, r, v, kd, proj, proj, proj, proj, mod3, g.reshape(1, d), gn_g.reshape(1, d),
      gn_b.reshape(1, d), r_k.reshape(1, d), gate_up, e, et, wa, wc, wr, wo)


def _ffn_tail_kernel(a_ref, zc_ref, zp_ref, zn_ref, cw_ref, mod_ref, g_ref, wd_ref, o_ref, buf_ref,
                     *, n_ctx_tiles, n_tiles, tm):
    first, last = _edge_flags(pl.program_id(1), n_ctx_tiles, n_tiles)
    h = SUBLANES
    buf_ref[0:h, :] = jnp.where(first, 0.0, zp_ref[...])
    buf_ref[h:h + tm, :] = zc_ref[...]
    buf_ref[h + tm:h + tm + h, :] = jnp.where(last, 0.0, zn_ref[...])

    def conv(sl):
        return (buf_ref[h - 1:h - 1 + tm, sl] * cw_ref[0:1, sl] + buf_ref[h:h + tm, sl] * cw_ref[1:2, sl]
                + buf_ref[h + 1:h + 1 + tm, sl] * cw_ref[2:3, sl])

    gate = conv(slice(0, D_FF))
    val = conv(slice(D_FF, 2 * D_FF))
    u = (gate * _sigmoid(gate) * val).astype(BF16)
    z = jnp.dot(u, wd_ref[...], preferred_element_type=F32)
    zn = z * lax.rsqrt(jnp.mean(z * z, axis=-1, keepdims=True) + EPS) * g_ref[...]
    o_ref[...] = a_ref[...] + mod_ref[5:6, :] * zn


def _ffn_tail(a, z, conv_w, mod3, g, wd, *, n_ctx_tiles, ctx_row, tm):
    bsz, tt, d = a.shape
    f2 = 2 * D_FF
    prev, nxt = _halo_specs(tm, SUBLANES, tt, 0, f2)
    one = pl.BlockSpec((None, tm, d), lambda b, i: (b, i, 0))
    return pl.pallas_call(
        functools.partial(_ffn_tail_kernel, n_ctx_tiles=n_ctx_tiles, n_tiles=tt // tm, tm=tm),
        grid=(bsz, tt // tm),
        in_specs=[one, pl.BlockSpec((None, tm, f2), lambda b, i: (b, i, 0)), prev, nxt,
                  pl.BlockSpec((3, f2), lambda b, i: (0, 0)),
                  pl.BlockSpec((None, N_MOD, d), _mod_row(n_ctx_tiles, ctx_row)),
                  pl.BlockSpec((1, d), lambda b, i: (0, 0)),
                  pl.BlockSpec((D_FF, d), lambda b, i: (0, 0))],
        out_specs=one,
        out_shape=jax.ShapeDtypeStruct((bsz, tt, d), F32),
        scratch_shapes=[pltpu.VMEM((tm + 2 * SUBLANES, f2), F32)],
        compiler_params=_cparams(("parallel", "parallel")),
        name="ffn_tail",
    )(a, z, z, z, conv_w, mod3, g.reshape(1, d), wd)


def _rope_tables(ctx_len, seq):
    rows = seq // GRID_W
    row = jnp.repeat(jnp.arange(rows, dtype=F32), GRID_W)
    col = jnp.tile(jnp.arange(GRID_W, dtype=F32), rows)
    inv_freq = ROPE_THETA ** (-jnp.arange(N_FREQ, dtype=F32) / N_FREQ)
    ang_r = row[:, None] * inv_freq
    ang_c = col[:, None] * inv_freq
    cos = jnp.concatenate([jnp.cos(ang_r), jnp.cos(ang_r), jnp.cos(ang_c), jnp.cos(ang_c)], axis=-1)
    sin = jnp.concatenate([-jnp.sin(ang_r), jnp.sin(ang_r), -jnp.sin(ang_c), jnp.sin(ang_c)], axis=-1)
    cos = jnp.concatenate([jnp.ones((ctx_len, HEAD_DIM), F32), cos], axis=0)
    sin = jnp.concatenate([jnp.zeros((ctx_len, HEAD_DIM), F32), sin], axis=0)
    return cos, sin


def _reorder_w_in(w):
    qkv = w[:, 0:1536]
    glu = w[:, 1536:3584]
    rkv = w[:, 3584:6656]
    lora = w[:, 6656:7040]
    gates = w[:, 7040:10112]
    pad = jnp.zeros((w.shape[0], LANES), w.dtype)
    return jnp.concatenate([qkv, lora, pad, glu, rkv, gates], axis=1).astype(BF16)


def _pad_lora_up(up):
    z = jnp.zeros_like(up[0])
    w = jnp.stack([jnp.concatenate([up[0], z], axis=0), jnp.concatenate([z, up[1]], axis=0)])
    hi = w.astype(BF16)
    return jnp.stack([hi, (w - hi.astype(F32)).astype(BF16)], axis=1)


def kernel(x, c, ctx, c_ctx, w_mod, b_mod, g_pre_mix, g_post_mix, g_pre_ffn, g_post_ffn, w_in, q_norm, k_norm,
           w_attn_o, conv_w, conv_b, conv_ln_g, conv_ln_b, w_conv_o, shift_w, decay_w0, decay_up, iclr_a0,
           iclr_up, gate_up, k_k, k_a, r_k, wkv_gn_g, wkv_gn_b, w_rwkv_o, w_out, w_ffn_up, ffn_conv_w, w_ffn_down):
    bsz, seq, d = x.shape
    ctx_len = ctx.shape[1]
    depth = w_mod.shape[0]
    tm = min(ROW_TILE, ctx_len)
    assert d == D_MODEL and ctx_len % tm == 0 and seq % tm == 0 and ctx_len % CHUNK == 0 and seq % CHUNK == 0
    n_ctx_tiles = ctx_len // tm

    a = jnp.concatenate([ctx, x], axis=1)
    mod_rows = -(-(bsz + 1) // SUBLANES) * SUBLANES
    cc = jnp.zeros((mod_rows, d), F32).at[:bsz].set(c).at[bsz].set(c_ctx)
    cos, sin = _rope_tables(ctx_len, seq)
    head = jnp.arange(d, dtype=jnp.int32) // RWKV_HEAD
    e = (head[:, None] == jnp.arange(LANES, dtype=jnp.int32)[None, :]).astype(BF16)
    et = e.T
    def seg_rows(segs, n_ctx):
        return jnp.array([bsz if s % segs < n_ctx else s // segs for s in range(bsz * segs)], jnp.int32)

    for l in range(depth):
        last = l == depth - 1
        mod3 = _mod_call(cc, w_mod[l], b_mod[l]).reshape(mod_rows, N_MOD, d)
        modseg = jnp.take(mod3, seg_rows((ctx_len + seq) // tm, n_ctx_tiles), axis=0)
        proj = _nm_matmul(a, modseg, g_pre_mix[l], _reorder_w_in(w_in[l]), shift_idx=0, tn=PROJ_TN, seg=tm,
                          max_seg=4)
        q, k, v = _qk_prep(proj, cos, sin, q_norm[l], k_norm[l], tm=tm)
        att = _attention(q, k, v, ctx_len=ctx_len, tq=tm, skip_ctx=last)
        conv = _conformer(proj, conv_w[l], conv_b[l], conv_ln_g[l], conv_ln_b[l], ctx_len=ctx_len, tm=tm,
                          skip_ctx=last)
        r, vv, kap, lw, kd, bd = _rwkv_prep(proj, shift_w[l], k_k[l], k_a[l], decay_w0[l], _pad_lora_up(decay_up[l]),
                                            iclr_a0[l], _pad_lora_up(iclr_up[l]), e, et, ctx_len=ctx_len, tm=tm)
        y = _rwkv_scan(r, vv, kap, lw, kd, bd, ctx_len=ctx_len)
        a = _merge(a, att, conv, y, r, vv, kd, proj, mod3, g_post_mix[l], wkv_gn_g[l], wkv_gn_b[l], r_k[l],
                   gate_up[l].astype(BF16), e, et, w_attn_o[l].astype(BF16), w_conv_o[l].astype(BF16),
                   w_rwkv_o[l].astype(BF16), w_out[l].astype(BF16), n_ctx_tiles=n_ctx_tiles, ctx_row=bsz, tm=tm,
                   skip_ctx=last)
        n_ctx = 0 if last else n_ctx_tiles
        if last:
            modseg = jnp.take(mod3, seg_rows(seq // tm, 0), axis=0)
        z = _nm_matmul(a, modseg, g_pre_ffn[l], w_ffn_up[l].astype(BF16), shift_idx=3, tn=D_FF // 2, seg=tm,
                       max_seg=4)
        a = _ffn_tail(a, z, ffn_conv_w[l], mod3, g_post_ffn[l], w_ffn_down[l].astype(BF16),
                      n_ctx_tiles=n_ctx, ctx_row=bsz, tm=tm)
    return a
```

```python
import functools
import math

import jax
import jax.numpy as jnp
from jax import lax
from jax.experimental import pallas as pl
from jax.experimental.pallas import tpu as pltpu

F32 = jnp.float32
BF16 = jnp.bfloat16

D_MODEL = 1024
GRID_W = 64
N_Q_HEADS = 8
N_KV_HEADS = 2
GQA_GROUP = N_Q_HEADS // N_KV_HEADS
HEAD_DIM = 128
N_FREQ = HEAD_DIM // 4
ROPE_THETA = 10000.0
CONV_K = 31
CONV_HALO = 16
RWKV_HEAD = 64
RWKV_HEADS = D_MODEL // RWKV_HEAD
DECAY_SCALE = math.exp(-0.5)
D_FF = 2816
N_MOD = 6
EPS = 1e-6
LN_EPS = 1e-5
GN_EPS = RWKV_HEAD * 1e-5
SOFTMAX_C = (HEAD_DIM ** -0.5) * math.log2(math.e)
LANES = 128
SUBLANES = 8
CHUNK = 64
N_PAIR = D_MODEL // LANES
SCAN_BATCH = 2
INV_BASE = 4
ROW_TILE = 256
CONV_ROWS = 32
MOD_TN = 1536
PROJ_TN = 2048
LORA_BLOCK = 512
ATTN_SPAN = 256
VMEM_LIMIT = 48 * 1024 * 1024

COL_Q = 0
COL_KV = 1024
COL_LORA = 1536
COL_GLU = 2048
COL_RKV = 4096
COL_GATE = 7168
N_IN_PAD = 10240


def _cparams(sem):
    return pltpu.CompilerParams(dimension_semantics=sem, vmem_limit_bytes=VMEM_LIMIT)


def _bdot(a, b):
    return jnp.dot(a.astype(BF16), b.astype(BF16), preferred_element_type=F32)


def _bdot_nt(a, b):
    return lax.dot_general(a.astype(BF16), b.astype(BF16), (((1,), (1,)), ((), ())),
                           preferred_element_type=F32)


def _bdot_tn(a, b):
    return lax.dot_general(a.astype(BF16), b.astype(BF16), (((0,), (0,)), ((), ())),
                           preferred_element_type=F32)


def _split2(x):
    hi = x.astype(BF16)
    lo = (x - hi.astype(F32)).astype(BF16)
    return hi, lo


def _dot3_presplit(a, bh, bl):
    ah, al = _split2(a)
    dot = functools.partial(jnp.dot, preferred_element_type=F32)
    return dot(ah, bh) + dot(al, bh) + dot(ah, bl)


def _dot3(a, b):
    return _dot3_presplit(a, *_split2(b))


def _sigmoid(x):
    return jax.nn.sigmoid(x)


def _head_sum(x, e_ref, et_ref):
    xh, xl = _split2(x)
    dot = functools.partial(jnp.dot, preferred_element_type=F32)
    s = dot(xh, e_ref[...]) + dot(xl, e_ref[...])
    sh, sl = _split2(s)
    return dot(sh, et_ref[...]) + dot(sl, et_ref[...])


def _mod_kernel(c_ref, w_ref, b_ref, o_ref):
    c = c_ref[...]
    o_ref[...] = _dot3(c * _sigmoid(c), w_ref[...]) + b_ref[...]


def _mod_call(cc, w_mod, b_mod):
    rows = cc.shape[0]
    n = w_mod.shape[1]
    tn = MOD_TN
    return pl.pallas_call(
        _mod_kernel,
        grid=(n // tn,),
        in_specs=[pl.BlockSpec((rows, D_MODEL), lambda j: (0, 0)),
                  pl.BlockSpec((D_MODEL, tn), lambda j: (0, j)),
                  pl.BlockSpec((1, tn), lambda j: (0, j))],
        out_specs=pl.BlockSpec((rows, tn), lambda j: (0, j)),
        out_shape=jax.ShapeDtypeStruct((rows, n), F32),
        compiler_params=_cparams(("parallel",)),
        name="mod",
    )(cc, w_mod, b_mod.reshape(1, n))


def _mod_row(n_ctx_tiles, ctx_row):
    return lambda b, i, *_: (jnp.where(i < n_ctx_tiles, ctx_row, b), 0, 0)


def _nm_matmul_kernel(a_ref, mod_ref, g_ref, w_ref, o_ref, h_ref, *, shift_idx, seg, nseg):
    @pl.when(pl.program_id(1) == 0)
    def _():
        for s in range(nseg):
            rows = slice(s * seg, (s + 1) * seg)
            x = a_ref[rows, :]
            y = x * lax.rsqrt(jnp.mean(x * x, axis=-1, keepdims=True) + EPS) * g_ref[...]
            h = y * (1.0 + mod_ref[s, shift_idx + 1:shift_idx + 2, :]) + mod_ref[s, shift_idx:shift_idx + 1, :]
            h_ref[rows, :] = h.astype(BF16)

    o_ref[...] = jnp.dot(h_ref[...], w_ref[...], preferred_element_type=F32).astype(o_ref.dtype)


def _nm_matmul(a, modseg, g, w, *, shift_idx, tn, seg, max_seg):
    bsz, tt, d = a.shape
    n = w.shape[1]
    rows = bsz * tt
    nseg = max(s for s in (4, 2, 1) if s <= max_seg and (rows // seg) % s == 0)
    tm = seg * nseg
    out = pl.pallas_call(
        functools.partial(_nm_matmul_kernel, shift_idx=shift_idx, seg=seg, nseg=nseg),
        grid=(rows // tm, n // tn),
        in_specs=[pl.BlockSpec((tm, d), lambda i, j: (i, 0)),
                  pl.BlockSpec((nseg, N_MOD, d), lambda i, j: (i, 0, 0)),
                  pl.BlockSpec((1, d), lambda i, j: (0, 0)),
                  pl.BlockSpec((d, tn), lambda i, j: (0, j))],
        out_specs=pl.BlockSpec((tm, tn), lambda i, j: (i, j)),
        out_shape=jax.ShapeDtypeStruct((rows, n), F32),
        scratch_shapes=[pltpu.VMEM((tm, d), BF16)],
        compiler_params=_cparams(("parallel", "arbitrary")),
        name="norm_mod_matmul",
    )(a.reshape(rows, d), modseg, g.reshape(1, d), w)
    return out.reshape(bsz, tt, n)


def _qk_prep_kernel(q_ref, kv_ref, cos_ref, sin_ref, qn_ref, kn_ref, qo_ref, ko_ref, vo_ref):
    cos = cos_ref[...]
    sin = sin_ref[...]
    lane = lax.broadcasted_iota(jnp.int32, cos.shape, 1)
    first = (lane & (N_FREQ)) == 0

    heads = ([(q_ref, h, qn_ref, qo_ref, SOFTMAX_C) for h in range(N_Q_HEADS)]
             + [(kv_ref, h, kn_ref, ko_ref, 1.0) for h in range(N_KV_HEADS)])
    sls = [slice(h * HEAD_DIM, (h + 1) * HEAD_DIM) for _, h, _, _, _ in heads]
    xs = [src[:, sl] for (src, _, _, _, _), sl in zip(heads, sls)]
    ms = [jnp.mean(x * x, axis=-1, keepdims=True) for x in xs]
    ys = [x * lax.rsqrt(m + EPS) * (g[...] * c) for x, m, (_, _, g, _, c) in zip(xs, ms, heads)]
    up = [pltpu.roll(y, LANES - N_FREQ, 1) for y in ys]
    dn = [pltpu.roll(y, N_FREQ, 1) for y in ys]
    for y, u, d, (_, _, _, dst, _), sl in zip(ys, up, dn, heads, sls):
        dst[:, sl] = (y * cos + jnp.where(first, u, d) * sin).astype(BF16)
    vo_ref[...] = kv_ref[:, N_KV_HEADS * HEAD_DIM:].astype(BF16)


def _qk_prep(proj, cos, sin, q_norm, k_norm, *, tm):
    bsz, tt, _ = proj.shape
    dq = N_Q_HEADS * HEAD_DIM
    dkv = N_KV_HEADS * HEAD_DIM
    return pl.pallas_call(
        _qk_prep_kernel,
        grid=(bsz, tt // tm),
        in_specs=[pl.BlockSpec((None, tm, dq), lambda b, i: (b, i, COL_Q // dq)),
                  pl.BlockSpec((None, tm, 2 * dkv), lambda b, i: (b, i, COL_KV // (2 * dkv))),
                  pl.BlockSpec((tm, HEAD_DIM), lambda b, i: (i, 0)),
                  pl.BlockSpec((tm, HEAD_DIM), lambda b, i: (i, 0)),
                  pl.BlockSpec((1, HEAD_DIM), lambda b, i: (0, 0)),
                  pl.BlockSpec((1, HEAD_DIM), lambda b, i: (0, 0))],
        out_specs=[pl.BlockSpec((None, tm, dq), lambda b, i: (b, i, 0)),
                   pl.BlockSpec((None, tm, dkv), lambda b, i: (b, i, 0)),
                   pl.BlockSpec((None, tm, dkv), lambda b, i: (b, i, 0))],
        out_shape=[jax.ShapeDtypeStruct((bsz, tt, dq), BF16),
                   jax.ShapeDtypeStruct((bsz, tt, dkv), BF16),
                   jax.ShapeDtypeStruct((bsz, tt, dkv), BF16)],
        compiler_params=_cparams(("parallel", "parallel")),
        name="qk_prep",
    )(proj, proj, cos, sin, q_norm.reshape(1, HEAD_DIM), k_norm.reshape(1, HEAD_DIM))


def _attn_kernel(q_ref, k_ref, v_ref, o_ref, *, n_ctx_tiles, ctx_len, skip_ctx):
    tq = q_ref.shape[0]
    tt = k_ref.shape[0]

    def attend(spans):
        q = jnp.concatenate([q_ref[:, g * HEAD_DIM:(g + 1) * HEAD_DIM] for g in range(GQA_GROUP)], axis=0)
        m = l = acc = None
        for start, size in spans:
            k = k_ref[start:start + size, :]
            v = v_ref[start:start + size, :]
            s = lax.dot_general(q, k, (((1,), (1,)), ((), ())), preferred_element_type=F32)
            mt = jnp.max(s, axis=-1, keepdims=True)
            if m is None:
                m = mt
                p = jnp.exp2(s - m)
                l = jnp.sum(p, axis=-1, keepdims=True)
                acc = jnp.dot(p.astype(BF16), v, preferred_element_type=F32)
            else:
                m_new = jnp.maximum(m, mt)
                alpha = jnp.exp2(m - m_new)
                p = jnp.exp2(s - m_new)
                l = alpha * l + jnp.sum(p, axis=-1, keepdims=True)
                acc = alpha * acc + jnp.dot(p.astype(BF16), v, preferred_element_type=F32)
                m = m_new
        o = acc / l
        for g in range(GQA_GROUP):
            o_ref[:, g * HEAD_DIM:(g + 1) * HEAD_DIM] = o[g * tq:(g + 1) * tq].astype(o_ref.dtype)

    x_spans = [(start, min(ATTN_SPAN, tt - start)) for start in range(0, tt, ATTN_SPAN)]
    if skip_ctx:
        attend(x_spans)
        return
    i = pl.program_id(2)

    @pl.when(i < n_ctx_tiles)
    def _():
        attend([(0, ctx_len)])

    @pl.when(i >= n_ctx_tiles)
    def _():
        attend(x_spans)


def _attention(q, k, v, *, ctx_len, tq, skip_ctx):
    bsz, tt, dq = q.shape
    gw = GQA_GROUP * HEAD_DIM
    off = ctx_len // tq if skip_ctx else 0
    return pl.pallas_call(
        functools.partial(_attn_kernel, n_ctx_tiles=ctx_len // tq, ctx_len=ctx_len, skip_ctx=skip_ctx),
        grid=(bsz, N_KV_HEADS, tt // tq - off),
        in_specs=[pl.BlockSpec((None, tq, gw), lambda b, h, i: (b, i + off, h)),
                  pl.BlockSpec((None, tt, HEAD_DIM), lambda b, h, i: (b, 0, h)),
                  pl.BlockSpec((None, tt, HEAD_DIM), lambda b, h, i: (b, 0, h))],
        out_specs=pl.BlockSpec((None, tq, gw), lambda b, h, i: (b, i, h)),
        out_shape=jax.ShapeDtypeStruct((bsz, tt - off * tq, dq), BF16),
        compiler_params=_cparams(("parallel", "parallel", "arbitrary")),
        name="attention",
    )(q, k, v)


def _halo_specs(tm, halo, tt, col_block, width, off=0):
    per = tm // halo
    last = tt // halo - 1
    prev = pl.BlockSpec((None, halo, width), lambda b, i: (b, jnp.maximum((i + off) * per - 1, 0), col_block))
    nxt = pl.BlockSpec((None, halo, width), lambda b, i: (b, jnp.minimum((i + off + 1) * per, last), col_block))
    return prev, nxt


def _edge_flags(i, n_ctx_tiles, n_tiles):
    first = jnp.logical_or(i == 0, i == n_ctx_tiles)
    last = jnp.logical_or(i == n_ctx_tiles - 1, i == n_tiles - 1)
    return first, last


def _conformer_kernel(cur_ref, prev_ref, next_ref, w_ref, cb_ref, lg_ref, lb_ref,
                      o_ref, buf_ref, sh_ref, *, n_ctx_tiles, n_tiles, tm, rc, tile0):
    first, last = _edge_flags(pl.program_id(1) + tile0, n_ctx_tiles, n_tiles)
    h = CONV_HALO
    d = D_MODEL

    def glu(ref):
        return ref[:, 0:d] * _sigmoid(ref[:, d:2 * d])

    buf_ref[0:h, :] = jnp.where(first, 0.0, glu(prev_ref))
    buf_ref[h:h + tm, :] = glu(cur_ref)
    buf_ref[h + tm:h + tm + h, :] = jnp.where(last, 0.0, glu(next_ref))
    n_sh = sh_ref.shape[1]
    for s in range(1, SUBLANES):
        sh_ref[s - 1] = buf_ref[s:s + n_sh, :]
    lead = h - CONV_K // 2
    for c in range(tm // rc):
        acc = jnp.zeros((rc, D_MODEL), F32)
        for j in range(CONV_K):
            s = (j + lead) % SUBLANES
            r0 = c * rc + (j + lead) - s
            tap = buf_ref[r0:r0 + rc, :] if s == 0 else sh_ref[s - 1, r0:r0 + rc, :]
            acc = acc + tap * jnp.concatenate([w_ref[j]] * (rc // SUBLANES), axis=0)
        y = acc + cb_ref[...]
        mu = jnp.mean(y, axis=-1, keepdims=True)
        dlt = y - mu
        var = jnp.mean(dlt * dlt, axis=-1, keepdims=True)
        z = dlt * lax.rsqrt(var + LN_EPS) * lg_ref[...] + lb_ref[...]
        o_ref[c * rc:(c + 1) * rc, :] = (z * _sigmoid(z)).astype(o_ref.dtype)


def _conformer(proj, conv_w, conv_b, ln_g, ln_b, *, ctx_len, tm, skip_ctx):
    bsz, tt, _ = proj.shape
    d = D_MODEL
    cg = COL_GLU // (2 * d)
    off = ctx_len // tm if skip_ctx else 0
    prev, nxt = _halo_specs(tm, CONV_HALO, tt, cg, 2 * d, off)
    vec = lambda: pl.BlockSpec((1, d), lambda b, i: (0, 0))
    return pl.pallas_call(
        functools.partial(_conformer_kernel, n_ctx_tiles=ctx_len // tm, n_tiles=tt // tm, tm=tm, rc=CONV_ROWS,
                          tile0=off),
        grid=(bsz, tt // tm - off),
        in_specs=[pl.BlockSpec((None, tm, 2 * d), lambda b, i: (b, i + off, cg)), prev, nxt,
                  pl.BlockSpec((CONV_K, SUBLANES, d), lambda b, i: (0, 0, 0)),
                  vec(), vec(), vec()],
        out_specs=pl.BlockSpec((None, tm, d), lambda b, i: (b, i, 0)),
        out_shape=jax.ShapeDtypeStruct((bsz, tt - off * tm, d), BF16),
        scratch_shapes=[pltpu.VMEM((tm + 2 * CONV_HALO, d), F32),
                        pltpu.VMEM((SUBLANES - 1, tm + 2 * CONV_HALO - SUBLANES, d), F32)],
        compiler_params=_cparams(("parallel", "parallel")),
        name="conformer",
    )(proj, proj, proj, jnp.broadcast_to(conv_w[:, None, :], (CONV_K, SUBLANES, d)),
      conv_b.reshape(1, d), ln_g.reshape(1, d), ln_b.reshape(1, d))


def _rwkv_prep_kernel(rc_ref, kc_ref, vc_ref, rp_ref, kp_ref, vp_ref, rn_ref, kn_ref, vn_ref, lora_ref,
                      sw_ref, kk_ref, ka_ref, w0_ref, dup_ref, a0_ref, iup_ref, e_ref, et_ref,
                      r_ref, v_ref, kap_ref, lw_ref, kd_ref, bd_ref,
                      *, n_ctx_tiles, n_tiles, tm):
    first, last = _edge_flags(pl.program_id(1), n_ctx_tiles, n_tiles)
    h = SUBLANES
    d = D_MODEL

    row = lax.broadcasted_iota(jnp.int32, (tm, 1), 0)

    def shift(cur_ref, prev_ref, next_ref, col):
        cur = cur_ref[...]
        before = jnp.where(first, 0.0, prev_ref[h - 1:h, :])
        after = jnp.where(last, 0.0, next_ref[0:1, :])
        prv = jnp.where(row == 0, before, pltpu.roll(cur, 1, 0))
        nxt = jnp.where(row == tm - 1, after, pltpu.roll(cur, tm - 1, 0))
        sl = slice(col * d, (col + 1) * d)
        return prv * sw_ref[0:1, sl] + cur * sw_ref[1:2, sl] + nxt * sw_ref[2:3, sl]

    r = shift(rc_ref, rp_ref, rn_ref, 0)
    k = shift(kc_ref, kp_ref, kn_ref, 1)
    v = shift(vc_ref, vp_ref, vn_ref, 2)
    r_ref[...] = r
    v_ref[...] = v
    kk = k * kk_ref[...]
    ss = _head_sum(kk * kk, e_ref, et_ref)
    kap = kk * lax.rsqrt(jnp.maximum(ss, 1e-12))
    kap_ref[...] = kap
    tw = jnp.tanh(lora_ref[:, 0:LANES])
    la = lora_ref[:, LANES:2 * LANES]
    for dr in range(2):
        z = w0_ref[dr:dr + 1, :] + _dot3_presplit(tw, dup_ref[dr, 0], dup_ref[dr, 1])
        lw_ref[dr] = -DECAY_SCALE * _sigmoid(z)
        a = _sigmoid(a0_ref[dr:dr + 1, :] + _dot3_presplit(la, iup_ref[dr, 0], iup_ref[dr, 1]))
        kd_ref[dr] = k * (1.0 + (a - 1.0) * ka_ref[...])
        bd_ref[dr] = a * kap


def _rwkv_prep(proj, shift_w, k_k, k_a, decay_w0, decay_up_pad, iclr_a0, iclr_up_pad, e, et, *, ctx_len, tm):
    bsz, tt, _ = proj.shape
    d = D_MODEL
    c0 = COL_RKV // d
    cur = lambda c: pl.BlockSpec((None, tm, d), lambda b, i: (b, i, c))
    halos = [_halo_specs(tm, SUBLANES, tt, c0 + c, d) for c in range(3)]
    full = lambda shape: pl.BlockSpec(shape, lambda b, i: (0,) * len(shape))
    out1 = pl.BlockSpec((None, tm, d), lambda b, i: (b, i, 0))
    out2 = pl.BlockSpec((2, None, tm, d), lambda b, i: (0, b, i, 0))
    s1 = jax.ShapeDtypeStruct((bsz, tt, d), F32)
    s2 = jax.ShapeDtypeStruct((2, bsz, tt, d), F32)
    return pl.pallas_call(
        functools.partial(_rwkv_prep_kernel, n_ctx_tiles=ctx_len // tm, n_tiles=tt // tm, tm=tm),
        grid=(bsz, tt // tm),
        in_specs=[cur(c0), cur(c0 + 1), cur(c0 + 2),
                  halos[0][0], halos[1][0], halos[2][0], halos[0][1], halos[1][1], halos[2][1],
                  pl.BlockSpec((None, tm, LORA_BLOCK), lambda b, i: (b, i, COL_LORA // LORA_BLOCK)),
                  full((3, 3 * d)), full((1, d)), full((1, d)), full((2, d)), full((2, 2, LANES, d)),
                  full((2, d)), full((2, 2, LANES, d)), full((d, LANES)), full((LANES, d))],
        out_specs=[out1, out1, out1, out2, out2, out2],
        out_shape=[s1, s1, s1, s2, s2, s2],
        compiler_params=_cparams(("parallel", "parallel")),
        name="rwkv_prep",
    )(proj, proj, proj, proj, proj, proj, proj, proj, proj, proj,
      shift_w, k_k.reshape(1, d), k_a.reshape(1, d), decay_w0, decay_up_pad, iclr_a0, iclr_up_pad, e, et)


def _scan_kernel(r_ref, v_ref, kap_ref, lw_ref, k_ref, b_ref, y_ref, h_ref):
    c = CHUNK
    c2 = 2 * c
    sgn = 1 - 2 * pl.program_id(1)

    @pl.when(pl.program_id(2) == 0)
    def _():
        h_ref[...] = jnp.zeros_like(h_ref)

    row = lax.broadcasted_iota(jnp.int32, (c, c), 0)
    col = lax.broadcasted_iota(jnp.int32, (c, c), 1)
    incl = jnp.where((col - row) * sgn <= 0, 1.0, 0.0).astype(BF16)
    dot = functools.partial(jnp.dot, preferred_element_type=F32)
    nb = lw_ref.shape[0]
    cum = []
    for bi in range(nb):
        lw = lw_ref[bi]
        hi = lw.astype(BF16)
        rem = lw - hi.astype(F32)
        mid = rem.astype(BF16)
        lo = (rem - mid.astype(F32)).astype(BF16)
        cum.append(dot(incl, hi) + dot(incl, mid) + dot(incl, lo))

    prow = lax.broadcasted_iota(jnp.int32, (c, c2), 0)
    pcol = lax.broadcasted_iota(jnp.int32, (c, c2), 1)
    order = ((pcol & (c - 1)) - prow) * sgn
    strict = order < 0
    upto = order <= 0
    eye = order == 0
    head0 = pcol < RWKV_HEAD

    def stack(x):
        return jnp.concatenate([jnp.where(head0, x, 0.0), jnp.where(head0, 0.0, x)], axis=0)

    def pack(x):
        return jnp.where(head0, x[0:c], x[c:c2])

    units = [(bi, slice(p * LANES, (p + 1) * LANES)) for bi in range(nb) for p in range(N_PAIR)]
    pairs = range(len(units))
    kt, rt, vs, kend, bend, ptot, a = [], [], [], [], [], [], []
    for bi, sl in units:
        cum_p = cum[bi][:, sl]
        lw_p = lw_ref[bi, :, sl]
        tot_p = jnp.sum(lw_p, axis=0, keepdims=True)
        p_inv = jnp.exp(-cum_p)
        p_end = jnp.exp(tot_p - cum_p)
        k = k_ref[bi, :, sl]
        b = b_ref[bi, :, sl]
        kt.append(kap_ref[bi, :, sl] * jnp.exp(cum_p - lw_p))
        rt.append(r_ref[bi, :, sl] * jnp.exp(cum_p))
        vs.append(v_ref[bi, :, sl])
        kend.append(k * p_end)
        bend.append(b * p_end)
        ptot.append(jnp.exp(tot_p))
        a.append(_bdot_nt(jnp.concatenate([kt[-1], rt[-1]], axis=0),
                          jnp.concatenate([stack(b * p_inv), stack(k * p_inv)], axis=0)))
    a_ab = [jnp.where(strict, a[p][0:c, 0:c2], 0.0) for p in pairs]
    a_ak = [jnp.where(strict, a[p][0:c, c2:2 * c2], 0.0) for p in pairs]
    a_rb = [jnp.where(upto, a[p][c:c2, 0:c2], 0.0) for p in pairs]
    a_rk = [jnp.where(upto, a[p][c:c2, c2:2 * c2], 0.0) for p in pairs]

    blk = (pcol & (c - 1)) ^ prow
    ident = jnp.where(eye, 1.0, 0.0)
    l0 = [jnp.where(blk < INV_BASE, a_ab[p], 0.0) for p in pairs]
    sq = [_bdot(l0[p], stack(l0[p])) for p in pairs]
    t = [ident - l0[p] for p in pairs]
    t = [t[p] + _bdot(t[p], stack(sq[p])) for p in pairs]
    s_blk = INV_BASE
    while s_blk < c:
        off = jnp.logical_and(blk >= s_blk, blk < 2 * s_blk)
        tl = [_bdot(t[p], stack(jnp.where(off, a_ab[p], 0.0))) for p in pairs]
        t = [t[p] - _bdot(tl[p], stack(t[p])) for p in pairs]
        s_blk *= 2

    asv = [_bdot(jnp.concatenate([a_ak[p], a_rk[p]], axis=0), stack(vs[p])) for p in pairs]
    wu = [_bdot(t[p], jnp.concatenate([stack(kt[p]), stack(asv[p][0:c])], axis=1)) for p in pairs]
    w = [wu[p][:, 0:c2] for p in pairs]
    u0 = [wu[p][:, c2:2 * c2] for p in pairs]
    arb = [_bdot(a_rb[p], jnp.concatenate([stack(w[p]), stack(u0[p])], axis=1)) for p in pairs]
    y0 = [asv[p][c:c2] - arb[p][:, c2:2 * c2] for p in pairs]
    y1 = [rt[p] - arb[p][:, 0:c2] for p in pairs]
    m = [jnp.where(eye, ptot[p], 0.0) - pack(_bdot_tn(bend[p], w[p])) for p in pairs]
    nn = [pack(_bdot_tn(jnp.concatenate([kend[p], -bend[p]], axis=0), jnp.concatenate([vs[p], u0[p]], axis=0)))
          for p in pairs]
    for p in pairs:
        h0 = h_ref[p]
        hh = h0.astype(BF16)
        hl = h0 - hh.astype(F32)
        mh, ml = _split2(m[p])
        top = _bdot(jnp.concatenate([mh, ml, y1[p].astype(BF16)], axis=0), stack(hh.astype(F32)))
        y_ref[units[p][0], :, units[p][1]] = top[c2:c2 + c] + y0[p]
        h_ref[p] = top[0:c] + top[c:c2] + _bdot(mh, stack(hl)) + nn[p]


def _rwkv_scan(r, v, kap, lw, kd, bd, *, ctx_len):
    bsz, tt, d = r.shape
    nc = tt // CHUNK
    ncc = ctx_len // CHUNK

    def chunk(dr, s):
        return jnp.where(dr == 0, s, jnp.where(s < ncc, ncc - 1 - s, nc + ncc - 1 - s))

    ub = SCAN_BATCH if bsz % SCAN_BATCH == 0 else 1
    shared = pl.BlockSpec((ub, CHUNK, d), lambda b, dr, s: (b, chunk(dr, s), 0))
    per_dir = pl.BlockSpec((None, ub, CHUNK, d), lambda b, dr, s: (dr, b, chunk(dr, s), 0))
    return pl.pallas_call(
        _scan_kernel,
        grid=(bsz // ub, 2, nc),
        in_specs=[shared, shared, shared, per_dir, per_dir, per_dir],
        out_specs=per_dir,
        out_shape=jax.ShapeDtypeStruct((2, bsz, tt, d), F32),
        scratch_shapes=[pltpu.VMEM((ub * N_PAIR, CHUNK, LANES), F32)],
        compiler_params=_cparams(("parallel", "parallel", "arbitrary")),
        name="rwkv_scan",
    )(r, v, kap, lw, kd, bd)


def _merge_kernel(a_ref, att_ref, conv_ref, y_ref, r_ref, v_ref, kd_ref, lora_ref, ga_ref, gc_ref, gr_ref,
                  mod_ref, g_ref, gg_ref, gb_ref, rk_ref, gup_ref, e_ref, et_ref,
                  wa_ref, wc_ref, wr_ref, wo_ref, o_ref):
    dot = functools.partial(jnp.dot, preferred_element_type=F32)
    inv = 1.0 / RWKV_HEAD
    y = y_ref[0] + y_ref[1]
    mu = _head_sum(y, e_ref, et_ref) * inv
    dlt = y - mu
    var = _head_sum(dlt * dlt, e_ref, et_ref) * inv
    yn = dlt * lax.rsqrt(var + GN_EPS) * gg_ref[...] + gb_ref[...]
    bonus = _head_sum(r_ref[...] * (kd_ref[0] + kd_ref[1]) * rk_ref[...], e_ref, et_ref) * v_ref[...]
    gate = _bdot(_sigmoid(lora_ref[:, 2 * LANES:3 * LANES]), gup_ref[...])
    rw = ((yn + bonus) * gate).astype(BF16)
    m = (_sigmoid(ga_ref[...]) * dot(att_ref[...], wa_ref[...])
         + _sigmoid(gc_ref[...]) * dot(conv_ref[...], wc_ref[...])
         + _sigmoid(gr_ref[...]) * dot(rw, wr_ref[...]))
    z = dot(m.astype(BF16), wo_ref[...])
    zn = z * lax.rsqrt(jnp.mean(z * z, axis=-1, keepdims=True) + EPS) * g_ref[...]
    o_ref[...] = a_ref[...] + mod_ref[2:3, :] * zn


def _merge(a, att, conv, y, r, v, kd, proj, mod3, g, gn_g, gn_b, r_k, gate_up, e, et, wa, wc, wr, wo,
           *, n_ctx_tiles, ctx_row, tm, skip_ctx):
    bsz, tt, d = a.shape
    cg = COL_GATE // d
    off = n_ctx_tiles if skip_ctx else 0
    loc = pl.BlockSpec((None, tm, d), lambda b, i: (b, i, 0))
    one = pl.BlockSpec((None, tm, d), lambda b, i: (b, i + off, 0))
    two = pl.BlockSpec((2, None, tm, d), lambda b, i: (0, b, i + off, 0))
    gate = lambda c: pl.BlockSpec((None, tm, d), lambda b, i: (b, i + off, cg + c))
    full = lambda shape: pl.BlockSpec(shape, lambda b, i: (0,) * len(shape), pipeline_mode=pl.Buffered(1))
    return pl.pallas_call(
        _merge_kernel,
        grid=(bsz, tt // tm - off),
        in_specs=[one, loc, loc, two, one, one, two,
                  pl.BlockSpec((None, tm, LORA_BLOCK), lambda b, i: (b, i + off, COL_LORA // LORA_BLOCK)),
                  gate(0), gate(1), gate(2),
                  pl.BlockSpec((None, N_MOD, d), _mod_row(n_ctx_tiles - off, ctx_row)),
                  full((1, d)), full((1, d)), full((1, d)), full((1, d)), full((LANES, d)),
                  full((d, LANES)), full((LANES, d)),
                  full((d, d)), full((d, d)), full((d, d)), full((d, d))],
        out_specs=loc,
        out_shape=jax.ShapeDtypeStruct((bsz, tt - off * tm, d), F32),
        compiler_params=_cparams(("parallel", "parallel")),
        name="merge",
    )(a, att, conv, y, r, v, kd, proj, proj, proj, proj, mod3, g.reshape(1, d), gn_g.reshape(1, d),
      gn_b.reshape(1, d), r_k.reshape(1, d), gate_up, e, et, wa, wc, wr, wo)


def _ffn_tail_kernel(a_ref, zc_ref, zp_ref, zn_ref, cw_ref, mod_ref, g_ref, wd_ref, o_ref,
                     *, n_ctx_tiles, n_tiles, tm):
    first, last = _edge_flags(pl.program_id(1), n_ctx_tiles, n_tiles)
    h = SUBLANES
    row = lax.broadcasted_iota(jnp.int32, (tm, 1), 0)

    def conv(sl):
        cur = zc_ref[:, sl]
        before = jnp.where(first, 0.0, zp_ref[h - 1:h, sl])
        after = jnp.where(last, 0.0, zn_ref[0:1, sl])
        prv = jnp.where(row == 0, before, pltpu.roll(cur, 1, 0))
        nxt = jnp.where(row == tm - 1, after, pltpu.roll(cur, tm - 1, 0))
        return prv * cw_ref[0:1, sl] + cur * cw_ref[1:2, sl] + nxt * cw_ref[2:3, sl]

    gate = conv(slice(0, D_FF))
    val = conv(slice(D_FF, 2 * D_FF))
    u = (gate * _sigmoid(gate) * val).astype(BF16)
    z = jnp.dot(u, wd_ref[...], preferred_element_type=F32)
    zn = z * lax.rsqrt(jnp.mean(z * z, axis=-1, keepdims=True) + EPS) * g_ref[...]
    o_ref[...] = a_ref[...] + mod_ref[5:6, :] * zn


def _ffn_tail(a, z, conv_w, mod3, g, wd, *, n_ctx_tiles, ctx_row, tm):
    bsz, tt, d = a.shape
    f2 = 2 * D_FF
    prev, nxt = _halo_specs(tm, SUBLANES, tt, 0, f2)
    one = pl.BlockSpec((None, tm, d), lambda b, i: (b, i, 0))
    return pl.pallas_call(
        functools.partial(_ffn_tail_kernel, n_ctx_tiles=n_ctx_tiles, n_tiles=tt // tm, tm=tm),
        grid=(bsz, tt // tm),
        in_specs=[one, pl.BlockSpec((None, tm, f2), lambda b, i: (b, i, 0)), prev, nxt,
                  pl.BlockSpec((3, f2), lambda b, i: (0, 0)),
                  pl.BlockSpec((None, N_MOD, d), _mod_row(n_ctx_tiles, ctx_row)),
                  pl.BlockSpec((1, d), lambda b, i: (0, 0)),
                  pl.BlockSpec((D_FF, d), lambda b, i: (0, 0))],
        out_specs=one,
        out_shape=jax.ShapeDtypeStruct((bsz, tt, d), F32),
        compiler_params=_cparams(("parallel", "parallel")),
        name="ffn_tail",
    )(a, z, z, z, conv_w, mod3, g.reshape(1, d), wd)


def _rope_tables(ctx_len, seq):
    rows = seq // GRID_W
    row = jnp.repeat(jnp.arange(rows, dtype=F32), GRID_W)
    col = jnp.tile(jnp.arange(GRID_W, dtype=F32), rows)
    inv_freq = ROPE_THETA ** (-jnp.arange(N_FREQ, dtype=F32) / N_FREQ)
    ang_r = row[:, None] * inv_freq
    ang_c = col[:, None] * inv_freq
    cos = jnp.concatenate([jnp.cos(ang_r), jnp.cos(ang_r), jnp.cos(ang_c), jnp.cos(ang_c)], axis=-1)
    sin = jnp.concatenate([-jnp.sin(ang_r), jnp.sin(ang_r), -jnp.sin(ang_c), jnp.sin(ang_c)], axis=-1)
    cos = jnp.concatenate([jnp.ones((ctx_len, HEAD_DIM), F32), cos], axis=0)
    sin = jnp.concatenate([jnp.zeros((ctx_len, HEAD_DIM), F32), sin], axis=0)
    return cos, sin


def _reorder_w_in(w):
    qkv = w[:, 0:1536]
    glu = w[:, 1536:3584]
    rkv = w[:, 3584:6656]
    lora = w[:, 6656:7040]
    gates = w[:, 7040:10112]
    pad = jnp.zeros((w.shape[0], LANES), w.dtype)
    return jnp.concatenate([qkv, lora, pad, glu, rkv, gates], axis=1).astype(BF16)


def _pad_lora_up(up):
    z = jnp.zeros_like(up[0])
    w = jnp.stack([jnp.concatenate([up[0], z], axis=0), jnp.concatenate([z, up[1]], axis=0)])
    hi = w.astype(BF16)
    return jnp.stack([hi, (w - hi.astype(F32)).astype(BF16)], axis=1)


def kernel(x, c, ctx, c_ctx, w_mod, b_mod, g_pre_mix, g_post_mix, g_pre_ffn, g_post_ffn, w_in, q_norm, k_norm,
           w_attn_o, conv_w, conv_b, conv_ln_g, conv_ln_b, w_conv_o, shift_w, decay_w0, decay_up, iclr_a0,
           iclr_up, gate_up, k_k, k_a, r_k, wkv_gn_g, wkv_gn_b, w_rwkv_o, w_out, w_ffn_up, ffn_conv_w, w_ffn_down):
    bsz, seq, d = x.shape
    ctx_len = ctx.shape[1]
    depth = w_mod.shape[0]
    tm = min(ROW_TILE, ctx_len)
    assert d == D_MODEL and ctx_len % tm == 0 and seq % tm == 0 and ctx_len % CHUNK == 0 and seq % CHUNK == 0
    n_ctx_tiles = ctx_len // tm

    a = jnp.concatenate([ctx, x], axis=1)
    mod_rows = -(-(bsz + 1) // SUBLANES) * SUBLANES
    cc = jnp.zeros((mod_rows, d), F32).at[:bsz].set(c).at[bsz].set(c_ctx)
    cos, sin = _rope_tables(ctx_len, seq)
    head = jnp.arange(d, dtype=jnp.int32) // RWKV_HEAD
    e = (head[:, None] == jnp.arange(LANES, dtype=jnp.int32)[None, :]).astype(BF16)
    et = e.T
    def seg_rows(segs, n_ctx):
        return jnp.array([bsz if s % segs < n_ctx else s // segs for s in range(bsz * segs)], jnp.int32)

    for l in range(depth):
        last = l == depth - 1
        mod3 = _mod_call(cc, w_mod[l], b_mod[l]).reshape(mod_rows, N_MOD, d)
        modseg = jnp.take(mod3, seg_rows((ctx_len + seq) // tm, n_ctx_tiles), axis=0)
        proj = _nm_matmul(a, modseg, g_pre_mix[l], _reorder_w_in(w_in[l]), shift_idx=0, tn=PROJ_TN, seg=tm,
                          max_seg=4)
        q, k, v = _qk_prep(proj, cos, sin, q_norm[l], k_norm[l], tm=tm)
        att = _attention(q, k, v, ctx_len=ctx_len, tq=tm, skip_ctx=last)
        conv = _conformer(proj, conv_w[l], conv_b[l], conv_ln_g[l], conv_ln_b[l], ctx_len=ctx_len, tm=tm,
                          skip_ctx=last)
        r, vv, kap, lw, kd, bd = _rwkv_prep(proj, shift_w[l], k_k[l], k_a[l], decay_w0[l], _pad_lora_up(decay_up[l]),
                                            iclr_a0[l], _pad_lora_up(iclr_up[l]), e, et, ctx_len=ctx_len, tm=tm)
        y = _rwkv_scan(r, vv, kap, lw, kd, bd, ctx_len=ctx_len)
        a = _merge(a, att, conv, y, r, vv, kd, proj, mod3, g_post_mix[l], wkv_gn_g[l], wkv_gn_b[l], r_k[l],
                   gate_up[l].astype(BF16), e, et, w_attn_o[l].astype(BF16), w_conv_o[l].astype(BF16),
                   w_rwkv_o[l].astype(BF16), w_out[l].astype(BF16), n_ctx_tiles=n_ctx_tiles, ctx_row=bsz, tm=tm,
                   skip_ctx=last)
        n_ctx = 0 if last else n_ctx_tiles
        if last:
            modseg = jnp.take(mod3, seg_rows(seq // tm, 0), axis=0)
        z = _nm_matmul(a, modseg, g_pre_ffn[l], w_ffn_up[l].astype(BF16), shift_idx=3, tn=D_FF // 2, seg=tm,
                       max_seg=4)
        a = _ffn_tail(a, z, ffn_conv_w[l], mod3, g_post_ffn[l], w_ffn_down[l].astype(BF16),
                      n_ctx_tiles=n_ctx, ctx_row=bsz, tm=tm)
    return a
```

```python
import functools
import math

import jax
import jax.numpy as jnp
from jax import lax
from jax.experimental import pallas as pl
from jax.experimental.pallas import tpu as pltpu

F32 = jnp.float32
BF16 = jnp.bfloat16

D_MODEL = 1024
GRID_W = 64
N_Q_HEADS = 8
N_KV_HEADS = 2
GQA_GROUP = N_Q_HEADS // N_KV_HEADS
HEAD_DIM = 128
N_FREQ = HEAD_DIM // 4
ROPE_THETA = 10000.0
CONV_K = 31
CONV_HALO = 16
RWKV_HEAD = 64
RWKV_HEADS = D_MODEL // RWKV_HEAD
DECAY_SCALE = math.exp(-0.5)
D_FF = 2816
N_MOD = 6
EPS = 1e-6
LN_EPS = 1e-5
GN_EPS = RWKV_HEAD * 1e-5
SOFTMAX_C = (HEAD_DIM ** -0.5) * math.log2(math.e)
LANES = 128
SUBLANES = 8
CHUNK = 64
N_PAIR = D_MODEL // LANES
SCAN_BATCH = 2
INV_BASE = 4
ROW_TILE = 256
CONV_ROWS = 32
MOD_TN = 1536
PROJ_TN = 2048
LORA_BLOCK = 512
ATTN_SPAN = 256
VMEM_LIMIT = 48 * 1024 * 1024

COL_Q = 0
COL_KV = 1024
COL_LORA = 1536
COL_GLU = 2048
COL_RKV = 4096
COL_GATE = 7168
N_IN_PAD = 10240


def _cparams(sem):
    return pltpu.CompilerParams(dimension_semantics=sem, vmem_limit_bytes=VMEM_LIMIT)


def _bdot(a, b):
    return jnp.dot(a.astype(BF16), b.astype(BF16), preferred_element_type=F32)


def _bdot_nt(a, b):
    return lax.dot_general(a.astype(BF16), b.astype(BF16), (((1,), (1,)), ((), ())),
                           preferred_element_type=F32)


def _bdot_tn(a, b):
    return lax.dot_general(a.astype(BF16), b.astype(BF16), (((0,), (0,)), ((), ())),
                           preferred_element_type=F32)


def _split2(x):
    hi = x.astype(BF16)
    lo = (x - hi.astype(F32)).astype(BF16)
    return hi, lo


def _dot3_presplit(a, bh, bl):
    ah, al = _split2(a)
    dot = functools.partial(jnp.dot, preferred_element_type=F32)
    return dot(ah, bh) + dot(al, bh) + dot(ah, bl)


def _dot3(a, b):
    return _dot3_presplit(a, *_split2(b))


def _sigmoid(x):
    return jax.nn.sigmoid(x)


def _head_sum(x, e_ref, et_ref):
    xh, xl = _split2(x)
    dot = functools.partial(jnp.dot, preferred_element_type=F32)
    s = dot(xh, e_ref[...]) + dot(xl, e_ref[...])
    sh, sl = _split2(s)
    return dot(sh, et_ref[...]) + dot(sl, et_ref[...])


def _mod_kernel(c_ref, w_ref, b_ref, o_ref):
    c = c_ref[...]
    o_ref[...] = _dot3(c * _sigmoid(c), w_ref[...]) + b_ref[...]


def _mod_call(cc, w_mod, b_mod):
    rows = cc.shape[0]
    n = w_mod.shape[1]
    tn = MOD_TN
    return pl.pallas_call(
        _mod_kernel,
        grid=(n // tn,),
        in_specs=[pl.BlockSpec((rows, D_MODEL), lambda j: (0, 0)),
                  pl.BlockSpec((D_MODEL, tn), lambda j: (0, j)),
                  pl.BlockSpec((1, tn), lambda j: (0, j))],
        out_specs=pl.BlockSpec((rows, tn), lambda j: (0, j)),
        out_shape=jax.ShapeDtypeStruct((rows, n), F32),
        compiler_params=_cparams(("parallel",)),
        name="mod",
    )(cc, w_mod, b_mod.reshape(1, n))


def _mod_row(n_ctx_tiles, ctx_row):
    return lambda b, i, *_: (jnp.where(i < n_ctx_tiles, ctx_row, b), 0, 0)


def _nm_matmul_kernel(a_ref, mod_ref, g_ref, w_ref, o_ref, h_ref, *, shift_idx, seg, nseg):
    @pl.when(pl.program_id(1) == 0)
    def _():
        for s in range(nseg):
            rows = slice(s * seg, (s + 1) * seg)
            x = a_ref[rows, :]
            y = x * lax.rsqrt(jnp.mean(x * x, axis=-1, keepdims=True) + EPS) * g_ref[...]
            h = y * (1.0 + mod_ref[s, shift_idx + 1:shift_idx + 2, :]) + mod_ref[s, shift_idx:shift_idx + 1, :]
            h_ref[rows, :] = h.astype(BF16)

    o_ref[...] = jnp.dot(h_ref[...], w_ref[...], preferred_element_type=F32).astype(o_ref.dtype)


def _nm_matmul(a, modseg, g, w, *, shift_idx, tn, seg, max_seg):
    bsz, tt, d = a.shape
    n = w.shape[1]
    rows = bsz * tt
    nseg = max(s for s in (4, 2, 1) if s <= max_seg and (rows // seg) % s == 0)
    tm = seg * nseg
    out = pl.pallas_call(
        functools.partial(_nm_matmul_kernel, shift_idx=shift_idx, seg=seg, nseg=nseg),
        grid=(rows // tm, n // tn),
        in_specs=[pl.BlockSpec((tm, d), lambda i, j: (i, 0)),
                  pl.BlockSpec((nseg, N_MOD, d), lambda i, j: (i, 0, 0)),
                  pl.BlockSpec((1, d), lambda i, j: (0, 0)),
                  pl.BlockSpec((d, tn), lambda i, j: (0, j))],
        out_specs=pl.BlockSpec((tm, tn), lambda i, j: (i, j)),
        out_shape=jax.ShapeDtypeStruct((rows, n), F32),
        scratch_shapes=[pltpu.VMEM((tm, d), BF16)],
        compiler_params=_cparams(("parallel", "arbitrary")),
        name="norm_mod_matmul",
    )(a.reshape(rows, d), modseg, g.reshape(1, d), w)
    return out.reshape(bsz, tt, n)


def _qk_prep_kernel(q_ref, kv_ref, cos_ref, sin_ref, qn_ref, kn_ref, qo_ref, ko_ref, vo_ref):
    cos = cos_ref[...]
    sin = sin_ref[...]
    lane = lax.broadcasted_iota(jnp.int32, cos.shape, 1)
    first = (lane & (N_FREQ)) == 0

    heads = ([(q_ref, h, qn_ref, qo_ref, SOFTMAX_C) for h in range(N_Q_HEADS)]
             + [(kv_ref, h, kn_ref, ko_ref, 1.0) for h in range(N_KV_HEADS)])
    sls = [slice(h * HEAD_DIM, (h + 1) * HEAD_DIM) for _, h, _, _, _ in heads]
    xs = [src[:, sl] for (src, _, _, _, _), sl in zip(heads, sls)]
    ms = [jnp.mean(x * x, axis=-1, keepdims=True) for x in xs]
    ys = [x * lax.rsqrt(m + EPS) * (g[...] * c) for x, m, (_, _, g, _, c) in zip(xs, ms, heads)]
    up = [pltpu.roll(y, LANES - N_FREQ, 1) for y in ys]
    dn = [pltpu.roll(y, N_FREQ, 1) for y in ys]
    for y, u, d, (_, _, _, dst, _), sl in zip(ys, up, dn, heads, sls):
        dst[:, sl] = (y * cos + jnp.where(first, u, d) * sin).astype(BF16)
    vo_ref[...] = kv_ref[:, N_KV_HEADS * HEAD_DIM:].astype(BF16)


def _qk_prep(proj, cos, sin, q_norm, k_norm, *, tm):
    bsz, tt, _ = proj.shape
    dq = N_Q_HEADS * HEAD_DIM
    dkv = N_KV_HEADS * HEAD_DIM
    return pl.pallas_call(
        _qk_prep_kernel,
        grid=(bsz, tt // tm),
        in_specs=[pl.BlockSpec((None, tm, dq), lambda b, i: (b, i, COL_Q // dq)),
                  pl.BlockSpec((None, tm, 2 * dkv), lambda b, i: (b, i, COL_KV // (2 * dkv))),
                  pl.BlockSpec((tm, HEAD_DIM), lambda b, i: (i, 0)),
                  pl.BlockSpec((tm, HEAD_DIM), lambda b, i: (i, 0)),
                  pl.BlockSpec((1, HEAD_DIM), lambda b, i: (0, 0)),
                  pl.BlockSpec((1, HEAD_DIM), lambda b, i: (0, 0))],
        out_specs=[pl.BlockSpec((None, tm, dq), lambda b, i: (b, i, 0)),
                   pl.BlockSpec((None, tm, dkv), lambda b, i: (b, i, 0)),
                   pl.BlockSpec((None, tm, dkv), lambda b, i: (b, i, 0))],
        out_shape=[jax.ShapeDtypeStruct((bsz, tt, dq), BF16),
                   jax.ShapeDtypeStruct((bsz, tt, dkv), BF16),
                   jax.ShapeDtypeStruct((bsz, tt, dkv), BF16)],
        compiler_params=_cparams(("parallel", "parallel")),
        name="qk_prep",
    )(proj, proj, cos, sin, q_norm.reshape(1, HEAD_DIM), k_norm.reshape(1, HEAD_DIM))


def _attn_kernel(q_ref, k_ref, v_ref, o_ref, *, n_ctx_tiles, ctx_len, skip_ctx):
    tq = q_ref.shape[0]
    tt = k_ref.shape[0]

    def attend(spans):
        q = jnp.concatenate([q_ref[:, g * HEAD_DIM:(g + 1) * HEAD_DIM] for g in range(GQA_GROUP)], axis=0)
        m = l = acc = None
        for start, size in spans:
            k = k_ref[start:start + size, :]
            v = v_ref[start:start + size, :]
            s = lax.dot_general(q, k, (((1,), (1,)), ((), ())), preferred_element_type=F32)
            mt = jnp.max(s, axis=-1, keepdims=True)
            if m is None:
                m = mt
                p = jnp.exp2(s - m)
                l = jnp.sum(p, axis=-1, keepdims=True)
                acc = jnp.dot(p.astype(BF16), v, preferred_element_type=F32)
            else:
                m_new = jnp.maximum(m, mt)
                alpha = jnp.exp2(m - m_new)
                p = jnp.exp2(s - m_new)
                l = alpha * l + jnp.sum(p, axis=-1, keepdims=True)
                acc = alpha * acc + jnp.dot(p.astype(BF16), v, preferred_element_type=F32)
                m = m_new
        o = acc / l
        for g in range(GQA_GROUP):
            o_ref[:, g * HEAD_DIM:(g + 1) * HEAD_DIM] = o[g * tq:(g + 1) * tq].astype(o_ref.dtype)

    x_spans = [(start, min(ATTN_SPAN, tt - start)) for start in range(0, tt, ATTN_SPAN)]
    if skip_ctx:
        attend(x_spans)
        return
    i = pl.program_id(2)

    @pl.when(i < n_ctx_tiles)
    def _():
        attend([(0, ctx_len)])

    @pl.when(i >= n_ctx_tiles)
    def _():
        attend(x_spans)


def _attention(q, k, v, *, ctx_len, tq, skip_ctx):
    bsz, tt, dq = q.shape
    gw = GQA_GROUP * HEAD_DIM
    off = ctx_len // tq if skip_ctx else 0
    return pl.pallas_call(
        functools.partial(_attn_kernel, n_ctx_tiles=ctx_len // tq, ctx_len=ctx_len, skip_ctx=skip_ctx),
        grid=(bsz, N_KV_HEADS, tt // tq - off),
        in_specs=[pl.BlockSpec((None, tq, gw), lambda b, h, i: (b, i + off, h)),
                  pl.BlockSpec((None, tt, HEAD_DIM), lambda b, h, i: (b, 0, h)),
                  pl.BlockSpec((None, tt, HEAD_DIM), lambda b, h, i: (b, 0, h))],
        out_specs=pl.BlockSpec((None, tq, gw), lambda b, h, i: (b, i, h)),
        out_shape=jax.ShapeDtypeStruct((bsz, tt - off * tq, dq), BF16),
        compiler_params=_cparams(("parallel", "parallel", "arbitrary")),
        name="attention",
    )(q, k, v)


def _halo_specs(tm, halo, tt, col_block, width, off=0):
    per = tm // halo
    last = tt // halo - 1
    prev = pl.BlockSpec((None, halo, width), lambda b, i: (b, jnp.maximum((i + off) * per - 1, 0), col_block))
    nxt = pl.BlockSpec((None, halo, width), lambda b, i: (b, jnp.minimum((i + off + 1) * per, last), col_block))
    return prev, nxt


def _edge_flags(i, n_ctx_tiles, n_tiles):
    first = jnp.logical_or(i == 0, i == n_ctx_tiles)
    last = jnp.logical_or(i == n_ctx_tiles - 1, i == n_tiles - 1)
    return first, last


def _conformer_kernel(cur_ref, prev_ref, next_ref, w_ref, cb_ref, lg_ref, lb_ref,
                      o_ref, buf_ref, sh_ref, *, n_ctx_tiles, n_tiles, tm, rc, tile0):
    first, last = _edge_flags(pl.program_id(1) + tile0, n_ctx_tiles, n_tiles)
    h = CONV_HALO
    d = D_MODEL

    def glu(ref):
        return ref[:, 0:d] * _sigmoid(ref[:, d:2 * d])

    buf_ref[0:h, :] = jnp.where(first, 0.0, glu(prev_ref))
    buf_ref[h:h + tm, :] = glu(cur_ref)
    buf_ref[h + tm:h + tm + h, :] = jnp.where(last, 0.0, glu(next_ref))
    n_sh = sh_ref.shape[1]
    for s in range(1, SUBLANES):
        sh_ref[s - 1] = buf_ref[s:s + n_sh, :]
    lead = h - CONV_K // 2
    for c in range(tm // rc):
        acc = jnp.zeros((rc, D_MODEL), F32)
        for j in range(CONV_K):
            s = (j + lead) % SUBLANES
            r0 = c * rc + (j + lead) - s
            tap = buf_ref[r0:r0 + rc, :] if s == 0 else sh_ref[s - 1, r0:r0 + rc, :]
            acc = acc + tap * jnp.concatenate([w_ref[j]] * (rc // SUBLANES), axis=0)
        y = acc + cb_ref[...]
        mu = jnp.mean(y, axis=-1, keepdims=True)
        dlt = y - mu
        var = jnp.mean(dlt * dlt, axis=-1, keepdims=True)
        z = dlt * lax.rsqrt(var + LN_EPS) * lg_ref[...] + lb_ref[...]
        o_ref[c * rc:(c + 1) * rc, :] = (z * _sigmoid(z)).astype(o_ref.dtype)


def _conformer(proj, conv_w, conv_b, ln_g, ln_b, *, ctx_len, tm, skip_ctx):
    bsz, tt, _ = proj.shape
    d = D_MODEL
    cg = COL_GLU // (2 * d)
    off = ctx_len // tm if skip_ctx else 0
    prev, nxt = _halo_specs(tm, CONV_HALO, tt, cg, 2 * d, off)
    kern = functools.partial(_conformer_kernel, n_ctx_tiles=ctx_len // tm, n_tiles=tt // tm, tm=tm, rc=CONV_ROWS,
                             tile0=off)
    cur = pl.BlockSpec((None, tm, 2 * d), lambda b, i: (b, i + off, cg), pipeline_mode=pl.Buffered(3))
    out_blk = pl.BlockSpec((None, tm, d), lambda b, i: (b, i, 0))

    def outer(p_hbm, w_ref, cb_ref, lg_ref, lb_ref, o_hbm, buf_ref, sh_ref):
        def body(cur_ref, prev_ref, next_ref, o_ref):
            kern(cur_ref, prev_ref, next_ref, w_ref, cb_ref, lg_ref, lb_ref, o_ref, buf_ref, sh_ref)

        pltpu.emit_pipeline(body, grid=(bsz, tt // tm - off), in_specs=[cur, prev, nxt],
                            out_specs=[out_blk])(p_hbm, p_hbm, p_hbm, o_hbm)

    whole = lambda: pl.BlockSpec(memory_space=pltpu.VMEM)
    return pl.pallas_call(
        outer,
        in_specs=[pl.BlockSpec(memory_space=pl.ANY), whole(), whole(), whole(), whole()],
        out_specs=pl.BlockSpec(memory_space=pl.ANY),
        out_shape=jax.ShapeDtypeStruct((bsz, tt - off * tm, d), BF16),
        scratch_shapes=[pltpu.VMEM((tm + 2 * CONV_HALO, d), F32),
                        pltpu.VMEM((SUBLANES - 1, tm + 2 * CONV_HALO - SUBLANES, d), F32)],
        compiler_params=pltpu.CompilerParams(vmem_limit_bytes=VMEM_LIMIT),
        name="conformer",
    )(proj, jnp.broadcast_to(conv_w[:, None, :], (CONV_K, SUBLANES, d)),
      conv_b.reshape(1, d), ln_g.reshape(1, d), ln_b.reshape(1, d))


def _rwkv_prep_kernel(rc_ref, kc_ref, vc_ref, rp_ref, kp_ref, vp_ref, rn_ref, kn_ref, vn_ref, lora_ref,
                      sw_ref, kk_ref, ka_ref, w0_ref, dup_ref, a0_ref, iup_ref, e_ref, et_ref,
                      r_ref, v_ref, kap_ref, lw_ref, kd_ref, bd_ref,
                      *, n_ctx_tiles, n_tiles, tm):
    first, last = _edge_flags(pl.program_id(1), n_ctx_tiles, n_tiles)
    h = SUBLANES
    d = D_MODEL

    row = lax.broadcasted_iota(jnp.int32, (tm, 1), 0)

    def shift(cur_ref, prev_ref, next_ref, col):
        cur = cur_ref[...]
        before = jnp.where(first, 0.0, prev_ref[h - 1:h, :])
        after = jnp.where(last, 0.0, next_ref[0:1, :])
        prv = jnp.where(row == 0, before, pltpu.roll(cur, 1, 0))
        nxt = jnp.where(row == tm - 1, after, pltpu.roll(cur, tm - 1, 0))
        sl = slice(col * d, (col + 1) * d)
        return prv * sw_ref[0:1, sl] + cur * sw_ref[1:2, sl] + nxt * sw_ref[2:3, sl]

    r = shift(rc_ref, rp_ref, rn_ref, 0)
    k = shift(kc_ref, kp_ref, kn_ref, 1)
    v = shift(vc_ref, vp_ref, vn_ref, 2)
    r_ref[...] = r
    v_ref[...] = v
    kk = k * kk_ref[...]
    ss = _head_sum(kk * kk, e_ref, et_ref)
    kap = kk * lax.rsqrt(jnp.maximum(ss, 1e-12))
    kap_ref[...] = kap
    tw = jnp.tanh(lora_ref[:, 0:LANES])
    la = lora_ref[:, LANES:2 * LANES]
    for dr in range(2):
        z = w0_ref[dr:dr + 1, :] + _dot3_presplit(tw, dup_ref[dr, 0], dup_ref[dr, 1])
        lw_ref[dr] = -DECAY_SCALE * _sigmoid(z)
        a = _sigmoid(a0_ref[dr:dr + 1, :] + _dot3_presplit(la, iup_ref[dr, 0], iup_ref[dr, 1]))
        kd_ref[dr] = k * (1.0 + (a - 1.0) * ka_ref[...])
        bd_ref[dr] = a * kap


def _rwkv_prep(proj, shift_w, k_k, k_a, decay_w0, decay_up_pad, iclr_a0, iclr_up_pad, e, et, *, ctx_len, tm):
    bsz, tt, _ = proj.shape
    d = D_MODEL
    c0 = COL_RKV // d
    cur = lambda c: pl.BlockSpec((None, tm, d), lambda b, i: (b, i, c))
    halos = [_halo_specs(tm, SUBLANES, tt, c0 + c, d) for c in range(3)]
    full = lambda shape: pl.BlockSpec(shape, lambda b, i: (0,) * len(shape))
    out1 = pl.BlockSpec((None, tm, d), lambda b, i: (b, i, 0))
    out2 = pl.BlockSpec((2, None, tm, d), lambda b, i: (0, b, i, 0))
    s1 = jax.ShapeDtypeStruct((bsz, tt, d), F32)
    s2 = jax.ShapeDtypeStruct((2, bsz, tt, d), F32)
    return pl.pallas_call(
        functools.partial(_rwkv_prep_kernel, n_ctx_tiles=ctx_len // tm, n_tiles=tt // tm, tm=tm),
        grid=(bsz, tt // tm),
        in_specs=[cur(c0), cur(c0 + 1), cur(c0 + 2),
                  halos[0][0], halos[1][0], halos[2][0], halos[0][1], halos[1][1], halos[2][1],
                  pl.BlockSpec((None, tm, LORA_BLOCK), lambda b, i: (b, i, COL_LORA // LORA_BLOCK)),
                  full((3, 3 * d)), full((1, d)), full((1, d)), full((2, d)), full((2, 2, LANES, d)),
                  full((2, d)), full((2, 2, LANES, d)), full((d, LANES)), full((LANES, d))],
        out_specs=[out1, out1, out1, out2, out2, out2],
        out_shape=[s1, s1, s1, s2, s2, s2],
        compiler_params=_cparams(("parallel", "parallel")),
        name="rwkv_prep",
    )(proj, proj, proj, proj, proj, proj, proj, proj, proj, proj,
      shift_w, k_k.reshape(1, d), k_a.reshape(1, d), decay_w0, decay_up_pad, iclr_a0, iclr_up_pad, e, et)


def _scan_kernel(r_ref, v_ref, kap_ref, lw_ref, k_ref, b_ref, y_ref, h_ref):
    c = CHUNK
    c2 = 2 * c
    sgn = 1 - 2 * pl.program_id(1)

    @pl.when(pl.program_id(2) == 0)
    def _():
        h_ref[...] = jnp.zeros_like(h_ref)

    row = lax.broadcasted_iota(jnp.int32, (c, c), 0)
    col = lax.broadcasted_iota(jnp.int32, (c, c), 1)
    incl = jnp.where((col - row) * sgn <= 0, 1.0, 0.0).astype(BF16)
    dot = functools.partial(jnp.dot, preferred_element_type=F32)
    nb = lw_ref.shape[0]
    cum = []
    for bi in range(nb):
        lw = lw_ref[bi]
        hi = lw.astype(BF16)
        rem = lw - hi.astype(F32)
        mid = rem.astype(BF16)
        lo = (rem - mid.astype(F32)).astype(BF16)
        cum.append(dot(incl, hi) + dot(incl, mid) + dot(incl, lo))

    prow = lax.broadcasted_iota(jnp.int32, (c, c2), 0)
    pcol = lax.broadcasted_iota(jnp.int32, (c, c2), 1)
    order = ((pcol & (c - 1)) - prow) * sgn
    strict = order < 0
    upto = order <= 0
    eye = order == 0
    head0 = pcol < RWKV_HEAD

    def stack(x):
        return jnp.concatenate([jnp.where(head0, x, 0.0), jnp.where(head0, 0.0, x)], axis=0)

    def pack(x):
        return jnp.where(head0, x[0:c], x[c:c2])

    units = [(bi, slice(p * LANES, (p + 1) * LANES)) for bi in range(nb) for p in range(N_PAIR)]
    pairs = range(len(units))
    kt, rt, vs, kend, bend, ptot, a = [], [], [], [], [], [], []
    for bi, sl in units:
        cum_p = cum[bi][:, sl]
        lw_p = lw_ref[bi, :, sl]
        tot_p = jnp.sum(lw_p, axis=0, keepdims=True)
        p_inv = jnp.exp(-cum_p)
        p_end = jnp.exp(tot_p - cum_p)
        k = k_ref[bi, :, sl]
        b = b_ref[bi, :, sl]
        kt.append(kap_ref[bi, :, sl] * jnp.exp(cum_p - lw_p))
        rt.append(r_ref[bi, :, sl] * jnp.exp(cum_p))
        vs.append(v_ref[bi, :, sl])
        kend.append(k * p_end)
        bend.append(b * p_end)
        ptot.append(jnp.exp(tot_p))
        a.append(_bdot_nt(jnp.concatenate([kt[-1], rt[-1]], axis=0),
                          jnp.concatenate([stack(b * p_inv), stack(k * p_inv)], axis=0)))
    a_ab = [jnp.where(strict, a[p][0:c, 0:c2], 0.0) for p in pairs]
    a_ak = [jnp.where(strict, a[p][0:c, c2:2 * c2], 0.0) for p in pairs]
    a_rb = [jnp.where(upto, a[p][c:c2, 0:c2], 0.0) for p in pairs]
    a_rk = [jnp.where(upto, a[p][c:c2, c2:2 * c2], 0.0) for p in pairs]

    blk = (pcol & (c - 1)) ^ prow
    ident = jnp.where(eye, 1.0, 0.0)
    l0 = [jnp.where(blk < INV_BASE, a_ab[p], 0.0) for p in pairs]
    sq = [_bdot(l0[p], stack(l0[p])) for p in pairs]
    t = [ident - l0[p] for p in pairs]
    t = [t[p] + _bdot(t[p], stack(sq[p])) for p in pairs]
    s_blk = INV_BASE
    while s_blk < c:
        off = jnp.logical_and(blk >= s_blk, blk < 2 * s_blk)
        tl = [_bdot(t[p], stack(jnp.where(off, a_ab[p], 0.0))) for p in pairs]
        t = [t[p] - _bdot(tl[p], stack(t[p])) for p in pairs]
        s_blk *= 2

    asv = [_bdot(jnp.concatenate([a_ak[p], a_rk[p]], axis=0), stack(vs[p])) for p in pairs]
    wu = [_bdot(t[p], jnp.concatenate([stack(kt[p]), stack(asv[p][0:c])], axis=1)) for p in pairs]
    w = [wu[p][:, 0:c2] for p in pairs]
    u0 = [wu[p][:, c2:2 * c2] for p in pairs]
    arb = [_bdot(a_rb[p], jnp.concatenate([stack(w[p]), stack(u0[p])], axis=1)) for p in pairs]
    y0 = [asv[p][c:c2] - arb[p][:, c2:2 * c2] for p in pairs]
    y1 = [rt[p] - arb[p][:, 0:c2] for p in pairs]
    m = [jnp.where(eye, ptot[p], 0.0) - pack(_bdot_tn(bend[p], w[p])) for p in pairs]
    nn = [pack(_bdot_tn(jnp.concatenate([kend[p], -bend[p]], axis=0), jnp.concatenate([vs[p], u0[p]], axis=0)))
          for p in pairs]
    for p in pairs:
        h0 = h_ref[p]
        hh = h0.astype(BF16)
        hl = h0 - hh.astype(F32)
        mh, ml = _split2(m[p])
        top = _bdot(jnp.concatenate([mh, ml, y1[p].astype(BF16)], axis=0), stack(hh.astype(F32)))
        y_ref[units[p][0], :, units[p][1]] = top[c2:c2 + c] + y0[p]
        h_ref[p] = top[0:c] + top[c:c2] + _bdot(mh, stack(hl)) + nn[p]


def _rwkv_scan(r, v, kap, lw, kd, bd, *, ctx_len):
    bsz, tt, d = r.shape
    nc = tt // CHUNK
    ncc = ctx_len // CHUNK

    def chunk(dr, s):
        return jnp.where(dr == 0, s, jnp.where(s < ncc, ncc - 1 - s, nc + ncc - 1 - s))

    ub = SCAN_BATCH if bsz % SCAN_BATCH == 0 else 1
    shared = pl.BlockSpec((ub, CHUNK, d), lambda b, dr, s: (b, chunk(dr, s), 0))
    per_dir = pl.BlockSpec((None, ub, CHUNK, d), lambda b, dr, s: (dr, b, chunk(dr, s), 0))
    return pl.pallas_call(
        _scan_kernel,
        grid=(bsz // ub, 2, nc),
        in_specs=[shared, shared, shared, per_dir, per_dir, per_dir],
        out_specs=per_dir,
        out_shape=jax.ShapeDtypeStruct((2, bsz, tt, d), F32),
        scratch_shapes=[pltpu.VMEM((ub * N_PAIR, CHUNK, LANES), F32)],
        compiler_params=_cparams(("parallel", "parallel", "arbitrary")),
        name="rwkv_scan",
    )(r, v, kap, lw, kd, bd)


def _merge_kernel(a_ref, att_ref, conv_ref, y_ref, r_ref, v_ref, kd_ref, lora_ref, ga_ref, gc_ref, gr_ref,
                  mod_ref, g_ref, gg_ref, gb_ref, rk_ref, gup_ref, e_ref, et_ref,
                  wa_ref, wc_ref, wr_ref, wo_ref, o_ref):
    dot = functools.partial(jnp.dot, preferred_element_type=F32)
    inv = 1.0 / RWKV_HEAD
    y = y_ref[0] + y_ref[1]
    mu = _head_sum(y, e_ref, et_ref) * inv
    dlt = y - mu
    var = _head_sum(dlt * dlt, e_ref, et_ref) * inv
    yn = dlt * lax.rsqrt(var + GN_EPS) * gg_ref[...] + gb_ref[...]
    bonus = _head_sum(r_ref[...] * (kd_ref[0] + kd_ref[1]) * rk_ref[...], e_ref, et_ref) * v_ref[...]
    gate = _bdot(_sigmoid(lora_ref[:, 2 * LANES:3 * LANES]), gup_ref[...])
    rw = ((yn + bonus) * gate).astype(BF16)
    m = (_sigmoid(ga_ref[...]) * dot(att_ref[...], wa_ref[...])
         + _sigmoid(gc_ref[...]) * dot(conv_ref[...], wc_ref[...])
         + _sigmoid(gr_ref[...]) * dot(rw, wr_ref[...]))
    z = dot(m.astype(BF16), wo_ref[...])
    zn = z * lax.rsqrt(jnp.mean(z * z, axis=-1, keepdims=True) + EPS) * g_ref[...]
    o_ref[...] = a_ref[...] + mod_ref[2:3, :] * zn


def _merge(a, att, conv, y, r, v, kd, proj, mod3, g, gn_g, gn_b, r_k, gate_up, e, et, wa, wc, wr, wo,
           *, n_ctx_tiles, ctx_row, tm, skip_ctx):
    bsz, tt, d = a.shape
    cg = COL_GATE // d
    off = n_ctx_tiles if skip_ctx else 0
    loc = pl.BlockSpec((None, tm, d), lambda b, i: (b, i, 0))
    one = pl.BlockSpec((None, tm, d), lambda b, i: (b, i + off, 0))
    two = pl.BlockSpec((2, None, tm, d), lambda b, i: (0, b, i + off, 0))
    gate = lambda c: pl.BlockSpec((None, tm, d), lambda b, i: (b, i + off, cg + c))
    full = lambda shape: pl.BlockSpec(shape, lambda b, i: (0,) * len(shape), pipeline_mode=pl.Buffered(1))
    return pl.pallas_call(
        _merge_kernel,
        grid=(bsz, tt // tm - off),
        in_specs=[one, loc, loc, two, one, one, two,
                  pl.BlockSpec((None, tm, LORA_BLOCK), lambda b, i: (b, i + off, COL_LORA // LORA_BLOCK)),
                  gate(0), gate(1), gate(2),
                  pl.BlockSpec((None, N_MOD, d), _mod_row(n_ctx_tiles - off, ctx_row)),
                  full((1, d)), full((1, d)), full((1, d)), full((1, d)), full((LANES, d)),
                  full((d, LANES)), full((LANES, d)),
                  full((d, d)), full((d, d)), full((d, d)), full((d, d))],
        out_specs=loc,
        out_shape=jax.ShapeDtypeStruct((bsz, tt - off * tm, d), F32),
        compiler_params=_cparams(("parallel", "parallel")),
        name="merge",
    )(a, att, conv, y, r, v, kd, proj, proj, proj, proj, mod3, g.reshape(1, d), gn_g.reshape(1, d),
      gn_b.reshape(1, d), r_k.reshape(1, d), gate_up, e, et, wa, wc, wr, wo)


def _ffn_tail_kernel(a_ref, zc_ref, zp_ref, zn_ref, cw_ref, mod_ref, g_ref, wd_ref, o_ref,
                     *, n_ctx_tiles, n_tiles, tm):
    first, last = _edge_flags(pl.program_id(1), n_ctx_tiles, n_tiles)
    h = SUBLANES
    row = lax.broadcasted_iota(jnp.int32, (tm, 1), 0)

    def conv(sl):
        cur = zc_ref[:, sl]
        before = jnp.where(first, 0.0, zp_ref[h - 1:h, sl])
        after = jnp.where(last, 0.0, zn_ref[0:1, sl])
        prv = jnp.where(row == 0, before, pltpu.roll(cur, 1, 0))
        nxt = jnp.where(row == tm - 1, after, pltpu.roll(cur, tm - 1, 0))
        return prv * cw_ref[0:1, sl] + cur * cw_ref[1:2, sl] + nxt * cw_ref[2:3, sl]

    gate = conv(slice(0, D_FF))
    val = conv(slice(D_FF, 2 * D_FF))
    u = (gate * _sigmoid(gate) * val).astype(BF16)
    z = jnp.dot(u, wd_ref[...], preferred_element_type=F32)
    zn = z * lax.rsqrt(jnp.mean(z * z, axis=-1, keepdims=True) + EPS) * g_ref[...]
    o_ref[...] = a_ref[...] + mod_ref[5:6, :] * zn


def _ffn_tail(a, z, conv_w, mod3, g, wd, *, n_ctx_tiles, ctx_row, tm):
    bsz, tt, d = a.shape
    f2 = 2 * D_FF
    prev, nxt = _halo_specs(tm, SUBLANES, tt, 0, f2)
    one = pl.BlockSpec((None, tm, d), lambda b, i: (b, i, 0))
    return pl.pallas_call(
        functools.partial(_ffn_tail_kernel, n_ctx_tiles=n_ctx_tiles, n_tiles=tt // tm, tm=tm),
        grid=(bsz, tt // tm),
        in_specs=[one, pl.BlockSpec((None, tm, f2), lambda b, i: (b, i, 0)), prev, nxt,
                  pl.BlockSpec((3, f2), lambda b, i: (0, 0)),
                  pl.BlockSpec((None, N_MOD, d), _mod_row(n_ctx_tiles, ctx_row)),
                  pl.BlockSpec((1, d), lambda b, i: (0, 0)),
                  pl.BlockSpec((D_FF, d), lambda b, i: (0, 0))],
        out_specs=one,
        out_shape=jax.ShapeDtypeStruct((bsz, tt, d), F32),
        compiler_params=_cparams(("parallel", "parallel")),
        name="ffn_tail",
    )(a, z, z, z, conv_w, mod3, g.reshape(1, d), wd)


def _rope_tables(ctx_len, seq):
    rows = seq // GRID_W
    row = jnp.repeat(jnp.arange(rows, dtype=F32), GRID_W)
    col = jnp.tile(jnp.arange(GRID_W, dtype=F32), rows)
    inv_freq = ROPE_THETA ** (-jnp.arange(N_FREQ, dtype=F32) / N_FREQ)
    ang_r = row[:, None] * inv_freq
    ang_c = col[:, None] * inv_freq
    cos = jnp.concatenate([jnp.cos(ang_r), jnp.cos(ang_r), jnp.cos(ang_c), jnp.cos(ang_c)], axis=-1)
    sin = jnp.concatenate([-jnp.sin(ang_r), jnp.sin(ang_r), -jnp.sin(ang_c), jnp.sin(ang_c)], axis=-1)
    cos = jnp.concatenate([jnp.ones((ctx_len, HEAD_DIM), F32), cos], axis=0)
    sin = jnp.concatenate([jnp.zeros((ctx_len, HEAD_DIM), F32), sin], axis=0)
    return cos, sin


def _reorder_w_in(w):
    qkv = w[:, 0:1536]
    glu = w[:, 1536:3584]
    rkv = w[:, 3584:6656]
    lora = w[:, 6656:7040]
    gates = w[:, 7040:10112]
    pad = jnp.zeros((w.shape[0], LANES), w.dtype)
    return jnp.concatenate([qkv, lora, pad, glu, rkv, gates], axis=1).astype(BF16)


def _pad_lora_up(up):
    z = jnp.zeros_like(up[0])
    w = jnp.stack([jnp.concatenate([up[0], z], axis=0), jnp.concatenate([z, up[1]], axis=0)])
    hi = w.astype(BF16)
    return jnp.stack([hi, (w - hi.astype(F32)).astype(BF16)], axis=1)


def kernel(x, c, ctx, c_ctx, w_mod, b_mod, g_pre_mix, g_post_mix, g_pre_ffn, g_post_ffn, w_in, q_norm, k_norm,
           w_attn_o, conv_w, conv_b, conv_ln_g, conv_ln_b, w_conv_o, shift_w, decay_w0, decay_up, iclr_a0,
           iclr_up, gate_up, k_k, k_a, r_k, wkv_gn_g, wkv_gn_b, w_rwkv_o, w_out, w_ffn_up, ffn_conv_w, w_ffn_down):
    bsz, seq, d = x.shape
    ctx_len = ctx.shape[1]
    depth = w_mod.shape[0]
    tm = min(ROW_TILE, ctx_len)
    assert d == D_MODEL and ctx_len % tm == 0 and seq % tm == 0 and ctx_len % CHUNK == 0 and seq % CHUNK == 0
    n_ctx_tiles = ctx_len // tm

    a = jnp.concatenate([ctx, x], axis=1)
    mod_rows = -(-(bsz + 1) // SUBLANES) * SUBLANES
    cc = jnp.zeros((mod_rows, d), F32).at[:bsz].set(c).at[bsz].set(c_ctx)
    cos, sin = _rope_tables(ctx_len, seq)
    head = jnp.arange(d, dtype=jnp.int32) // RWKV_HEAD
    e = (head[:, None] == jnp.arange(LANES, dtype=jnp.int32)[None, :]).astype(BF16)
    et = e.T
    def seg_rows(segs, n_ctx):
        return jnp.array([bsz if s % segs < n_ctx else s // segs for s in range(bsz * segs)], jnp.int32)

    for l in range(depth):
        last = l == depth - 1
        mod3 = _mod_call(cc, w_mod[l], b_mod[l]).reshape(mod_rows, N_MOD, d)
        modseg = jnp.take(mod3, seg_rows((ctx_len + seq) // tm, n_ctx_tiles), axis=0)
        proj = _nm_matmul(a, modseg, g_pre_mix[l], _reorder_w_in(w_in[l]), shift_idx=0, tn=PROJ_TN, seg=tm,
                          max_seg=4)
        q, k, v = _qk_prep(proj, cos, sin, q_norm[l], k_norm[l], tm=tm)
        att = _attention(q, k, v, ctx_len=ctx_len, tq=tm, skip_ctx=last)
        conv = _conformer(proj, conv_w[l], conv_b[l], conv_ln_g[l], conv_ln_b[l], ctx_len=ctx_len, tm=tm,
                          skip_ctx=last)
        r, vv, kap, lw, kd, bd = _rwkv_prep(proj, shift_w[l], k_k[l], k_a[l], decay_w0[l], _pad_lora_up(decay_up[l]),
                                            iclr_a0[l], _pad_lora_up(iclr_up[l]), e, et, ctx_len=ctx_len, tm=tm)
        y = _rwkv_scan(r, vv, kap, lw, kd, bd, ctx_len=ctx_len)
        a = _merge(a, att, conv, y, r, vv, kd, proj, mod3, g_post_mix[l], wkv_gn_g[l], wkv_gn_b[l], r_k[l],
                   gate_up[l].astype(BF16), e, et, w_attn_o[l].astype(BF16), w_conv_o[l].astype(BF16),
                   w_rwkv_o[l].astype(BF16), w_out[l].astype(BF16), n_ctx_tiles=n_ctx_tiles, ctx_row=bsz, tm=tm,
                   skip_ctx=last)
        n_ctx = 0 if last else n_ctx_tiles
        if last:
            modseg = jnp.take(mod3, seg_rows(seq // tm, 0), axis=0)
        z = _nm_matmul(a, modseg, g_pre_ffn[l], w_ffn_up[l].astype(BF16), shift_idx=3, tn=D_FF // 2, seg=tm,
                       max_seg=4)
        a = _ffn_tail(a, z, ffn_conv_w[l], mod3, g_post_ffn[l], w_ffn_down[l].astype(BF16),
                      n_ctx_tiles=n_ctx, ctx_row=bsz, tm=tm)
    return a
```
